```python
import functools
import jax, jax.numpy as jnp
from jax import lax
import numpy as np

D_MODEL = 2048
BATCH = 2
SEQ = 4096
DEPTH = 4
DEC_BATCH = 8
DEC_SEQ = 4
PAST_LEN = 16384
PAGE_SIZE = 128

HEAD_DIM = 128
SB_HEADS = D_MODEL // (2 * HEAD_DIM)
SB_KV_HEADS = 2
DSA_HEADS = D_MODEL // (2 * HEAD_DIM)
DSA_KV_HEADS = 2
IDX_HEADS = 16
IDX_DIM = 64
TOPK_MAX = 256
ROPE_THETA = 500000.0
Q_BLOCK = 128
D_FF = 128 * ((8 * D_MODEL // 3 + 127) // 128)
CONV_W = 3
PLE_DIM = 256
LN_EPS = 1e-5
DEEPNORM_ALPHA = (2 * DEPTH) ** 0.25
DEEPNORM_BETA = (8 * DEPTH) ** -0.25
IN_SIZES = (SB_HEADS * HEAD_DIM, SB_KV_HEADS * HEAD_DIM, SB_KV_HEADS * HEAD_DIM,
            DSA_HEADS * HEAD_DIM, DSA_KV_HEADS * HEAD_DIM, DSA_KV_HEADS * HEAD_DIM,
            IDX_HEADS * IDX_DIM, IDX_DIM, IDX_HEADS, D_MODEL, D_MODEL)
IN_WIDTH = sum(IN_SIZES)

kernel_name = "stickbreak_dsa_parallel_deepnorm_convffn_step"

F32 = jnp.float32


def layer_norm(x, g, b):
    xf = x.astype(F32)
    mu = jnp.mean(xf, axis=-1, keepdims=True)
    var = jnp.mean(jnp.square(xf - mu), axis=-1, keepdims=True)
    return ((xf - mu) * lax.rsqrt(var + LN_EPS) * g.astype(F32) + b.astype(F32)).astype(x.dtype)


def partial_rope(x, pos):
    d = x.shape[-1]
    rot = d // 4
    half = rot // 2
    inv = ROPE_THETA ** (-(2.0 * jnp.arange(half, dtype=F32)) / rot)
    ang = pos.astype(F32)[:, None] * inv[None, :]
    cos = jnp.cos(ang)[:, None, :]
    sin = jnp.sin(ang)[:, None, :]
    xf = x.astype(F32)
    x1, x2, xp = xf[..., :half], xf[..., half:rot], xf[..., rot:]
    out = jnp.concatenate([x1 * cos - x2 * sin, x2 * cos + x1 * sin, xp], axis=-1)
    return out.astype(x.dtype)


def project_in(x, w_in, pos):
    b, t = x.shape[:2]
    splits = [int(s) for s in np.cumsum(IN_SIZES)[:-1]]
    qa, ka, va, qb, kb, vb, qi, ki, wi, ga, gb = jnp.split(x @ w_in, splits, axis=-1)
    qa = qa.reshape(b, t, SB_KV_HEADS, SB_HEADS // SB_KV_HEADS, HEAD_DIM)
    ka = ka.reshape(b, t, SB_KV_HEADS, HEAD_DIM)
    va = va.reshape(b, t, SB_KV_HEADS, HEAD_DIM)
    qb = partial_rope(qb.reshape(b, t, DSA_HEADS, HEAD_DIM), pos)
    qb = qb.reshape(b, t, DSA_KV_HEADS, DSA_HEADS // DSA_KV_HEADS, HEAD_DIM)
    kb = partial_rope(kb.reshape(b, t, DSA_KV_HEADS, HEAD_DIM), pos)
    vb = vb.reshape(b, t, DSA_KV_HEADS, HEAD_DIM)
    qi = partial_rope(qi.reshape(b, t, IDX_HEADS, IDX_DIM), pos)
    ki = partial_rope(ki.reshape(b, t, 1, IDX_DIM), pos)[:, :, 0]
    return qa, ka, va, qb, kb, vb, qi, ki, wi, ga, gb


def _to_blocks(a, blk, nb):
    pad = nb * blk - a.shape[1]
    a = jnp.pad(a, [(0, 0), (0, pad)] + [(0, 0)] * (a.ndim - 2))
    a = a.reshape(a.shape[0], nb, blk, *a.shape[2:])
    return jnp.moveaxis(a, 1, 0)


def _from_blocks(o, t):
    o = jnp.moveaxis(o, 0, 1)
    o = o.reshape(o.shape[0], -1, *o.shape[3:])
    return o[:, :t]


def blocked_map(fn, q_arrays, q_pos):
    t = q_pos.shape[0]
    blk = min(Q_BLOCK, t)
    nb = -(-t // blk)
    qs = tuple(_to_blocks(a, blk, nb) for a in q_arrays)
    pb = jnp.pad(q_pos, (0, nb * blk - t), mode="edge").reshape(nb, blk)
    out = lax.map(lambda args: fn(*args), (*qs, pb))
    return _from_blocks(out, t)


def stick_breaking_block(q, pos, k, v, k_pos):
    z = jnp.einsum('bqkgd,bskd->bkgqs', q.astype(F32), k) * (HEAD_DIM ** -0.5)
    mask = k_pos[None, :] < pos[:, None]
    log_1m = jnp.where(mask, jax.nn.log_sigmoid(-z), 0.0)
    after = lax.cumsum(log_1m, axis=z.ndim - 1, reverse=True) - log_1m
    w = jnp.where(mask, jnp.exp(jax.nn.log_sigmoid(z) + after), 0.0)
    return jnp.einsum('bkgqs,bskd->bqkgd', w, v)


def _gather_rows(a, idx):
    return jax.vmap(lambda aa, ii: aa[ii])(a, idx)


def dsa_block(q, qi, wi, pos, k, v, ki, k_pos, n_top):
    s_idx = jnp.einsum('bqhd,bsd->bqhs', qi.astype(F32), ki) * (IDX_DIM ** -0.5)
    score = jnp.einsum('bqhs,bqh->bqs', jax.nn.relu(s_idx), wi.astype(F32) * (IDX_HEADS ** -0.5))
    mask = k_pos[None, :] <= pos[:, None]
    score = jnp.where(mask[None], score, -jnp.inf)
    top_val, top_idx = lax.top_k(score, n_top)
    valid = jnp.isfinite(top_val)
    kg = _gather_rows(k, top_idx)
    vg = _gather_rows(v, top_idx)
    logits = jnp.einsum('bqkgd,bqnkd->bqkgn', q.astype(F32), kg) * (HEAD_DIM ** -0.5)
    logits = jnp.where(valid[:, :, None, None, :], logits, -jnp.inf)
    probs = jax.nn.softmax(logits, axis=-1)
    return jnp.einsum('bqkgn,bqnkd->bqkgd', probs, vg)


def conv_ffn(x, w_ffn_in, conv_w, conv_b, w_ffn_out, conv_state):
    t = x.shape[1]
    a, up = jnp.split(x @ w_ffn_in, [D_FF], axis=-1)
    a_ext = jnp.concatenate([conv_state.astype(a.dtype), a], axis=1)
    c = conv_b
    for i in range(CONV_W):
        c = c + conv_w[i] * a_ext[:, i:i + t]
    h = jax.nn.gelu(c, approximate=True) * up
    return h @ w_ffn_out, a_ext[:, a_ext.shape[1] - (CONV_W - 1):]


def layer_step(x, p_l, pos, past_sb, past_dsa, past_idx, conv_state, n_top,
               w_in, w_branch_sb, w_branch_dsa, w_out, ln1_g, ln1_b,
               w_ffn_in, ffn_conv_w, ffn_conv_b, w_ffn_out, ln2_g, ln2_b,
               w_ple_gate, w_ple_proj):
    b, t = x.shape[:2]
    qa, ka, va, qb, kb, vb, qi, ki, wi, ga, gb = project_in(x, w_in, pos)
    new_sb = jnp.stack([ka, va], axis=2)
    new_dsa = jnp.stack([kb, vb], axis=2)
    sb_all = jnp.concatenate([past_sb.astype(x.dtype), new_sb], axis=1)
    dsa_all = jnp.concatenate([past_dsa.astype(x.dtype), new_dsa], axis=1)
    idx_all = jnp.concatenate([past_idx.astype(x.dtype), ki], axis=1)
    k_pos = jnp.arange(sb_all.shape[1], dtype=jnp.int32)

    k_a, v_a = sb_all[:, :, 0].astype(F32), sb_all[:, :, 1].astype(F32)
    oa = blocked_map(functools.partial(stick_breaking_block, k=k_a, v=v_a, k_pos=k_pos), (qa,), pos)

    k_b, v_b = dsa_all[:, :, 0].astype(F32), dsa_all[:, :, 1].astype(F32)
    ob = blocked_map(functools.partial(dsa_block, k=k_b, v=v_b, ki=idx_all.astype(F32),
                                       k_pos=k_pos, n_top=n_top), (qb, qi, wi), pos)

    branch_a = oa.astype(x.dtype).reshape(b, t, -1) @ w_branch_sb
    branch_b = ob.astype(x.dtype).reshape(b, t, -1) @ w_branch_dsa
    mix = jax.nn.sigmoid(ga) * branch_a + jax.nn.sigmoid(gb) * branch_b
    h = layer_norm(DEEPNORM_ALPHA * x + mix @ w_out, ln1_g, ln1_b)

    f, new_conv = conv_ffn(h, w_ffn_in, ffn_conv_w, ffn_conv_b, w_ffn_out, conv_state)
    h2 = layer_norm(DEEPNORM_ALPHA * h + f, ln2_g, ln2_b)
    out = h2 + jax.nn.sigmoid(h2 @ w_ple_gate) * (p_l @ w_ple_proj)
    return out, new_sb, new_dsa, ki, new_conv


def gather_pages(pool, page_table):
    g = pool[page_table]
    return g.reshape(g.shape[0], g.shape[1] * g.shape[2], *g.shape[3:])


def setup_inputs(seed: int = 0) -> dict:
    key = jax.random.key(seed)
    ks = jax.random.split(key, 24)
    n_pages = PAST_LEN // PAGE_SIZE
    n_pool = (DEC_BATCH * n_pages * 5) // 4

    def nrm(k, shape, scale=1.0):
        return jax.random.normal(k, shape, F32) * scale

    page_table = jax.random.permutation(ks[7], n_pool)[: DEC_BATCH * n_pages]
    page_table = page_table.reshape(DEC_BATCH, n_pages).astype(jnp.int32)
    return {
        "x_prompt": nrm(ks[0], (BATCH, SEQ, D_MODEL)),
        "x_sample": nrm(ks[1], (DEC_BATCH, DEC_SEQ, D_MODEL)),
        "cache_sb_kv": nrm(ks[2], (DEPTH, n_pool, PAGE_SIZE, 2, SB_KV_HEADS, HEAD_DIM)),
        "cache_dsa_kv": nrm(ks[3], (DEPTH, n_pool, PAGE_SIZE, 2, DSA_KV_HEADS, HEAD_DIM)),
        "cache_idx_k": nrm(ks[4], (DEPTH, n_pool, PAGE_SIZE, IDX_DIM)),
        "state_ffn_conv": nrm(ks[5], (DEPTH, DEC_BATCH, CONV_W - 1, D_FF)),
        "page_table": page_table,
        "p_prompt": nrm(ks[8], (DEPTH, BATCH, SEQ, PLE_DIM)),
        "p_sample": nrm(ks[9], (DEPTH, DEC_BATCH, DEC_SEQ, PLE_DIM)),
        "w_in": nrm(ks[10], (DEPTH, D_MODEL, IN_WIDTH), D_MODEL ** -0.5),
        "w_branch_sb": nrm(ks[11], (DEPTH, SB_HEADS * HEAD_DIM, D_MODEL), (SB_HEADS * HEAD_DIM) ** -0.5),
        "w_branch_dsa": nrm(ks[12], (DEPTH, DSA_HEADS * HEAD_DIM, D_MODEL), (DSA_HEADS * HEAD_DIM) ** -0.5),
        "w_out": nrm(ks[13], (DEPTH, D_MODEL, D_MODEL), DEEPNORM_BETA * D_MODEL ** -0.5),
        "ln1_g": 1.0 + nrm(ks[14], (DEPTH, D_MODEL), 0.01),
        "ln1_b": nrm(ks[15], (DEPTH, D_MODEL), 0.01),
        "w_ffn_in": nrm(ks[16], (DEPTH, D_MODEL, 2 * D_FF), D_MODEL ** -0.5),
        "ffn_conv_w": nrm(ks[17], (DEPTH, CONV_W, D_FF), CONV_W ** -0.5),
        "ffn_conv_b": nrm(ks[18], (DEPTH, D_FF), 0.01),
        "w_ffn_out": nrm(ks[19], (DEPTH, D_FF, D_MODEL), DEEPNORM_BETA * D_FF ** -0.5),
        "ln2_g": 1.0 + nrm(ks[20], (DEPTH, D_MODEL), 0.01),
        "ln2_b": nrm(ks[21], (DEPTH, D_MODEL), 0.01),
        "w_ple_gate": nrm(ks[22], (DEPTH, D_MODEL, D_MODEL), D_MODEL ** -0.5),
        "w_ple_proj": nrm(ks[23], (DEPTH, PLE_DIM, D_MODEL), PLE_DIM ** -0.5),
    }


def reference(x_prompt, x_sample, cache_sb_kv, cache_dsa_kv, cache_idx_k, state_ffn_conv,
              page_table, p_prompt, p_sample, w_in, w_branch_sb, w_branch_dsa, w_out,
              ln1_g, ln1_b, w_ffn_in, ffn_conv_w, ffn_conv_b, w_ffn_out, ln2_g, ln2_b,
              w_ple_gate, w_ple_proj):
    b, t = x_prompt.shape[:2]
    db, ts = x_sample.shape[:2]
    past_len = page_table.shape[1] * cache_sb_kv.shape[2]
    pos_p = jnp.arange(t, dtype=jnp.int32)
    pos_s = past_len + jnp.arange(ts, dtype=jnp.int32)
    top_p = max(1, min(TOPK_MAX, t // 4))
    top_s = max(1, min(TOPK_MAX, (past_len + ts) // 4))
    dt = x_prompt.dtype
    empty_sb = jnp.zeros((b, 0, 2, SB_KV_HEADS, HEAD_DIM), dt)
    empty_dsa = jnp.zeros((b, 0, 2, DSA_KV_HEADS, HEAD_DIM), dt)
    empty_idx = jnp.zeros((b, 0, IDX_DIM), dt)
    zero_conv = jnp.zeros((b, CONV_W - 1, D_FF), dt)

    xp, xs = x_prompt, x_sample
    sb_p, dsa_p, idx_p, conv_p = [], [], [], []
    sb_s, dsa_s, idx_s, conv_s = [], [], [], []
    for l in range(DEPTH):
        lw = (w_in[l], w_branch_sb[l], w_branch_dsa[l], w_out[l], ln1_g[l], ln1_b[l],
              w_ffn_in[l], ffn_conv_w[l], ffn_conv_b[l], w_ffn_out[l], ln2_g[l], ln2_b[l],
              w_ple_gate[l], w_ple_proj[l])
        xp, a1, a2, a3, a4 = layer_step(xp, p_prompt[l], pos_p, empty_sb, empty_dsa, empty_idx,
                                        zero_conv, top_p, *lw)
        sb_p.append(a1); dsa_p.append(a2); idx_p.append(a3); conv_p.append(a4)
        xs, c1, c2, c3, c4 = layer_step(xs, p_sample[l], pos_s,
                                        gather_pages(cache_sb_kv[l], page_table),
                                        gather_pages(cache_dsa_kv[l], page_table),
                                        gather_pages(cache_idx_k[l], page_table),
                                        state_ffn_conv[l], top_s, *lw)
        sb_s.append(c1); dsa_s.append(c2); idx_s.append(c3); conv_s.append(c4)

    return (xp, xs, jnp.stack(sb_p), jnp.stack(dsa_p), jnp.stack(idx_p), jnp.stack(conv_p),
            jnp.stack(sb_s), jnp.stack(dsa_s), jnp.stack(idx_s), jnp.stack(conv_s))
```

```python
import functools

import jax
import jax.numpy as jnp
import numpy as np
from jax import lax
from jax.experimental import pallas as pl
from jax.experimental.pallas import tpu as pltpu

F32 = jnp.float32
BF16 = jnp.bfloat16
I32 = jnp.int32

D_MODEL = 2048
HEAD_DIM = 128
N_HEADS = 8
N_KV = 2
GROUP = N_HEADS // N_KV
IDX_HEADS = 16
IDX_DIM = 64
TOPK_MAX = 256
ROPE_THETA = 500000.0
D_FF = 5504
CONV_W = 3
PLE_DIM = 256
LN_EPS = 1e-5

LANES = 128
SUBLANES = 8
VMEM_LIMIT = 56 * 1024 * 1024

TN = 512
COL_QA = 0
COL_KVA = 1024
COL_QB = 1536
COL_KVB = 2560
COL_QI = 3072
COL_KIW = 4096
COL_GA = 4608
COL_GB = 6656
IN_COLS = 8704
N_IN_TILES = IN_COLS // TN
D_FF_PAD = 5632
TF = 512

NEG_BIG = -1e30
KEY_NEG_INF = np.int32(np.array(0xFF800000, dtype=np.uint32).view(np.int32) ^ 0x7FFFFFFF)
INT_MIN = np.int32(-2 ** 31)


def _params(sem):
    return pltpu.CompilerParams(dimension_semantics=sem, vmem_limit_bytes=VMEM_LIMIT)


def _dot_t(a, b):
    return lax.dot_general(a, b, (((1,), (1,)), ((), ())), preferred_element_type=F32)


def _dot(a, b):
    return jnp.dot(a, b, preferred_element_type=F32)


def _rope(y, c, s, half):
    w = y.shape[1]
    reps = w // LANES
    if reps > 1:
        c = jnp.concatenate([c] * reps, axis=1)
        s = jnp.concatenate([s] * reps, axis=1)
    lane = lax.broadcasted_iota(I32, y.shape, 1)
    first = (lane & (2 * half - 1)) < half
    partner = jnp.where(first, pltpu.roll(y, w - half, 1), pltpu.roll(y, half, 1))
    return y * c + partner * s


def _proj_kernel(x_ref, w_ref, c128_ref, s128_ref, c64_ref, s64_ref, y_ref):
    n = pl.program_id(1)
    y = _dot(x_ref[...], w_ref[...])

    @pl.when(n < COL_QB // TN)
    def _():
        y_ref[...] = y

    @pl.when(jnp.logical_and(n >= COL_QB // TN, n < COL_KVB // TN))
    def _():
        y_ref[...] = _rope(y, c128_ref[...], s128_ref[...], 16)

    @pl.when(n == COL_KVB // TN)
    def _():
        y_ref[:, :256] = _rope(y[:, :256], c128_ref[...], s128_ref[...], 16)
        y_ref[:, 256:] = y[:, 256:]

    @pl.when(jnp.logical_and(n >= COL_QI // TN, n < COL_KIW // TN))
    def _():
        y_ref[...] = _rope(y, c64_ref[...], s64_ref[...], 8)

    @pl.when(n == COL_KIW // TN)
    def _():
        lane = lax.broadcasted_iota(I32, c64_ref.shape, 1)
        c = jnp.where(lane < IDX_DIM, c64_ref[...], 1.0)
        s = jnp.where(lane < IDX_DIM, s64_ref[...], 0.0)
        y_ref[:, :LANES] = _rope(y[:, :LANES], c, s, 8)
        y_ref[:, LANES:] = y[:, LANES:]

    @pl.when(n >= COL_GA // TN)
    def _():
        y_ref[...] = jax.nn.sigmoid(y)


def _project(xb, w, tabs):
    rows = xb.shape[0]
    tr = min(rows, 1024)
    tab_spec = pl.BlockSpec((tr, LANES), lambda r, n: (r, 0))
    return pl.pallas_call(
        _proj_kernel,
        out_shape=jax.ShapeDtypeStruct((rows, IN_COLS), F32),
        grid=(rows // tr, N_IN_TILES),
        in_specs=[pl.BlockSpec((tr, D_MODEL), lambda r, n: (r, 0)),
                  pl.BlockSpec((D_MODEL, TN), lambda r, n: (0, n)),
                  tab_spec, tab_spec, tab_spec, tab_spec],
        out_specs=pl.BlockSpec((tr, TN), lambda r, n: (r, n)),
        compiler_params=_params(("parallel", "arbitrary")),
        name="in_proj",
    )(xb, w, *tabs)


def _suffix_matrix():
    j = lax.broadcasted_iota(I32, (LANES, 2 * LANES), 0)
    s = lax.broadcasted_iota(I32, (LANES, 2 * LANES), 1)
    return jnp.where(jnp.logical_or(j > s, s >= LANES), 1.0, 0.0).astype(BF16)


def _sb_block(q, k, v, u, carry, mask):
    z = _dot_t(q, k)
    sp = jnp.maximum(z, 0.0) + jnp.log1p(jnp.exp(-jnp.abs(z)))
    lm = -sp
    if mask is not None:
        lm = jnp.where(mask, lm, 0.0)
    hi = lm.astype(BF16)
    lo = (lm - hi.astype(F32)).astype(BF16)
    r = _dot(hi, u) + _dot(lo, u)
    after = r[:, :LANES] + carry
    w = jnp.exp(z - sp + after)
    if mask is not None:
        w = jnp.where(mask, w, 0.0)
    return _dot(w.astype(BF16), v), carry + r[:, LANES:]


def _softmax_block(q, k, v, bias, m_old, l_old, acc_old):
    logit = _dot_t(q, k) + bias
    m_new = jnp.maximum(m_old, jnp.max(logit, axis=1, keepdims=True))
    p = jnp.exp(logit - m_new)
    alpha = jnp.exp(m_old - m_new)
    l_new = alpha * l_old + jnp.sum(p, axis=1, keepdims=True)
    acc_new = alpha * acc_old + _dot(p.astype(BF16), v)
    return m_new, l_new, acc_new


def _sortable(x):
    b = pltpu.bitcast(x, I32)
    return jnp.where(b < 0, b ^ jnp.int32(0x7FFFFFFF), b)


def _kth_largest(count_ge, shape, k):
    t0 = jnp.full(shape, INT_MIN, I32)
    t = jnp.where(count_ge(jnp.zeros(shape, I32)) >= k, jnp.zeros(shape, I32), t0)

    def body(i, t):
        cand = t + jnp.left_shift(jnp.int32(1), jnp.int32(30) - i)
        return jnp.where(count_ge(cand) >= k, cand, t)

    return lax.fori_loop(0, 31, body, t)


def _tie_cutoff(count_tie_below, shape, need, n_bits):
    def body(i, j):
        cand = j + jnp.left_shift(jnp.int32(1), jnp.int32(n_bits - 1) - i)
        return jnp.where(count_tie_below(cand) < need, cand, j)

    return lax.fori_loop(0, n_bits, body, jnp.zeros(shape, I32))


def _sb_prompt_kernel(q_ref, k_ref, v_ref, o_ref, kb_sc, vb_sc, carry_sc, acc_sc, *, tq):
    iq = pl.program_id(2)

    @pl.when(iq == 0)
    def _():
        kb_sc[...] = k_ref[...].astype(BF16)
        vb_sc[...] = v_ref[...].astype(BF16)

    scale = HEAD_DIM ** -0.5
    q = jnp.concatenate([q_ref[:, h * HEAD_DIM:(h + 1) * HEAD_DIM] for h in range(GROUP)], axis=0)
    q = (q * scale).astype(BF16)
    u = _suffix_matrix()
    m_rows = GROUP * tq

    row = lax.broadcasted_iota(I32, (m_rows, LANES), 0) & (tq - 1)
    col = lax.broadcasted_iota(I32, (m_rows, LANES), 1)
    off = pl.multiple_of(iq * tq, tq)
    contrib, carry = _sb_block(q, kb_sc[pl.ds(off, tq), :], vb_sc[pl.ds(off, tq), :], u,
                               jnp.zeros((m_rows, LANES), F32), col < row)
    acc_sc[...] = contrib
    carry_sc[...] = carry

    def body(i, _):
        o2 = pl.multiple_of((iq - 1 - i) * tq, tq)
        c, cr = _sb_block(q, kb_sc[pl.ds(o2, tq), :], vb_sc[pl.ds(o2, tq), :], u,
                          carry_sc[...], None)
        acc_sc[...] += c
        carry_sc[...] = cr
        return 0

    lax.fori_loop(0, iq, body, 0)
    for h in range(GROUP):
        o_ref[:, h * HEAD_DIM:(h + 1) * HEAD_DIM] = acc_sc[h * tq:(h + 1) * tq, :].astype(o_ref.dtype)


def _sb_prompt(y, batch, seq):
    tq = LANES
    nq = seq // tq
    gw = GROUP * HEAD_DIM
    return pl.pallas_call(
        functools.partial(_sb_prompt_kernel, tq=tq),
        out_shape=jax.ShapeDtypeStruct((batch * seq, N_HEADS * HEAD_DIM), BF16),
        grid=(batch, N_KV, nq),
        in_specs=[pl.BlockSpec((tq, gw), lambda b, g, i: (b * nq + i, COL_QA // gw + g)),
                  pl.BlockSpec((seq, HEAD_DIM), lambda b, g, i: (b, COL_KVA // HEAD_DIM + g)),
                  pl.BlockSpec((seq, HEAD_DIM), lambda b, g, i: (b, COL_KVA // HEAD_DIM + N_KV + g))],
        out_specs=pl.BlockSpec((tq, gw), lambda b, g, i: (b * nq + i, g)),
        scratch_shapes=[pltpu.VMEM((seq, HEAD_DIM), BF16), pltpu.VMEM((seq, HEAD_DIM), BF16),
                        pltpu.VMEM((GROUP * tq, LANES), F32), pltpu.VMEM((GROUP * tq, LANES), F32)],
        compiler_params=_params(("parallel", "parallel", "arbitrary")),
        name="sb_prompt",
    )(y, y, y)


TKI = 256


def _dsa_prompt_kernel(qb0_ref, qb1_ref, k_ref, v_ref, qi_ref, kiw_all_ref, kiw_q_ref, o_ref,
                       kb_sc, vb_sc, ki2_sc, wb_sc, key_sc, bias_sc, cut_sc, m_sc, l_sc, acc_sc,
                       *, tq, n_top, idx_bits):
    iq = pl.program_id(1)
    seq = k_ref.shape[0]

    @pl.when(iq == 0)
    def _():
        kb_sc[...] = k_ref[...].astype(BF16)
        vb_sc[...] = v_ref[...].astype(BF16)
        lane = lax.broadcasted_iota(I32, (seq, LANES), 1)
        kia = jnp.where(lane < IDX_DIM, kiw_all_ref[...], 0.0)
        ki2_sc[0] = kia.astype(BF16)
        ki2_sc[1] = pltpu.roll(kia, IDX_DIM, 1).astype(BF16)

    wq = kiw_q_ref[...] * ((IDX_DIM ** -0.5) * (IDX_HEADS ** -0.5))
    for h in range(IDX_HEADS):
        wb_sc[h] = jnp.broadcast_to(wq[:, IDX_DIM + h:IDX_DIM + h + 1], (tq, LANES))

    q_pairs = [qi_ref[:, p * LANES:(p + 1) * LANES].astype(BF16) for p in range(IDX_HEADS // 2)]
    row_pos = iq * tq + lax.broadcasted_iota(I32, (tq, 1), 0)
    n_blk = (iq * tq + tq + TKI - 1) // TKI

    def idx_body(j, _):
        off = pl.multiple_of(j * TKI, TKI)
        rhs_e = ki2_sc[0, pl.ds(off, TKI), :]
        rhs_o = ki2_sc[1, pl.ds(off, TKI), :]
        acc = jnp.zeros((tq, TKI), F32)
        for p in range(IDX_HEADS // 2):
            we = jnp.concatenate([wb_sc[2 * p]] * (TKI // LANES), axis=1)
            wo = jnp.concatenate([wb_sc[2 * p + 1]] * (TKI // LANES), axis=1)
            acc = acc + jnp.maximum(_dot_t(q_pairs[p], rhs_e), 0.0) * we
            acc = acc + jnp.maximum(_dot_t(q_pairs[p], rhs_o), 0.0) * wo
        col_pos = off + lax.broadcasted_iota(I32, (1, TKI), 1)
        key_sc[:, pl.ds(off, TKI)] = jnp.where(col_pos <= row_pos, _sortable(acc), KEY_NEG_INF)
        return 0

    lax.fori_loop(0, n_blk, idx_body, 0)

    def count_ge(t):
        def body(j, c):
            kb = key_sc[:, pl.ds(pl.multiple_of(j * TKI, TKI), TKI)]
            ge = jnp.where(kb >= t, 1, 0)
            return c + ge[:, :LANES] + ge[:, LANES:]
        c = lax.fori_loop(0, n_blk, body, jnp.zeros((tq, LANES), I32))
        return jnp.sum(c, axis=1, keepdims=True)

    thr = _kth_largest(count_ge, (tq, 1), n_top)
    n_gt = count_ge(thr + 1)
    n_ge = count_ge(thr)
    need = n_top - n_gt

    cut_sc[...] = jnp.full((tq, LANES), seq, I32)

    @pl.when(jnp.max(n_ge) > n_top)
    def _():
        def count_tie_below(jc):
            def body(j, c):
                off = pl.multiple_of(j * TKI, TKI)
                kb = key_sc[:, pl.ds(off, TKI)]
                idx = off + lax.broadcasted_iota(I32, (tq, TKI), 1)
                hit = jnp.where(kb == thr, jnp.where(idx < jc, 1, 0), 0)
                return c + hit[:, :LANES] + hit[:, LANES:]
            c = lax.fori_loop(0, n_blk, body, jnp.zeros((tq, LANES), I32))
            return jnp.sum(c, axis=1, keepdims=True)
        cut = _tie_cutoff(count_tie_below, (tq, 1), need, idx_bits)
        cut_sc[...] = jnp.broadcast_to(cut, (tq, LANES))

    cut = cut_sc[:, 0:1]
    cut = jnp.where(thr == KEY_NEG_INF, -1, cut)

    def bias_body(j, _):
        off = pl.multiple_of(j * TKI, TKI)
        kb = key_sc[:, pl.ds(off, TKI)]
        idx = off + lax.broadcasted_iota(I32, (tq, TKI), 1)
        tie = jnp.where(kb == thr, jnp.where(idx <= cut, 0.0, -jnp.inf), -jnp.inf)
        bias_sc[:, pl.ds(off, TKI)] = jnp.where(kb > thr, 0.0, tie)
        return 0

    lax.fori_loop(0, n_blk, bias_body, 0)

    scale = HEAD_DIM ** -0.5
    m_rows = GROUP * tq
    for g, q_ref in enumerate((qb0_ref, qb1_ref)):
        q = jnp.concatenate([q_ref[:, h * HEAD_DIM:(h + 1) * HEAD_DIM] for h in range(GROUP)], axis=0)
        q = (q * scale).astype(BF16)
        m_sc[...] = jnp.full((m_rows, LANES), NEG_BIG, F32)
        l_sc[...] = jnp.zeros((m_rows, LANES), F32)
        acc_sc[...] = jnp.zeros((m_rows, LANES), F32)

        def att_body(j, _, q=q, g=g):
            off = pl.multiple_of(j * tq, tq)
            k = kb_sc[pl.ds(off, tq), g * HEAD_DIM:(g + 1) * HEAD_DIM]
            v = vb_sc[pl.ds(off, tq), g * HEAD_DIM:(g + 1) * HEAD_DIM]
            b = bias_sc[:, pl.ds(off, tq)]
            bias = jnp.concatenate([b] * GROUP, axis=0)
            m, l, a = _softmax_block(q, k, v, bias, m_sc[...], l_sc[...], acc_sc[...])
            m_sc[...] = m
            l_sc[...] = l
            acc_sc[...] = a
            return 0

        lax.fori_loop(0, iq + 1, att_body, 0)
        out = acc_sc[...] / l_sc[...]
        for h in range(GROUP):
            c0 = (g * GROUP + h) * HEAD_DIM
            o_ref[:, c0:c0 + HEAD_DIM] = out[h * tq:(h + 1) * tq, :].astype(o_ref.dtype)


def _dsa_prompt(y, batch, seq):
    tq = LANES
    nq = seq // tq
    gw = GROUP * HEAD_DIM
    kvw = N_KV * HEAD_DIM
    n_top = max(1, min(TOPK_MAX, seq // 4))
    idx_bits = int(seq).bit_length()
    m_rows = GROUP * tq
    return pl.pallas_call(
        functools.partial(_dsa_prompt_kernel, tq=tq, n_top=n_top, idx_bits=idx_bits),
        out_shape=jax.ShapeDtypeStruct((batch * seq, N_HEADS * HEAD_DIM), BF16),
        grid=(batch, nq),
        in_specs=[pl.BlockSpec((tq, gw), lambda b, i: (b * nq + i, COL_QB // gw)),
                  pl.BlockSpec((tq, gw), lambda b, i: (b * nq + i, COL_QB // gw + 1)),
                  pl.BlockSpec((seq, kvw), lambda b, i: (b, COL_KVB // kvw)),
                  pl.BlockSpec((seq, kvw), lambda b, i: (b, COL_KVB // kvw + 1)),
                  pl.BlockSpec((tq, IDX_HEADS * IDX_DIM), lambda b, i: (b * nq + i, COL_QI // (IDX_HEADS * IDX_DIM))),
                  pl.BlockSpec((seq, LANES), lambda b, i: (b, COL_KIW // LANES)),
                  pl.BlockSpec((tq, LANES), lambda b, i: (b * nq + i, COL_KIW // LANES))],
        out_specs=pl.BlockSpec((tq, N_HEADS * HEAD_DIM), lambda b, i: (b * nq + i, 0)),
        scratch_shapes=[pltpu.VMEM((seq, kvw), BF16), pltpu.VMEM((seq, kvw), BF16),
                        pltpu.VMEM((2, seq, LANES), BF16),
                        pltpu.VMEM((IDX_HEADS, tq, LANES), F32),
                        pltpu.VMEM((tq, seq), I32), pltpu.VMEM((tq, seq), F32),
                        pltpu.VMEM((tq, LANES), I32),
                        pltpu.VMEM((m_rows, LANES), F32), pltpu.VMEM((m_rows, LANES), F32),
                        pltpu.VMEM((m_rows, LANES), F32)],
        compiler_params=_params(("parallel", "arbitrary")),
        name="dsa_prompt",
    )(y, y, y, y, y, y, y)


TOK_PAD = SUBLANES
S_ROWS = GROUP * TOK_PAD


def _sb_sample_kernel(pt_ref, q_ref, new_ref, *rest, pps, page):
    page_refs = rest[:pps]
    o_ref = rest[pps]
    carry_sc, acc_sc = rest[pps + 1:]
    j = pl.program_id(1)
    scale = HEAD_DIM ** -0.5
    u = _suffix_matrix()
    qs = [(q_ref[0, g] * scale).astype(BF16) for g in range(N_KV)]

    def visit(kv, mask):
        for g in range(N_KV):
            k = kv[:, g * HEAD_DIM:(g + 1) * HEAD_DIM].astype(BF16)
            v = kv[:, (N_KV + g) * HEAD_DIM:(N_KV + g + 1) * HEAD_DIM].astype(BF16)
            c, cr = _sb_block(qs[g], k, v, u, carry_sc[g], mask)
            acc_sc[g] += c
            carry_sc[g] = cr

    @pl.when(j == 0)
    def _():
        carry_sc[...] = jnp.zeros(carry_sc.shape, F32)
        acc_sc[...] = jnp.zeros(acc_sc.shape, F32)
        tok = lax.broadcasted_iota(I32, (S_ROWS, page), 0) & (TOK_PAD - 1)
        col = lax.broadcasted_iota(I32, (S_ROWS, page), 1)
        visit(new_ref[0], col < tok)

    for i in range(pps):
        visit(page_refs[i][0], None)

    @pl.when(j == pl.num_programs(1) - 1)
    def _():
        o_ref[0] = acc_sc[...]


def _sb_sample(page_table, q, new_kv, cache, pps):
    nseq, n_pages = page_table.shape
    page = cache.shape[1]
    width = cache.shape[2]

    def page_spec(i):
        return pl.BlockSpec((1, page, width),
                            lambda b, j, pt, i=i: (pt[b, n_pages - 1 - (j * pps + i)], 0, 0))

    grid_spec = pltpu.PrefetchScalarGridSpec(
        num_scalar_prefetch=1,
        grid=(nseq, n_pages // pps),
        in_specs=[pl.BlockSpec((1, N_KV, S_ROWS, HEAD_DIM), lambda b, j, pt: (b, 0, 0, 0)),
                  pl.BlockSpec((1, page, width), lambda b, j, pt: (b, 0, 0))]
                 + [page_spec(i) for i in range(pps)],
        out_specs=pl.BlockSpec((1, N_KV, S_ROWS, HEAD_DIM), lambda b, j, pt: (b, 0, 0, 0)),
        scratch_shapes=[pltpu.VMEM((N_KV, S_ROWS, LANES), F32), pltpu.VMEM((N_KV, S_ROWS, LANES), F32)],
    )
    return pl.pallas_call(
        functools.partial(_sb_sample_kernel, pps=pps, page=page),
        out_shape=jax.ShapeDtypeStruct((nseq, N_KV, S_ROWS, HEAD_DIM), F32),
        grid_spec=grid_spec,
        compiler_params=_params(("parallel", "arbitrary")),
        name="sb_sample",
    )(page_table, q, new_kv, *([cache] * pps))


def _idx_sample_kernel(pt_ref, qi_ref, w_ref, new_ref, *rest, pps, page, n_pages, n_top, idx_bits):
    page_refs = rest[:pps]
    bias_ref = rest[pps]
    score_sc = rest[pps + 1]
    j = pl.program_id(1)
    qi = qi_ref[0].astype(BF16)
    wm = w_ref[0]

    def page_score(ki):
        s = jnp.maximum(_dot_t(qi, ki.astype(BF16)), 0.0) * wm
        return jnp.sum(s.reshape(IDX_HEADS, TOK_PAD, page), axis=0)

    @pl.when(j == 0)
    def _():
        tok = lax.broadcasted_iota(I32, (TOK_PAD, page), 0)
        col = lax.broadcasted_iota(I32, (TOK_PAD, page), 1)
        score_sc[:, n_pages * page:] = jnp.where(col <= tok, page_score(new_ref[0]), -jnp.inf)

    for i in range(pps):
        off = pl.multiple_of((j * pps + i) * page, page)
        score_sc[:, pl.ds(off, page)] = page_score(page_refs[i][0])

    @pl.when(j == pl.num_programs(1) - 1)
    def _():
        key = _sortable(score_sc[...])
        idx = lax.broadcasted_iota(I32, key.shape, 1)

        def count_ge(t):
            return jnp.sum(jnp.where(key >= t, 1, 0), axis=1, keepdims=True)

        thr = _kth_largest(count_ge, (TOK_PAD, 1), n_top)
        need = n_top - count_ge(thr + 1)

        def count_tie_below(jc):
            return jnp.sum(jnp.where(key == thr, jnp.where(idx < jc, 1, 0), 0), axis=1, keepdims=True)

        cut = _tie_cutoff(count_tie_below, (TOK_PAD, 1), need, idx_bits)
        cut = jnp.where(thr == KEY_NEG_INF, -1, cut)
        tie = jnp.where(key == thr, jnp.where(idx <= cut, 0.0, -jnp.inf), -jnp.inf)
        bias_ref[0] = jnp.where(key > thr, 0.0, tie)


def _idx_sample(page_table, qi, wmat, new_ki, cache, pps, n_top):
    nseq, n_pages = page_table.shape
    page = cache.shape[1]
    n_cols = (n_pages + 1) * page

    def page_spec(i):
        return pl.BlockSpec((1, page, IDX_DIM), lambda b, j, pt, i=i: (pt[b, j * pps + i], 0, 0))

    grid_spec = pltpu.PrefetchScalarGridSpec(
        num_scalar_prefetch=1,
        grid=(nseq, n_pages // pps),
        in_specs=[pl.BlockSpec((1, IDX_HEADS * TOK_PAD, IDX_DIM), lambda b, j, pt: (b, 0, 0)),
                  pl.BlockSpec((1, IDX_HEADS * TOK_PAD, LANES), lambda b, j, pt: (b, 0, 0)),
                  pl.BlockSpec((1, page, IDX_DIM), lambda b, j, pt: (b, 0, 0))]
                 + [page_spec(i) for i in range(pps)],
        out_specs=pl.BlockSpec((1, TOK_PAD, n_cols), lambda b, j, pt: (b, 0, 0)),
        scratch_shapes=[pltpu.VMEM((TOK_PAD, n_cols), F32)],
    )
    return pl.pallas_call(
        functools.partial(_idx_sample_kernel, pps=pps, page=page, n_pages=n_pages, n_top=n_top,
                          idx_bits=int(n_cols).bit_length()),
        out_shape=jax.ShapeDtypeStruct((nseq, TOK_PAD, n_cols), F32),
        grid_spec=grid_spec,
        compiler_params=_params(("parallel", "arbitrary")),
        name="idx_sample",
    )(page_table, qi, wmat, new_ki, *([cache] * pps))


def _dsa_sample_kernel(pt_ref, q_ref, new_ref, bias_ref, *rest, pps, page, n_pages):
    page_refs = rest[:pps]
    o_ref = rest[pps]
    m_sc, l_sc, acc_sc = rest[pps + 1:]
    j = pl.program_id(1)
    scale = HEAD_DIM ** -0.5
    qs = [(q_ref[0, g] * scale).astype(BF16) for g in range(N_KV)]

    def visit(kv, b8):
        bias = jnp.concatenate([b8] * GROUP, axis=0)
        for g in range(N_KV):
            k = kv[:, g * HEAD_DIM:(g + 1) * HEAD_DIM].astype(BF16)
            v = kv[:, (N_KV + g) * HEAD_DIM:(N_KV + g + 1) * HEAD_DIM].astype(BF16)
            m, l, a = _softmax_block(qs[g], k, v, bias, m_sc[g], l_sc[g], acc_sc[g])
            m_sc[g] = m
            l_sc[g] = l
            acc_sc[g] = a

    @pl.when(j == 0)
    def _():
        m_sc[...] = jnp.full(m_sc.shape, NEG_BIG, F32)
        l_sc[...] = jnp.zeros(l_sc.shape, F32)
        acc_sc[...] = jnp.zeros(acc_sc.shape, F32)
        visit(new_ref[0], bias_ref[0, :, n_pages * page:])

    for i in range(pps):
        off = pl.multiple_of((j * pps + i) * page, page)
        visit(page_refs[i][0], bias_ref[0, :, pl.ds(off, page)])

    @pl.when(j == pl.num_programs(1) - 1)
    def _():
        o_ref[0] = acc_sc[...] / l_sc[...]


def _dsa_sample(page_table, q, new_kv, bias, cache, pps):
    nseq, n_pages = page_table.shape
    page = cache.shape[1]
    width = cache.shape[2]
    n_cols = bias.shape[2]

    def page_spec(i):
        return pl.BlockSpec((1, page, width), lambda b, j, pt, i=i: (pt[b, j * pps + i], 0, 0))

    grid_spec = pltpu.PrefetchScalarGridSpec(
        num_scalar_prefetch=1,
        grid=(nseq, n_pages // pps),
        in_specs=[pl.BlockSpec((1, N_KV, S_ROWS, HEAD_DIM), lambda b, j, pt: (b, 0, 0, 0)),
                  pl.BlockSpec((1, page, width), lambda b, j, pt: (b, 0, 0)),
                  pl.BlockSpec((1, TOK_PAD, n_cols), lambda b, j, pt: (b, 0, 0))]
                 + [page_spec(i) for i in range(pps)],
        out_specs=pl.BlockSpec((1, N_KV, S_ROWS, HEAD_DIM), lambda b, j, pt: (b, 0, 0, 0)),
        scratch_shapes=[pltpu.VMEM((N_KV, S_ROWS, LANES), F32)] * 3,
    )
    return pl.pallas_call(
        functools.partial(_dsa_sample_kernel, pps=pps, page=page, n_pages=n_pages),
        out_shape=jax.ShapeDtypeStruct((nseq, N_KV, S_ROWS, HEAD_DIM), F32),
        grid_spec=grid_spec,
        compiler_params=_params(("parallel", "arbitrary")),
        name="dsa_sample",
    )(page_table, q, new_kv, bias, *([cache] * pps))


def _layer_norm(x, g, b):
    mu = jnp.mean(x, axis=-1, keepdims=True)
    xc = x - mu
    var = jnp.mean(xc * xc, axis=-1, keepdims=True)
    return xc * lax.rsqrt(var + LN_EPS) * g + b


def _merge_kernel(oa_ref, ob_ref, wa_ref, wb_ref, ga_ref, gb_ref, wo_ref, x_ref, g_ref, b_ref,
                  h_ref, hb_ref, acc_sc, *, alpha):
    kt = pl.program_id(1)

    @pl.when(kt == 0)
    def _():
        acc_sc[...] = jnp.zeros(acc_sc.shape, F32)

    mix = ga_ref[...] * _dot(oa_ref[...], wa_ref[...]) + gb_ref[...] * _dot(ob_ref[...], wb_ref[...])
    acc_sc[...] += _dot(mix.astype(BF16), wo_ref[...])

    @pl.when(kt == pl.num_programs(1) - 1)
    def _():
        h = _layer_norm(alpha * x_ref[...] + acc_sc[...], g_ref[...], b_ref[...])
        h_ref[...] = h
        hb_ref[...] = h.astype(BF16)


def _merge(oa, ob, y, x, wa, wb, wo, g, b, alpha):
    rows = x.shape[0]
    tr = min(rows, 512)
    tk = 512
    kw = N_HEADS * HEAD_DIM
    return pl.pallas_call(
        functools.partial(_merge_kernel, alpha=alpha),
        out_shape=(jax.ShapeDtypeStruct((rows, D_MODEL), F32), jax.ShapeDtypeStruct((rows, D_MODEL), BF16)),
        grid=(rows // tr, D_MODEL // tk),
        in_specs=[pl.BlockSpec((tr, kw), lambda r, k: (r, 0)),
                  pl.BlockSpec((tr, kw), lambda r, k: (r, 0)),
                  pl.BlockSpec((kw, tk), lambda r, k: (0, k)),
                  pl.BlockSpec((kw, tk), lambda r, k: (0, k)),
                  pl.BlockSpec((tr, tk), lambda r, k: (r, COL_GA // tk + k)),
                  pl.BlockSpec((tr, tk), lambda r, k: (r, COL_GB // tk + k)),
                  pl.BlockSpec((tk, D_MODEL), lambda r, k: (k, 0)),
                  pl.BlockSpec((tr, D_MODEL), lambda r, k: (r, 0)),
                  pl.BlockSpec((1, D_MODEL), lambda r, k: (0, 0)),
                  pl.BlockSpec((1, D_MODEL), lambda r, k: (0, 0))],
        out_specs=(pl.BlockSpec((tr, D_MODEL), lambda r, k: (r, 0)),
                   pl.BlockSpec((tr, D_MODEL), lambda r, k: (r, 0))),
        scratch_shapes=[pltpu.VMEM((tr, D_MODEL), F32)],
        compiler_params=_params(("parallel", "arbitrary")),
        name="merge_ln",
    )(oa, ob, wa, wb, y, y, wo, x, g, b)


def _gelu_tanh(x):
    return 0.5 * x * (1.0 + jnp.tanh(np.sqrt(2.0 / np.pi) * (x + 0.044715 * (x * x * x))))


HALO = 16


def _ffn_kernel(*refs, alpha, tr, seq_len, blocks_per_seq, prompt_mode):
    if prompt_mode:
        (hb_ref, halo_ref, w1a_ref, w1u_ref, cw_ref, cb_ref, w2_ref, h_ref, g_ref, b_ref,
         h2_ref, h2b_ref, a_ref, acc_sc) = refs
    else:
        (hb_ref, s1_ref, s2_ref, w1a_ref, w1u_ref, cw_ref, cb_ref, w2_ref, h_ref, g_ref, b_ref,
         h2_ref, h2b_ref, a_ref, acc_sc) = refs
    r = pl.program_id(0)
    ft = pl.program_id(1)

    @pl.when(ft == 0)
    def _():
        acc_sc[...] = jnp.zeros(acc_sc.shape, F32)

    hb = hb_ref[...]
    a = _dot(hb, w1a_ref[...])
    up = _dot(hb, w1u_ref[...])
    row = lax.broadcasted_iota(I32, a.shape, 0)
    p1 = pltpu.roll(a, 1, 0)
    p2 = pltpu.roll(a, 2, 0)
    if prompt_mode:
        a_halo = _dot(halo_ref[...], w1a_ref[...])
        keep = jnp.where(r % blocks_per_seq == 0, 0.0, 1.0)
        h6 = a_halo[HALO - 2:HALO - 1, :] * keep
        h7 = a_halo[HALO - 1:HALO, :] * keep
        p1 = jnp.where(row == 0, h7, p1)
        p2 = jnp.where(row == 0, h6, jnp.where(row == 1, h7, p2))
        a_ref[...] = a[tr - SUBLANES:, :]
    else:
        t = row & (seq_len - 1)
        p1 = jnp.where(t == 0, s1_ref[...], p1)
        p2 = jnp.where(t < 2, s2_ref[...], p2)
        a_ref[...] = a
    c = cb_ref[...] + cw_ref[0:1, :] * p2 + cw_ref[1:2, :] * p1 + cw_ref[2:3, :] * a
    hmid = (_gelu_tanh(c) * up).astype(BF16)
    acc_sc[...] += _dot(hmid, w2_ref[...])

    @pl.when(ft == pl.num_programs(1) - 1)
    def _():
        h2 = _layer_norm(alpha * h_ref[...] + acc_sc[...], g_ref[...], b_ref[...])
        h2_ref[...] = h2
        h2b_ref[...] = h2.astype(BF16)


def _ffn(h, hb, w1a, w1u, cw, cb, w2, g, b, alpha, seq_len, state=None):
    rows = h.shape[0]
    prompt_mode = state is None
    tr = min(seq_len, 512) if prompt_mode else rows
    n_r = rows // tr
    n_f = D_FF_PAD // TF
    common_w = [pl.BlockSpec((D_MODEL, TF), lambda r, f: (0, f)),
                pl.BlockSpec((D_MODEL, TF), lambda r, f: (0, f)),
                pl.BlockSpec((SUBLANES, TF), lambda r, f: (0, f)),
                pl.BlockSpec((1, TF), lambda r, f: (0, f)),
                pl.BlockSpec((TF, D_MODEL), lambda r, f: (f, 0)),
                pl.BlockSpec((tr, D_MODEL), lambda r, f: (r, 0)),
                pl.BlockSpec((1, D_MODEL), lambda r, f: (0, 0)),
                pl.BlockSpec((1, D_MODEL), lambda r, f: (0, 0))]
    if prompt_mode:
        assert seq_len % tr == 0 and tr % HALO == 0
        per = tr // HALO
        extra_specs = [pl.BlockSpec((HALO, D_MODEL), lambda r, f: (jnp.maximum(r * per - 1, 0), 0))]
        extra = [hb]
        a_rows, a_blk = n_r * SUBLANES, SUBLANES
    else:
        assert seq_len & (seq_len - 1) == 0 and seq_len >= CONV_W - 1
        extra_specs = [pl.BlockSpec((tr, TF), lambda r, f: (0, f)),
                       pl.BlockSpec((tr, TF), lambda r, f: (0, f))]
        extra = list(state)
        a_rows, a_blk = rows, tr
    return pl.pallas_call(
        functools.partial(_ffn_kernel, alpha=alpha, tr=tr, seq_len=seq_len,
                          blocks_per_seq=max(seq_len // tr, 1), prompt_mode=prompt_mode),
        out_shape=(jax.ShapeDtypeStruct((rows, D_MODEL), F32), jax.ShapeDtypeStruct((rows, D_MODEL), BF16),
                   jax.ShapeDtypeStruct((a_rows, D_FF_PAD), F32)),
        grid=(n_r, n_f),
        in_specs=[pl.BlockSpec((tr, D_MODEL), lambda r, f: (r, 0))] + extra_specs + common_w,
        out_specs=(pl.BlockSpec((tr, D_MODEL), lambda r, f: (r, 0)),
                   pl.BlockSpec((tr, D_MODEL), lambda r, f: (r, 0)),
                   pl.BlockSpec((a_blk, TF), lambda r, f: (r, f))),
        scratch_shapes=[pltpu.VMEM((tr, D_MODEL), F32)],
        compiler_params=_params(("parallel", "arbitrary")),
        name="conv_ffn_ln",
    )(hb, *extra, w1a, w1u, cw, cb, w2, h, g, b)


def _ple_kernel(hb_ref, wg_ref, p_ref, wp_ref, h_ref, o_ref, ob_ref):
    gate = jax.nn.sigmoid(_dot(hb_ref[...], wg_ref[...]))
    out = h_ref[...] + gate * _dot(p_ref[...], wp_ref[...])
    o_ref[...] = out
    ob_ref[...] = out.astype(BF16)


def _ple(h2, h2b, pb, wg, wp):
    rows = h2.shape[0]
    tr = min(rows, 1024)
    tn = 512
    return pl.pallas_call(
        _ple_kernel,
        out_shape=(jax.ShapeDtypeStruct((rows, D_MODEL), F32), jax.ShapeDtypeStruct((rows, D_MODEL), BF16)),
        grid=(rows // tr, D_MODEL // tn),
        in_specs=[pl.BlockSpec((tr, D_MODEL), lambda r, n: (r, 0)),
                  pl.BlockSpec((D_MODEL, tn), lambda r, n: (0, n)),
                  pl.BlockSpec((tr, PLE_DIM), lambda r, n: (r, 0)),
                  pl.BlockSpec((PLE_DIM, tn), lambda r, n: (0, n)),
                  pl.BlockSpec((tr, tn), lambda r, n: (r, n))],
        out_specs=(pl.BlockSpec((tr, tn), lambda r, n: (r, n)),
                   pl.BlockSpec((tr, tn), lambda r, n: (r, n))),
        compiler_params=_params(("parallel", "arbitrary")),
        name="ple_gate",
    )(h2b, wg, pb, wp, h2)


def _pack_w_in(w_in):
    sizes = (1024, 256, 256, 1024, 256, 256, 1024, 64, 16, 2048, 2048)
    offs = np.concatenate([[0], np.cumsum(sizes)])
    qa, ka, va, qb, kb, vb, qi, ki, wi, ga, gb = [w_in[..., offs[i]:offs[i + 1]] for i in range(11)]
    pad = jnp.zeros(w_in.shape[:-1] + (TN - IDX_DIM - IDX_HEADS,), w_in.dtype)
    return jnp.concatenate([qa, ka, va, qb, kb, vb, qi, ki, wi, pad, ga, gb], axis=-1).astype(BF16)


def _rope_tables(pos):
    pos = pos.astype(F32)[:, None]

    def table(head_dim):
        rot = head_dim // 4
        half = rot // 2
        inv = ROPE_THETA ** (-(2.0 * jnp.arange(half, dtype=F32)) / rot)
        ang = pos * inv[None, :]
        cos, sin = jnp.cos(ang), jnp.sin(ang)
        ones = jnp.ones((pos.shape[0], head_dim - rot), F32)
        c = jnp.concatenate([cos, cos, ones], axis=1)
        s = jnp.concatenate([-sin, sin, 0.0 * ones], axis=1)
        reps = LANES // head_dim
        return jnp.tile(c, (1, reps)), jnp.tile(s, (1, reps))

    c128, s128 = table(HEAD_DIM)
    c64, s64 = table(IDX_DIM)
    return c128, s128, c64, s64


def _rows_to_sample_q(q, nseq, n_tok):
    q = q.reshape(nseq, n_tok, N_KV, GROUP, HEAD_DIM).transpose(0, 2, 3, 1, 4)
    q = jnp.pad(q, ((0, 0), (0, 0), (0, 0), (0, TOK_PAD - n_tok), (0, 0)))
    return q.reshape(nseq, N_KV, S_ROWS, HEAD_DIM)


def _sample_out_to_rows(o, nseq, n_tok):
    o = o.reshape(nseq, N_KV, GROUP, TOK_PAD, HEAD_DIM)[:, :, :, :n_tok]
    return o.transpose(0, 3, 1, 2, 4).reshape(nseq * n_tok, N_HEADS * HEAD_DIM)


def _pad_page(rows, nseq, n_tok, page):
    w = rows.shape[-1]
    return jnp.pad(rows.reshape(nseq, n_tok, w), ((0, 0), (0, page - n_tok), (0, 0)))


def kernel(x_prompt, x_sample, cache_sb_kv, cache_dsa_kv, cache_idx_k, state_ffn_conv, page_table,
           p_prompt, p_sample, w_in, w_branch_sb, w_branch_dsa, w_out, ln1_g, ln1_b, w_ffn_in,
           ffn_conv_w, ffn_conv_b, w_ffn_out, ln2_g, ln2_b, w_ple_gate, w_ple_proj):
    batch, seq = x_prompt.shape[:2]
    nseq, n_tok = x_sample.shape[:2]
    depth = w_in.shape[0]
    n_pool, page = cache_sb_kv.shape[1:3]
    n_pages = page_table.shape[1]
    past_len = n_pages * page
    alpha = (2 * depth) ** 0.25
    kv_w = 2 * N_KV * HEAD_DIM
    top_s = max(1, min(TOPK_MAX, (past_len + n_tok) // 4))
    pps = min(16, n_pages)
    pps_idx = min(32, n_pages)

    w_in_p = _pack_w_in(w_in)
    wa = w_branch_sb.astype(BF16)
    wb = w_branch_dsa.astype(BF16)
    wo = w_out.astype(BF16)
    ff_pad = D_FF_PAD - D_FF
    w1a = jnp.pad(w_ffn_in[..., :D_FF], ((0, 0), (0, 0), (0, ff_pad))).astype(BF16)
    w1u = jnp.pad(w_ffn_in[..., D_FF:], ((0, 0), (0, 0), (0, ff_pad))).astype(BF16)
    w2 = jnp.pad(w_ffn_out, ((0, 0), (0, ff_pad), (0, 0))).astype(BF16)
    cw = jnp.pad(ffn_conv_w, ((0, 0), (0, SUBLANES - CONV_W), (0, ff_pad)))
    cb = jnp.pad(ffn_conv_b, ((0, 0), (0, ff_pad)))[:, None, :]
    wg = w_ple_gate.astype(BF16)
    wp = w_ple_proj.astype(BF16)
    g1, b1 = ln1_g[:, None, :], ln1_b[:, None, :]
    g2, b2 = ln2_g[:, None, :], ln2_b[:, None, :]

    tabs_p = _rope_tables(jnp.tile(jnp.arange(seq, dtype=jnp.int32), batch))
    tabs_s = _rope_tables(jnp.tile(past_len + jnp.arange(n_tok, dtype=jnp.int32), nseq))

    sb_pages = cache_sb_kv.reshape(depth, n_pool, page, kv_w)
    dsa_pages = cache_dsa_kv.reshape(depth, n_pool, page, kv_w)

    xp = x_prompt.reshape(batch * seq, D_MODEL)
    xs = x_sample.reshape(nseq * n_tok, D_MODEL)
    xpb, xsb = xp.astype(BF16), xs.astype(BF16)
    ppb = p_prompt.reshape(depth, batch * seq, PLE_DIM).astype(BF16)
    psb = p_sample.reshape(depth, nseq * n_tok, PLE_DIM).astype(BF16)

    outs = {k: [] for k in ("sb_p", "dsa_p", "idx_p", "conv_p", "sb_s", "dsa_s", "idx_s", "conv_s")}
    for l in range(depth):
        y = _project(xpb, w_in_p[l], tabs_p)
        oa = _sb_prompt(y, batch, seq)
        ob = _dsa_prompt(y, batch, seq)
        h, hb = _merge(oa, ob, y, xp, wa[l], wb[l], wo[l], g1[l], b1[l], alpha)
        h2, h2b, a_tail = _ffn(h, hb, w1a[l], w1u[l], cw[l], cb[l], w2[l], g2[l], b2[l], alpha, seq)
        xp, xpb = _ple(h2, h2b, ppb[l], wg[l], wp[l])
        outs["sb_p"].append(y[:, COL_KVA:COL_KVA + kv_w].reshape(batch, seq, 2, N_KV, HEAD_DIM))
        outs["dsa_p"].append(y[:, COL_KVB:COL_KVB + kv_w].reshape(batch, seq, 2, N_KV, HEAD_DIM))
        outs["idx_p"].append(y[:, COL_KIW:COL_KIW + IDX_DIM].reshape(batch, seq, IDX_DIM))
        tails = a_tail.reshape(batch, -1, SUBLANES, D_FF_PAD)[:, -1, SUBLANES - (CONV_W - 1):, :D_FF]
        outs["conv_p"].append(tails)

        ys = _project(xsb, w_in_p[l], tabs_s)
        new_sb = ys[:, COL_KVA:COL_KVA + kv_w]
        new_dsa = ys[:, COL_KVB:COL_KVB + kv_w]
        new_ki = ys[:, COL_KIW:COL_KIW + IDX_DIM]
        qa_s = _rows_to_sample_q(ys[:, COL_QA:COL_QA + N_HEADS * HEAD_DIM], nseq, n_tok)
        qb_s = _rows_to_sample_q(ys[:, COL_QB:COL_QB + N_HEADS * HEAD_DIM], nseq, n_tok)
        oa_s = _sb_sample(page_table, qa_s, _pad_page(new_sb, nseq, n_tok, page), sb_pages[l], pps)
        qi_s = ys[:, COL_QI:COL_QI + IDX_HEADS * IDX_DIM].reshape(nseq, n_tok, IDX_HEADS, IDX_DIM)
        qi_s = jnp.pad(qi_s.transpose(0, 2, 1, 3), ((0, 0), (0, 0), (0, TOK_PAD - n_tok), (0, 0)))
        qi_s = qi_s.reshape(nseq, IDX_HEADS * TOK_PAD, IDX_DIM)
        wi_s = ys[:, COL_KIW + IDX_DIM:COL_KIW + IDX_DIM + IDX_HEADS].reshape(nseq, n_tok, IDX_HEADS)
        wi_s = wi_s * ((IDX_DIM ** -0.5) * (IDX_HEADS ** -0.5))
        wi_s = jnp.pad(wi_s.transpose(0, 2, 1), ((0, 0), (0, 0), (0, TOK_PAD - n_tok)))
        wmat = jnp.broadcast_to(wi_s.reshape(nseq, IDX_HEADS * TOK_PAD, 1), (nseq, IDX_HEADS * TOK_PAD, LANES))
        bias = _idx_sample(page_table, qi_s, wmat, _pad_page(new_ki, nseq, n_tok, page),
                           cache_idx_k[l], pps_idx, top_s)
        ob_s = _dsa_sample(page_table, qb_s, _pad_page(new_dsa, nseq, n_tok, page), bias, dsa_pages[l], pps)
        oa_r = _sample_out_to_rows(oa_s, nseq, n_tok).astype(BF16)
        ob_r = _sample_out_to_rows(ob_s, nseq, n_tok).astype(BF16)
        hs, hsb = _merge(oa_r, ob_r, ys, xs, wa[l], wb[l], wo[l], g1[l], b1[l], alpha)
        st = jnp.pad(state_ffn_conv[l], ((0, 0), (0, 0), (0, ff_pad)))
        s1 = jnp.repeat(st[:, 1], n_tok, axis=0)
        s2 = jnp.pad(st, ((0, 0), (0, n_tok - (CONV_W - 1)), (0, 0))).reshape(nseq * n_tok, D_FF_PAD)
        h2s, h2sb, a_s = _ffn(hs, hsb, w1a[l], w1u[l], cw[l], cb[l], w2[l], g2[l], b2[l], alpha,
                              n_tok, state=(s1, s2))
        xs, xsb = _ple(h2s, h2sb, psb[l], wg[l], wp[l])
        outs["sb_s"].append(new_sb.reshape(nseq, n_tok, 2, N_KV, HEAD_DIM))
        outs["dsa_s"].append(new_dsa.reshape(nseq, n_tok, 2, N_KV, HEAD_DIM))
        outs["idx_s"].append(new_ki.reshape(nseq, n_tok, IDX_DIM))
        outs["conv_s"].append(a_s.reshape(nseq, n_tok, D_FF_PAD)[:, n_tok - (CONV_W - 1):, :D_FF])

    return (xp.reshape(batch, seq, D_MODEL), xs.reshape(nseq, n_tok, D_MODEL),
            jnp.stack(outs["sb_p"]), jnp.stack(outs["dsa_p"]), jnp.stack(outs["idx_p"]),
            jnp.stack(outs["conv_p"]), jnp.stack(outs["sb_s"]), jnp.stack(outs["dsa_s"]),
            jnp.stack(outs["idx_s"]), jnp.stack(outs["conv_s"]))
```

```python
import functools

import jax
import jax.numpy as jnp
import numpy as np
from jax import lax
from jax.experimental import pallas as pl
from jax.experimental.pallas import tpu as pltpu

F32 = jnp.float32
BF16 = jnp.bfloat16
I32 = jnp.int32

D_MODEL = 2048
HEAD_DIM = 128
N_HEADS = 8
N_KV = 2
GROUP = N_HEADS // N_KV
IDX_HEADS = 16
IDX_DIM = 64
TOPK_MAX = 256
ROPE_THETA = 500000.0
D_FF = 5504
CONV_W = 3
PLE_DIM = 256
LN_EPS = 1e-5

LANES = 128
SUBLANES = 8
VMEM_LIMIT = 56 * 1024 * 1024

TN = 512
COL_QA = 0
COL_KVA = 1024
COL_QB = 1536
COL_KVB = 2560
COL_QI = 3072
COL_KIW = 4096
COL_GA = 4608
COL_GB = 6656
IN_COLS = 8704
N_IN_TILES = IN_COLS // TN
D_FF_PAD = 5632
TF = 512

NEG_BIG = -1e30
KEY_NEG_INF = np.int32(np.array(0xFF800000, dtype=np.uint32).view(np.int32) ^ 0x7FFFFFFF)
INT_MIN = np.int32(-2 ** 31)


def _params(sem):
    return pltpu.CompilerParams(dimension_semantics=sem, vmem_limit_bytes=VMEM_LIMIT)


def _dot_t(a, b):
    return lax.dot_general(a, b, (((1,), (1,)), ((), ())), preferred_element_type=F32)


def _dot(a, b):
    return jnp.dot(a, b, preferred_element_type=F32)


def _rope(y, c, s, half):
    w = y.shape[1]
    reps = w // LANES
    if reps > 1:
        c = jnp.concatenate([c] * reps, axis=1)
        s = jnp.concatenate([s] * reps, axis=1)
    lane = lax.broadcasted_iota(I32, y.shape, 1)
    first = (lane & (2 * half - 1)) < half
    partner = jnp.where(first, pltpu.roll(y, w - half, 1), pltpu.roll(y, half, 1))
    return y * c + partner * s


def _proj_kernel(x_ref, w_ref, c128_ref, s128_ref, c64_ref, s64_ref, y_ref):
    n = pl.program_id(1)
    y = _dot(x_ref[...], w_ref[...])

    @pl.when(n < COL_QB // TN)
    def _():
        y_ref[...] = y

    @pl.when(jnp.logical_and(n >= COL_QB // TN, n < COL_KVB // TN))
    def _():
        y_ref[...] = _rope(y, c128_ref[...], s128_ref[...], 16)

    @pl.when(n == COL_KVB // TN)
    def _():
        y_ref[:, :256] = _rope(y[:, :256], c128_ref[...], s128_ref[...], 16)
        y_ref[:, 256:] = y[:, 256:]

    @pl.when(jnp.logical_and(n >= COL_QI // TN, n < COL_KIW // TN))
    def _():
        y_ref[...] = _rope(y, c64_ref[...], s64_ref[...], 8)

    @pl.when(n == COL_KIW // TN)
    def _():
        lane = lax.broadcasted_iota(I32, c64_ref.shape, 1)
        c = jnp.where(lane < IDX_DIM, c64_ref[...], 1.0)
        s = jnp.where(lane < IDX_DIM, s64_ref[...], 0.0)
        y_ref[:, :LANES] = _rope(y[:, :LANES], c, s, 8)
        y_ref[:, LANES:] = y[:, LANES:]

    @pl.when(n >= COL_GA // TN)
    def _():
        y_ref[...] = jax.nn.sigmoid(y)


def _project(xb, w, layer, tabs):
    rows = xb.shape[0]
    tr = min(rows, 1024)
    tab_spec = pl.BlockSpec((tr, LANES), lambda r, n: (r, 0))
    return pl.pallas_call(
        _proj_kernel,
        out_shape=jax.ShapeDtypeStruct((rows, IN_COLS), F32),
        grid=(rows // tr, N_IN_TILES),
        in_specs=[pl.BlockSpec((tr, D_MODEL), lambda r, n: (r, 0)),
                  pl.BlockSpec((None, D_MODEL, TN), lambda r, n: (layer, 0, n)),
                  tab_spec, tab_spec, tab_spec, tab_spec],
        out_specs=pl.BlockSpec((tr, TN), lambda r, n: (r, n)),
        compiler_params=_params(("parallel", "arbitrary")),
        name="in_proj",
    )(xb, w, *tabs)


def _suffix_matrix():
    j = lax.broadcasted_iota(I32, (2 * LANES, 2 * LANES), 0) & (LANES - 1)
    s = lax.broadcasted_iota(I32, (2 * LANES, 2 * LANES), 1)
    return jnp.where(jnp.logical_or(j > s, s >= LANES), -1.0, 0.0).astype(BF16)


def _sb_block(q, k, v, u, carry, mask):
    n_sub = k.shape[0] // LANES
    z = _dot_t(q, k)
    sp = jnp.maximum(z, 0.0) + jnp.log(1.0 + jnp.exp(-jnp.abs(z)))
    spm = sp if mask is None else jnp.where(mask, sp, 0.0)
    hi = spm.astype(BF16)
    lo = (spm - hi.astype(F32)).astype(BF16)
    afters = [None] * n_sub
    for i in reversed(range(n_sub)):
        sl = slice(i * LANES, (i + 1) * LANES)
        r = _dot(jnp.concatenate([hi[:, sl], lo[:, sl]], axis=1), u)
        afters[i] = r[:, :LANES] + carry
        carry = carry + r[:, LANES:]
    after = afters[0] if n_sub == 1 else jnp.concatenate(afters, axis=1)
    w = jnp.exp(z - sp + after)
    if mask is not None:
        w = jnp.where(mask, w, 0.0)
    return _dot(w.astype(BF16), v), carry


def _softmax_block(q, k, v, bias, m_old, l_old, acc_old):
    logit = _dot_t(q, k) + bias
    m_new = jnp.maximum(m_old, jnp.max(logit, axis=1, keepdims=True))
    p = jnp.exp(logit - m_new[:, 0:1])
    alpha = jnp.exp(m_old - m_new)
    l_new = alpha * l_old + jnp.sum(p, axis=1, keepdims=True)
    acc_new = alpha * acc_old + _dot(p.astype(BF16), v)
    return m_new, l_new, acc_new


def _sortable(x):
    b = pltpu.bitcast(x, I32)
    return jnp.where(b < 0, b ^ jnp.int32(0x7FFFFFFF), b)


def _kth_largest(count_ge, shape, k, n_total):
    bits_per_check = 4

    def cond(state):
        i, _, cnt = state
        return jnp.logical_and(i < 32, jnp.max(cnt) > k)

    def body(state):
        i, t, cnt = state
        for _ in range(bits_per_check):
            cand = t + jnp.left_shift(jnp.int32(1), jnp.int32(31) - i)
            c = count_ge(cand)
            ok = c >= k
            i, t, cnt = i + 1, jnp.where(ok, cand, t), jnp.where(ok, c, cnt)
        return i, t, cnt

    state = (jnp.int32(0), jnp.full(shape, INT_MIN, I32), jnp.full(shape, n_total, I32))
    _, t, cnt = lax.while_loop(cond, body, state)
    return t, cnt


def _tie_cutoff(count_tie_below, shape, need, n_bits):
    def body(i, j):
        cand = j + jnp.left_shift(jnp.int32(1), jnp.int32(n_bits - 1) - i)
        return jnp.where(count_tie_below(cand) < need, cand, j)

    return lax.fori_loop(0, n_bits, body, jnp.zeros(shape, I32))


def _sb_prompt_kernel(q_ref, k_ref, v_ref, o_ref, kb_sc, vb_sc, carry_sc, acc_sc, *, tq):
    iq = pl.program_id(2)

    @pl.when(iq == 0)
    def _():
        kb_sc[...] = k_ref[...].astype(BF16)
        vb_sc[...] = v_ref[...].astype(BF16)

    scale = HEAD_DIM ** -0.5
    u = _suffix_matrix()
    n_chain = GROUP // 2
    c_rows = 2 * tq
    qs = []
    for c in range(n_chain):
        q = jnp.concatenate([q_ref[:, h * HEAD_DIM:(h + 1) * HEAD_DIM] for h in (2 * c, 2 * c + 1)], axis=0)
        qs.append((q * scale).astype(BF16))

    row = lax.broadcasted_iota(I32, (c_rows, tq), 0) & (tq - 1)
    col = lax.broadcasted_iota(I32, (c_rows, tq), 1)
    off = pl.multiple_of(iq * tq, tq)
    for c in range(n_chain):
        contrib, carry = _sb_block(qs[c], kb_sc[pl.ds(off, tq), :], vb_sc[pl.ds(off, tq), :], u,
                                   jnp.zeros((c_rows, LANES), F32), col < row)
        acc_sc[c] = contrib
        carry_sc[c] = carry

    def body(i, _):
        o2 = pl.multiple_of((iq - 1 - i) * tq, tq)
        k = kb_sc[pl.ds(o2, tq), :]
        v = vb_sc[pl.ds(o2, tq), :]
        for c in range(n_chain):
            contrib, carry = _sb_block(qs[c], k, v, u, carry_sc[c], None)
            acc_sc[c] += contrib
            carry_sc[c] = carry
        return 0

    lax.fori_loop(0, iq, body, 0)
    for h in range(GROUP):
        r0 = (h % 2) * tq
        o_ref[:, h * HEAD_DIM:(h + 1) * HEAD_DIM] = acc_sc[h // 2, r0:r0 + tq, :].astype(o_ref.dtype)


def _sb_prompt(y, batch, seq):
    tq = min(2 * LANES, seq)
    nq = seq // tq
    gw = GROUP * HEAD_DIM
    return pl.pallas_call(
        functools.partial(_sb_prompt_kernel, tq=tq),
        out_shape=jax.ShapeDtypeStruct((batch * seq, N_HEADS * HEAD_DIM), BF16),
        grid=(batch, N_KV, nq),
        in_specs=[pl.BlockSpec((tq, gw), lambda b, g, i: (b * nq + i, COL_QA // gw + g)),
                  pl.BlockSpec((seq, HEAD_DIM), lambda b, g, i: (b, COL_KVA // HEAD_DIM + g)),
                  pl.BlockSpec((seq, HEAD_DIM), lambda b, g, i: (b, COL_KVA // HEAD_DIM + N_KV + g))],
        out_specs=pl.BlockSpec((tq, gw), lambda b, g, i: (b * nq + i, g)),
        scratch_shapes=[pltpu.VMEM((seq, HEAD_DIM), BF16), pltpu.VMEM((seq, HEAD_DIM), BF16),
                        pltpu.VMEM((GROUP // 2, 2 * tq, LANES), F32),
                        pltpu.VMEM((GROUP // 2, 2 * tq, LANES), F32)],
        compiler_params=_params(("parallel", "parallel", "arbitrary")),
        name="sb_prompt",
    )(y, y, y)


TKI = 256


def _fold_rows(x, op):
    return op(x.reshape(x.shape[0] // SUBLANES, SUBLANES, x.shape[1]), axis=0)


def _dsa_prompt_kernel(qb0_ref, qb1_ref, k_ref, v_ref, qi_ref, kiw_all_ref, kiw_q_ref, o_ref,
                       kb_sc, vb_sc, ki2_sc, key_sc, bias_sc, cut_sc, m_sc, l_sc, acc_sc,
                       *, tq, n_top, idx_bits):
    iq = pl.program_id(1)
    seq = k_ref.shape[0]

    @pl.when(iq == 0)
    def _():
        kb_sc[...] = k_ref[...].astype(BF16)
        vb_sc[...] = v_ref[...].astype(BF16)
        lane = lax.broadcasted_iota(I32, (seq, LANES), 1)
        kia = jnp.where(lane < IDX_DIM, kiw_all_ref[...], 0.0)
        ki2_sc[0] = kia.astype(BF16)
        ki2_sc[1] = pltpu.roll(kia, IDX_DIM, 1).astype(BF16)

    w_t = (kiw_q_ref[...] * ((IDX_DIM ** -0.5) * (IDX_HEADS ** -0.5))).T
    w_rows = [w_t[IDX_DIM + h:IDX_DIM + h + 1, :] for h in range(IDX_HEADS)]

    q_pairs = [qi_ref[:, p * LANES:(p + 1) * LANES].astype(BF16) for p in range(IDX_HEADS // 2)]
    q_pos = iq * tq + lax.broadcasted_iota(I32, (1, tq), 1)
    n_blk = (iq * tq + tq + TKI - 1) // TKI

    def idx_body(j, _):
        off = pl.multiple_of(j * TKI, TKI)
        ki_e = ki2_sc[0, pl.ds(off, TKI), :]
        ki_o = ki2_sc[1, pl.ds(off, TKI), :]
        acc = jnp.zeros((TKI, tq), F32)
        for p in range(IDX_HEADS // 2):
            acc = acc + jnp.maximum(_dot_t(ki_e, q_pairs[p]), 0.0) * w_rows[2 * p]
            acc = acc + jnp.maximum(_dot_t(ki_o, q_pairs[p]), 0.0) * w_rows[2 * p + 1]
        k_pos = off + lax.broadcasted_iota(I32, (TKI, 1), 0)
        key_sc[pl.ds(off, TKI), :] = jnp.where(k_pos <= q_pos, _sortable(acc), KEY_NEG_INF)
        return 0

    lax.fori_loop(0, n_blk, idx_body, 0)

    def query_counts(hit_fn):
        def body(j, c):
            off = pl.multiple_of(j * TKI, TKI)
            return c + _fold_rows(hit_fn(key_sc[pl.ds(off, TKI), :], off), jnp.sum)

        c = lax.fori_loop(0, n_blk, body, jnp.zeros((SUBLANES, tq), I32))
        return jnp.sum(c, axis=0, keepdims=True)

    def count_ge(t):
        return query_counts(lambda kb, off: jnp.where(kb >= t, 1, 0))

    thr, n_ge = _kth_largest(count_ge, (1, tq), n_top, n_blk * TKI)

    cut_sc[...] = jnp.full(cut_sc.shape, seq, I32)

    @pl.when(jnp.max(n_ge) > n_top)
    def _():
        need = n_top - count_ge(thr + 1)

        def count_tie_below(jc):
            def hit(kb, off):
                idx = off + lax.broadcasted_iota(I32, kb.shape, 0)
                return jnp.where(kb == thr, jnp.where(idx < jc, 1, 0), 0)
            return query_counts(hit)

        cut = _tie_cutoff(count_tie_below, (1, tq), need, idx_bits)
        cut_sc[...] = jnp.broadcast_to(cut, cut_sc.shape)

    cut = cut_sc[0:1, :]

    def bias_body(j, _):
        off = pl.multiple_of(j * TKI, TKI)
        kb = key_sc[pl.ds(off, TKI), :]
        idx = off + lax.broadcasted_iota(I32, (TKI, tq), 0)
        tie = jnp.where(kb == thr, jnp.where(idx <= cut, 0.0, -jnp.inf), -jnp.inf)
        sel = jnp.where(kb > thr, 0.0, tie)
        sel = jnp.where(kb > KEY_NEG_INF, sel, -jnp.inf)
        bias_sc[:, pl.ds(off, TKI)] = sel.T
        return 0

    lax.fori_loop(0, n_blk, bias_body, 0)

    scale = (HEAD_DIM ** -0.5) * np.log2(np.e)
    n_chain = N_HEADS // 2
    n_sub = TKI // LANES
    qs = []
    for c in range(n_chain):
        q_ref = (qb0_ref, qb1_ref)[c // 2]
        hs = (2 * (c % 2), 2 * (c % 2) + 1)
        q = jnp.concatenate([q_ref[:, h * HEAD_DIM:(h + 1) * HEAD_DIM] for h in hs], axis=0)
        qs.append((q * scale).astype(BF16))

    def bias_rows(off):
        b = bias_sc[:, pl.ds(off, TKI)]
        return jnp.concatenate([b, b], axis=0)

    m_sc[...] = jnp.full(m_sc.shape, -jnp.inf, F32)

    def max_body(j, _):
        off = pl.multiple_of(j * TKI, TKI)
        bias = bias_rows(off)
        for c in range(n_chain):
            g = c // 2
            lg = _dot_t(qs[c], kb_sc[pl.ds(off, TKI), g * HEAD_DIM:(g + 1) * HEAD_DIM]) + bias
            m = m_sc[c]
            for i in range(n_sub):
                m = jnp.maximum(m, lg[:, i * LANES:(i + 1) * LANES])
            m_sc[c] = m
        return 0

    lax.fori_loop(0, n_blk, max_body, 0)

    for c in range(n_chain):
        m_sc[c] = jnp.broadcast_to(jnp.max(m_sc[c], axis=1, keepdims=True), (2 * tq, LANES))
    l_sc[...] = jnp.zeros(l_sc.shape, F32)
    acc_sc[...] = jnp.zeros(acc_sc.shape, F32)

    def sum_body(j, _):
        off = pl.multiple_of(j * TKI, TKI)
        bias = bias_rows(off)
        for c in range(n_chain):
            g = c // 2
            lg = _dot_t(qs[c], kb_sc[pl.ds(off, TKI), g * HEAD_DIM:(g + 1) * HEAD_DIM]) + bias
            p = jnp.exp2(lg - jnp.concatenate([m_sc[c]] * n_sub, axis=1))
            l = l_sc[c]
            for i in range(n_sub):
                l = l + p[:, i * LANES:(i + 1) * LANES]
            l_sc[c] = l
            acc_sc[c] += _dot(p.astype(BF16), vb_sc[pl.ds(off, TKI), g * HEAD_DIM:(g + 1) * HEAD_DIM])
        return 0

    lax.fori_loop(0, n_blk, sum_body, 0)
    for c in range(n_chain):
        out = acc_sc[c] / jnp.sum(l_sc[c], axis=1, keepdims=True)
        for hh in range(2):
            c0 = (2 * c + hh) * HEAD_DIM
            o_ref[:, c0:c0 + HEAD_DIM] = out[hh * tq:(hh + 1) * tq, :].astype(o_ref.dtype)


def _dsa_prompt(y, batch, seq):
    tq = min(2 * LANES, seq)
    assert tq == TKI or seq == tq
    nq = seq // tq
    gw = GROUP * HEAD_DIM
    kvw = N_KV * HEAD_DIM
    n_top = max(1, min(TOPK_MAX, seq // 4))
    idx_bits = int(seq).bit_length()
    q_rows = 2 * tq
    return pl.pallas_call(
        functools.partial(_dsa_prompt_kernel, tq=tq, n_top=n_top, idx_bits=idx_bits),
        out_shape=jax.ShapeDtypeStruct((batch * seq, N_HEADS * HEAD_DIM), BF16),
        grid=(batch, nq),
        in_specs=[pl.BlockSpec((tq, gw), lambda b, i: (b * nq + i, COL_QB // gw)),
                  pl.BlockSpec((tq, gw), lambda b, i: (b * nq + i, COL_QB // gw + 1)),
                  pl.BlockSpec((seq, kvw), lambda b, i: (b, COL_KVB // kvw)),
                  pl.BlockSpec((seq, kvw), lambda b, i: (b, COL_KVB // kvw + 1)),
                  pl.BlockSpec((tq, IDX_HEADS * IDX_DIM), lambda b, i: (b * nq + i, COL_QI // (IDX_HEADS * IDX_DIM))),
                  pl.BlockSpec((seq, LANES), lambda b, i: (b, COL_KIW // LANES)),
                  pl.BlockSpec((tq, LANES), lambda b, i: (b * nq + i, COL_KIW // LANES))],
        out_specs=pl.BlockSpec((tq, N_HEADS * HEAD_DIM), lambda b, i: (b * nq + i, 0)),
        scratch_shapes=[pltpu.VMEM((seq, kvw), BF16),
                        pltpu.VMEM((seq, kvw), BF16),
                        pltpu.VMEM((2, seq, LANES), BF16),
                        pltpu.VMEM((seq, tq), I32),
                        pltpu.VMEM((tq, seq), F32),
                        pltpu.VMEM((SUBLANES, tq), I32),
                        pltpu.VMEM((N_HEADS // 2, q_rows, LANES), F32),
                        pltpu.VMEM((N_HEADS // 2, q_rows, LANES), F32),
                        pltpu.VMEM((N_HEADS // 2, q_rows, HEAD_DIM), F32)],
        compiler_params=_params(("parallel", "arbitrary")),
        name="dsa_prompt",
    )(y, y, y, y, y, y, y)


TOK_PAD = SUBLANES
S_ROWS = GROUP * TOK_PAD


KV_SLOTS = 2 * N_KV


def _page_rows(ref, slot, page):
    return ref[pl.ds(slot, page, stride=KV_SLOTS), :].astype(BF16)


def _gather_kv(page_refs, g, page):
    k = [_page_rows(r, g, page) for r in page_refs]
    v = [_page_rows(r, N_KV + g, page) for r in page_refs]
    if len(page_refs) == 1:
        return k[0], v[0]
    return jnp.concatenate(k, axis=0), jnp.concatenate(v, axis=0)


def _sb_sample_kernel(pt_ref, q_ref, new_ref, *rest, pps, page):
    page_refs = rest[:pps]
    o_ref = rest[pps]
    carry_sc, acc_sc = rest[pps + 1:]
    j = pl.program_id(1)
    scale = HEAD_DIM ** -0.5
    u = _suffix_matrix()
    qs = [(q_ref[0, g] * scale).astype(BF16) for g in range(N_KV)]

    def visit(refs, mask):
        for g in range(N_KV):
            k, v = _gather_kv(refs, g, page)
            c, cr = _sb_block(qs[g], k, v, u, carry_sc[g], mask)
            acc_sc[g] += c
            carry_sc[g] = cr

    @pl.when(j == 0)
    def _():
        carry_sc[...] = jnp.zeros(carry_sc.shape, F32)
        acc_sc[...] = jnp.zeros(acc_sc.shape, F32)
        tok = lax.broadcasted_iota(I32, (S_ROWS, page), 0) & (TOK_PAD - 1)
        col = lax.broadcasted_iota(I32, (S_ROWS, page), 1)
        visit([new_ref], col < tok)

    visit(page_refs, None)

    @pl.when(j == pl.num_programs(1) - 1)
    def _():
        o_ref[0] = acc_sc[...]


def _sb_sample(page_table, q, new_kv, cache, layer, pps):
    nseq, n_pages = page_table.shape
    rows = cache.shape[2]
    page = rows // KV_SLOTS
    n_steps = n_pages // pps

    def page_spec(i):
        return pl.BlockSpec((None, None, rows, HEAD_DIM),
                            lambda b, j, pt, i=i: (layer, pt[b, (n_steps - 1 - j) * pps + i], 0, 0))

    grid_spec = pltpu.PrefetchScalarGridSpec(
        num_scalar_prefetch=1,
        grid=(nseq, n_steps),
        in_specs=[pl.BlockSpec((1, N_KV, S_ROWS, HEAD_DIM), lambda b, j, pt: (b, 0, 0, 0)),
                  pl.BlockSpec((None, rows, HEAD_DIM), lambda b, j, pt: (b, 0, 0))]
                 + [page_spec(i) for i in range(pps)],
        out_specs=pl.BlockSpec((1, N_KV, S_ROWS, HEAD_DIM), lambda b, j, pt: (b, 0, 0, 0)),
        scratch_shapes=[pltpu.VMEM((N_KV, S_ROWS, LANES), F32), pltpu.VMEM((N_KV, S_ROWS, LANES), F32)],
    )
    return pl.pallas_call(
        functools.partial(_sb_sample_kernel, pps=pps, page=page),
        out_shape=jax.ShapeDtypeStruct((nseq, N_KV, S_ROWS, HEAD_DIM), F32),
        grid_spec=grid_spec,
        compiler_params=_params(("parallel", "arbitrary")),
        name="sb_sample",
    )(page_table, q, new_kv, *([cache] * pps))


def _idx_sample_kernel(pt_ref, qi_ref, w_ref, new_ref, *rest, pps, page, n_pages, n_top, idx_bits):
    page_refs = rest[:pps]
    bias_ref = rest[pps]
    score_sc = rest[pps + 1]
    j = pl.program_id(1)
    qi = qi_ref[0].astype(BF16)
    wm = w_ref[0]

    def page_score(ki_t):
        s = jnp.maximum(_dot(qi, ki_t.astype(BF16)), 0.0) * wm
        return jnp.sum(s.reshape(IDX_HEADS, TOK_PAD, page), axis=0)

    @pl.when(j == 0)
    def _():
        tok = lax.broadcasted_iota(I32, (TOK_PAD, page), 0)
        col = lax.broadcasted_iota(I32, (TOK_PAD, page), 1)
        score_sc[:, n_pages * page:] = jnp.where(col <= tok, page_score(new_ref[...]), -jnp.inf)

    for i in range(pps):
        off = pl.multiple_of((j * pps + i) * page, page)
        score_sc[:, pl.ds(off, page)] = page_score(page_refs[i][...])

    @pl.when(j == pl.num_programs(1) - 1)
    def _():
        key = _sortable(score_sc[...])
        idx = lax.broadcasted_iota(I32, key.shape, 1)

        def count_ge(t):
            return jnp.sum(jnp.where(key >= t, 1, 0), axis=1, keepdims=True)

        thr, _ = _kth_largest(count_ge, (TOK_PAD, 1), n_top, key.shape[1])
        need = n_top - count_ge(thr + 1)

        def count_tie_below(jc):
            return jnp.sum(jnp.where(key == thr, jnp.where(idx < jc, 1, 0), 0), axis=1, keepdims=True)

        cut = _tie_cutoff(count_tie_below, (TOK_PAD, 1), need, idx_bits)
        tie = jnp.where(key == thr, jnp.where(idx <= cut, 0.0, -jnp.inf), -jnp.inf)
        sel = jnp.where(key > thr, 0.0, tie)
        bias_ref[0] = jnp.where(key > KEY_NEG_INF, sel, -jnp.inf)


def _idx_sample(page_table, qi, wmat, new_ki_t, cache_t, layer, pps, n_top):
    nseq, n_pages = page_table.shape
    page = cache_t.shape[3]
    n_cols = (n_pages + 1) * page
    cache = cache_t

    def page_spec(i):
        return pl.BlockSpec((None, None, IDX_DIM, page),
                            lambda b, j, pt, i=i: (layer, pt[b, j * pps + i], 0, 0))

    grid_spec = pltpu.PrefetchScalarGridSpec(
        num_scalar_prefetch=1,
        grid=(nseq, n_pages // pps),
        in_specs=[pl.BlockSpec((1, IDX_HEADS * TOK_PAD, IDX_DIM), lambda b, j, pt: (b, 0, 0)),
                  pl.BlockSpec((1, IDX_HEADS * TOK_PAD, LANES), lambda b, j, pt: (b, 0, 0)),
                  pl.BlockSpec((None, IDX_DIM, page), lambda b, j, pt: (b, 0, 0))]
                 + [page_spec(i) for i in range(pps)],
        out_specs=pl.BlockSpec((1, TOK_PAD, n_cols), lambda b, j, pt: (b, 0, 0)),
        scratch_shapes=[pltpu.VMEM((TOK_PAD, n_cols), F32)],
    )
    return pl.pallas_call(
        functools.partial(_idx_sample_kernel, pps=pps, page=page, n_pages=n_pages, n_top=n_top,
                          idx_bits=int(n_cols).bit_length()),
        out_shape=jax.ShapeDtypeStruct((nseq, TOK_PAD, n_cols), F32),
        grid_spec=grid_spec,
        compiler_params=_params(("parallel", "arbitrary")),
        name="idx_sample",
    )(page_table, qi, wmat, new_ki_t, *([cache] * pps))


def _dsa_sample_kernel(pt_ref, q_ref, new_ref, bias_ref, *rest, pps, page, n_pages):
    page_refs = rest[:pps]
    o_ref = rest[pps]
    m_sc, l_sc, acc_sc = rest[pps + 1:]
    j = pl.program_id(1)
    scale = HEAD_DIM ** -0.5
    qs = [(q_ref[0, g] * scale).astype(BF16) for g in range(N_KV)]

    def visit(refs, b8):
        bias = jnp.concatenate([b8] * GROUP, axis=0)
        for g in range(N_KV):
            k, v = _gather_kv(refs, g, page)
            m, l, a = _softmax_block(qs[g], k, v, bias, m_sc[g], l_sc[g], acc_sc[g])
            m_sc[g] = m
            l_sc[g] = l
            acc_sc[g] = a

    @pl.when(j == 0)
    def _():
        m_sc[...] = jnp.full(m_sc.shape, NEG_BIG, F32)
        l_sc[...] = jnp.zeros(l_sc.shape, F32)
        acc_sc[...] = jnp.zeros(acc_sc.shape, F32)
        visit([new_ref], bias_ref[0, :, n_pages * page:])

    off = pl.multiple_of(j * (pps * page), pps * page)
    visit(page_refs, bias_ref[0, :, pl.ds(off, pps * page)])

    @pl.when(j == pl.num_programs(1) - 1)
    def _():
        o_ref[0] = acc_sc[...] / l_sc[...]


def _dsa_sample(page_table, q, new_kv, bias, cache, layer, pps):
    nseq, n_pages = page_table.shape
    rows = cache.shape[2]
    page = rows // KV_SLOTS
    n_cols = bias.shape[2]

    def page_spec(i):
        return pl.BlockSpec((None, None, rows, HEAD_DIM),
                            lambda b, j, pt, i=i: (layer, pt[b, j * pps + i], 0, 0))

    grid_spec = pltpu.PrefetchScalarGridSpec(
        num_scalar_prefetch=1,
        grid=(nseq, n_pages // pps),
        in_specs=[pl.BlockSpec((1, N_KV, S_ROWS, HEAD_DIM), lambda b, j, pt: (b, 0, 0, 0)),
                  pl.BlockSpec((None, rows, HEAD_DIM), lambda b, j, pt: (b, 0, 0)),
                  pl.BlockSpec((1, TOK_PAD, n_cols), lambda b, j, pt: (b, 0, 0))]
                 + [page_spec(i) for i in range(pps)],
        out_specs=pl.BlockSpec((1, N_KV, S_ROWS, HEAD_DIM), lambda b, j, pt: (b, 0, 0, 0)),
        scratch_shapes=[pltpu.VMEM((N_KV, S_ROWS, LANES), F32)] * 3,
    )
    return pl.pallas_call(
        functools.partial(_dsa_sample_kernel, pps=pps, page=page, n_pages=n_pages),
        out_shape=jax.ShapeDtypeStruct((nseq, N_KV, S_ROWS, HEAD_DIM), F32),
        grid_spec=grid_spec,
        compiler_params=_params(("parallel", "arbitrary")),
        name="dsa_sample",
    )(page_table, q, new_kv, bias, *([cache] * pps))


def _layer_norm(x, g, b):
    mu = jnp.mean(x, axis=-1, keepdims=True)
    xc = x - mu
    var = jnp.mean(xc * xc, axis=-1, keepdims=True)
    return xc * lax.rsqrt(var + LN_EPS) * g + b


def _merge_kernel(oa_ref, ob_ref, wa_ref, wb_ref, ga_ref, gb_ref, wo_ref, x_ref, g_ref, b_ref,
                  h_ref, hb_ref, acc_sc, *, alpha):
    kt = pl.program_id(1)

    @pl.when(kt == 0)
    def _():
        acc_sc[...] = jnp.zeros(acc_sc.shape, F32)

    mix = ga_ref[...] * _dot(oa_ref[...], wa_ref[...]) + gb_ref[...] * _dot(ob_ref[...], wb_ref[...])
    acc_sc[...] += _dot(mix.astype(BF16), wo_ref[...])

    @pl.when(kt == pl.num_programs(1) - 1)
    def _():
        h = _layer_norm(alpha * x_ref[...] + acc_sc[...], g_ref[...], b_ref[...])
        h_ref[...] = h
        hb_ref[...] = h.astype(BF16)


def _merge(oa, ob, y, x, wa, wb, wo, g, b, layer, alpha):
    rows = x.shape[0]
    tr = min(rows, 512)
    tk = 512
    kw = N_HEADS * HEAD_DIM
    return pl.pallas_call(
        functools.partial(_merge_kernel, alpha=alpha),
        out_shape=(jax.ShapeDtypeStruct((rows, D_MODEL), F32), jax.ShapeDtypeStruct((rows, D_MODEL), BF16)),
        grid=(rows // tr, D_MODEL // tk),
        in_specs=[pl.BlockSpec((tr, kw), lambda r, k: (r, 0)),
                  pl.BlockSpec((tr, kw), lambda r, k: (r, 0)),
                  pl.BlockSpec((None, kw, tk), lambda r, k: (layer, 0, k)),
                  pl.BlockSpec((None, kw, tk), lambda r, k: (layer, 0, k)),
                  pl.BlockSpec((tr, tk), lambda r, k: (r, COL_GA // tk + k)),
                  pl.BlockSpec((tr, tk), lambda r, k: (r, COL_GB // tk + k)),
                  pl.BlockSpec((None, tk, D_MODEL), lambda r, k: (layer, k, 0)),
                  pl.BlockSpec((tr, D_MODEL), lambda r, k: (r, 0)),
                  pl.BlockSpec((None, 1, D_MODEL), lambda r, k: (layer, 0, 0)),
                  pl.BlockSpec((None, 1, D_MODEL), lambda r, k: (layer, 0, 0))],
        out_specs=(pl.BlockSpec((tr, D_MODEL), lambda r, k: (r, 0)),
                   pl.BlockSpec((tr, D_MODEL), lambda r, k: (r, 0))),
        scratch_shapes=[pltpu.VMEM((tr, D_MODEL), F32)],
        compiler_params=_params(("parallel", "arbitrary")),
        name="merge_ln",
    )(oa, ob, wa, wb, y, y, wo, x, g, b)


def _gelu_tanh(x):
    return 0.5 * x * (1.0 + jnp.tanh(np.sqrt(2.0 / np.pi) * (x + 0.044715 * (x * x * x))))


HALO = 16


def _ffn_kernel(*refs, alpha, tr, seq_len, blocks_per_seq, prompt_mode):
    if prompt_mode:
        (hb_ref, halo_ref, w1a_ref, w1u_ref, cw_ref, cb_ref, w2_ref, h_ref, g_ref, b_ref,
         h2_ref, h2b_ref, a_ref, acc_sc) = refs
    else:
        (hb_ref, s1_ref, s2_ref, w1a_ref, w1u_ref, cw_ref, cb_ref, w2_ref, h_ref, g_ref, b_ref,
         h2_ref, h2b_ref, a_ref, acc_sc) = refs
    r = pl.program_id(0)
    ft = pl.program_id(1)

    @pl.when(ft == 0)
    def _():
        acc_sc[...] = jnp.zeros(acc_sc.shape, F32)

    hb = hb_ref[...]
    a = _dot(hb, w1a_ref[...])
    up = _dot(hb, w1u_ref[...])
    row = lax.broadcasted_iota(I32, a.shape, 0)
    p1 = pltpu.roll(a, 1, 0)
    p2 = pltpu.roll(a, 2, 0)
    if prompt_mode:
        a_halo = _dot(halo_ref[...], w1a_ref[...])
        keep = jnp.where(r % blocks_per_seq == 0, 0.0, 1.0)
        h6 = a_halo[HALO - 2:HALO - 1, :] * keep
        h7 = a_halo[HALO - 1:HALO, :] * keep
        p1 = jnp.where(row == 0, h7, p1)
        p2 = jnp.where(row == 0, h6, jnp.where(row == 1, h7, p2))
        a_ref[...] = a[tr - SUBLANES:, :]
    else:
        t = row & (seq_len - 1)
        p1 = jnp.where(t == 0, s1_ref[...], p1)
        p2 = jnp.where(t < 2, s2_ref[...], p2)
        a_ref[...] = a
    c = cb_ref[...] + cw_ref[0:1, :] * p2 + cw_ref[1:2, :] * p1 + cw_ref[2:3, :] * a
    hmid = (_gelu_tanh(c) * up).astype(BF16)
    acc_sc[...] += _dot(hmid, w2_ref[...])

    @pl.when(ft == pl.num_programs(1) - 1)
    def _():
        h2 = _layer_norm(alpha * h_ref[...] + acc_sc[...], g_ref[...], b_ref[...])
        h2_ref[...] = h2
        h2b_ref[...] = h2.astype(BF16)


def _ffn(h, hb, w1a, w1u, cw, cb, w2, g, b, layer, alpha, seq_len, state=None):
    rows = h.shape[0]
    prompt_mode = state is None
    tr = min(seq_len, 512) if prompt_mode else rows
    n_r = rows // tr
    n_f = D_FF_PAD // TF
    common_w = [pl.BlockSpec((None, D_MODEL, TF), lambda r, f: (layer, 0, f)),
                pl.BlockSpec((None, D_MODEL, TF), lambda r, f: (layer, 0, f)),
                pl.BlockSpec((None, SUBLANES, TF), lambda r, f: (layer, 0, f)),
                pl.BlockSpec((None, 1, TF), lambda r, f: (layer, 0, f)),
                pl.BlockSpec((None, TF, D_MODEL), lambda r, f: (layer, f, 0)),
                pl.BlockSpec((tr, D_MODEL), lambda r, f: (r, 0)),
                pl.BlockSpec((None, 1, D_MODEL), lambda r, f: (layer, 0, 0)),
                pl.BlockSpec((None, 1, D_MODEL), lambda r, f: (layer, 0, 0))]
    if prompt_mode:
        assert seq_len % tr == 0 and tr % HALO == 0
        per = tr // HALO
        extra_specs = [pl.BlockSpec((HALO, D_MODEL), lambda r, f: (jnp.maximum(r * per - 1, 0), 0))]
        extra = [hb]
        a_rows, a_blk = n_r * SUBLANES, SUBLANES
    else:
        assert seq_len & (seq_len - 1) == 0 and seq_len >= CONV_W - 1
        extra_specs = [pl.BlockSpec((tr, TF), lambda r, f: (0, f)),
                       pl.BlockSpec((tr, TF), lambda r, f: (0, f))]
        extra = list(state)
        a_rows, a_blk = rows, tr
    return pl.pallas_call(
        functools.partial(_ffn_kernel, alpha=alpha, tr=tr, seq_len=seq_len,
                          blocks_per_seq=max(seq_len // tr, 1), prompt_mode=prompt_mode),
        out_shape=(jax.ShapeDtypeStruct((rows, D_MODEL), F32), jax.ShapeDtypeStruct((rows, D_MODEL), BF16),
                   jax.ShapeDtypeStruct((a_rows, D_FF_PAD), F32)),
        grid=(n_r, n_f),
        in_specs=[pl.BlockSpec((tr, D_MODEL), lambda r, f: (r, 0))] + extra_specs + common_w,
        out_specs=(pl.BlockSpec((tr, D_MODEL), lambda r, f: (r, 0)),
                   pl.BlockSpec((tr, D_MODEL), lambda r, f: (r, 0)),
                   pl.BlockSpec((a_blk, TF), lambda r, f: (r, f))),
        scratch_shapes=[pltpu.VMEM((tr, D_MODEL), F32)],
        compiler_params=_params(("parallel", "arbitrary")),
        name="conv_ffn_ln",
    )(hb, *extra, w1a, w1u, cw, cb, w2, h, g, b)


def _ple_kernel(hb_ref, wg_ref, p_ref, wp_ref, h_ref, o_ref, ob_ref):
    gate = jax.nn.sigmoid(_dot(hb_ref[...], wg_ref[...]))
    out = h_ref[...] + gate * _dot(p_ref[...], wp_ref[...])
    o_ref[...] = out
    ob_ref[...] = out.astype(BF16)


def _ple(h2, h2b, pb, wg, wp, layer):
    rows = h2.shape[0]
    tr = min(rows, 1024)
    tn = 512
    return pl.pallas_call(
        _ple_kernel,
        out_shape=(jax.ShapeDtypeStruct((rows, D_MODEL), F32), jax.ShapeDtypeStruct((rows, D_MODEL), BF16)),
        grid=(rows // tr, D_MODEL // tn),
        in_specs=[pl.BlockSpec((tr, D_MODEL), lambda r, n: (r, 0)),
                  pl.BlockSpec((None, D_MODEL, tn), lambda r, n: (layer, 0, n)),
                  pl.BlockSpec((None, tr, PLE_DIM), lambda r, n: (layer, r, 0)),
                  pl.BlockSpec((None, PLE_DIM, tn), lambda r, n: (layer, 0, n)),
                  pl.BlockSpec((tr, tn), lambda r, n: (r, n))],
        out_specs=(pl.BlockSpec((tr, tn), lambda r, n: (r, n)),
                   pl.BlockSpec((tr, tn), lambda r, n: (r, n))),
        compiler_params=_params(("parallel", "arbitrary")),
        name="ple_gate",
    )(h2b, wg, pb, wp, h2)


def _pack_w_in(w_in):
    sizes = (1024, 256, 256, 1024, 256, 256, 1024, 64, 16, 2048, 2048)
    offs = np.concatenate([[0], np.cumsum(sizes)])
    qa, ka, va, qb, kb, vb, qi, ki, wi, ga, gb = [w_in[..., offs[i]:offs[i + 1]] for i in range(11)]
    pad = jnp.zeros(w_in.shape[:-1] + (TN - IDX_DIM - IDX_HEADS,), w_in.dtype)
    return jnp.concatenate([qa, ka, va, qb, kb, vb, qi, ki, wi, pad, ga, gb], axis=-1).astype(BF16)


def _rope_tables(pos):
    pos = pos.astype(F32)[:, None]

    def table(head_dim):
        rot = head_dim // 4
        half = rot // 2
        inv = ROPE_THETA ** (-(2.0 * jnp.arange(half, dtype=F32)) / rot)
        ang = pos * inv[None, :]
        cos, sin = jnp.cos(ang), jnp.sin(ang)
        ones = jnp.ones((pos.shape[0], head_dim - rot), F32)
        c = jnp.concatenate([cos, cos, ones], axis=1)
        s = jnp.concatenate([-sin, sin, 0.0 * ones], axis=1)
        reps = LANES // head_dim
        return jnp.tile(c, (1, reps)), jnp.tile(s, (1, reps))

    c128, s128 = table(HEAD_DIM)
    c64, s64 = table(IDX_DIM)
    return c128, s128, c64, s64


def _rows_to_sample_q(q, nseq, n_tok):
    q = q.reshape(nseq, n_tok, N_KV, GROUP, HEAD_DIM).transpose(0, 2, 3, 1, 4)
    q = jnp.pad(q, ((0, 0), (0, 0), (0, 0), (0, TOK_PAD - n_tok), (0, 0)))
    return q.reshape(nseq, N_KV, S_ROWS, HEAD_DIM)


def _sample_out_to_rows(o, nseq, n_tok):
    o = o.reshape(nseq, N_KV, GROUP, TOK_PAD, HEAD_DIM)[:, :, :, :n_tok]
    return o.transpose(0, 3, 1, 2, 4).reshape(nseq * n_tok, N_HEADS * HEAD_DIM)


def _pad_page(rows, nseq, n_tok, page):
    w = rows.shape[-1]
    return jnp.pad(rows.reshape(nseq, n_tok, w), ((0, 0), (0, page - n_tok), (0, 0)))


def _pad_kv_page(rows, nseq, n_tok, page):
    return _pad_page(rows, nseq, n_tok, page).reshape(nseq, page * KV_SLOTS, HEAD_DIM)


def kernel(x_prompt, x_sample, cache_sb_kv, cache_dsa_kv, cache_idx_k, state_ffn_conv, page_table,
           p_prompt, p_sample, w_in, w_branch_sb, w_branch_dsa, w_out, ln1_g, ln1_b, w_ffn_in,
           ffn_conv_w, ffn_conv_b, w_ffn_out, ln2_g, ln2_b, w_ple_gate, w_ple_proj):
    batch, seq = x_prompt.shape[:2]
    nseq, n_tok = x_sample.shape[:2]
    depth = w_in.shape[0]
    n_pool, page = cache_sb_kv.shape[1:3]
    n_pages = page_table.shape[1]
    past_len = n_pages * page
    alpha = (2 * depth) ** 0.25
    kv_w = 2 * N_KV * HEAD_DIM
    top_s = max(1, min(TOPK_MAX, (past_len + n_tok) // 4))
    pps = min(16, n_pages)
    pps_idx = min(32, n_pages)

    w_in_p = _pack_w_in(w_in)
    wa = w_branch_sb.astype(BF16)
    wb = w_branch_dsa.astype(BF16)
    wo = w_out.astype(BF16)
    ff_pad = D_FF_PAD - D_FF
    w1a = jnp.pad(w_ffn_in[..., :D_FF], ((0, 0), (0, 0), (0, ff_pad))).astype(BF16)
    w1u = jnp.pad(w_ffn_in[..., D_FF:], ((0, 0), (0, 0), (0, ff_pad))).astype(BF16)
    w2 = jnp.pad(w_ffn_out, ((0, 0), (0, ff_pad), (0, 0))).astype(BF16)
    cw = jnp.pad(ffn_conv_w, ((0, 0), (0, SUBLANES - CONV_W), (0, ff_pad)))
    cb = jnp.pad(ffn_conv_b, ((0, 0), (0, ff_pad)))[:, None, :]
    wg = w_ple_gate.astype(BF16)
    wp = w_ple_proj.astype(BF16)
    g1, b1 = ln1_g[:, None, :], ln1_b[:, None, :]
    g2, b2 = ln2_g[:, None, :], ln2_b[:, None, :]

    tabs_p = _rope_tables(jnp.tile(jnp.arange(seq, dtype=jnp.int32), batch))
    tabs_s = _rope_tables(jnp.tile(past_len + jnp.arange(n_tok, dtype=jnp.int32), nseq))

    sb_pages = cache_sb_kv.reshape(depth, n_pool, page * KV_SLOTS, HEAD_DIM)
    dsa_pages = cache_dsa_kv.reshape(depth, n_pool, page * KV_SLOTS, HEAD_DIM)
    idx_pages_t = jnp.swapaxes(cache_idx_k, 2, 3)

    xp = x_prompt.reshape(batch * seq, D_MODEL)
    xs = x_sample.reshape(nseq * n_tok, D_MODEL)
    xpb, xsb = xp.astype(BF16), xs.astype(BF16)
    ppb = p_prompt.reshape(depth, batch * seq, PLE_DIM).astype(BF16)
    psb = p_sample.reshape(depth, nseq * n_tok, PLE_DIM).astype(BF16)

    outs = {k: [] for k in ("sb_p", "dsa_p", "idx_p", "conv_p", "sb_s", "dsa_s", "idx_s", "conv_s")}
    for l in range(depth):
        y = _project(xpb, w_in_p, l, tabs_p)
        oa = _sb_prompt(y, batch, seq)
        ob = _dsa_prompt(y, batch, seq)
        h, hb = _merge(oa, ob, y, xp, wa, wb, wo, g1, b1, l, alpha)
        h2, h2b, a_tail = _ffn(h, hb, w1a, w1u, cw, cb, w2, g2, b2, l, alpha, seq)
        xp, xpb = _ple(h2, h2b, ppb, wg, wp, l)
        outs["sb_p"].append(y[:, COL_KVA:COL_KVA + kv_w].reshape(batch, seq, 2, N_KV, HEAD_DIM))
        outs["dsa_p"].append(y[:, COL_KVB:COL_KVB + kv_w].reshape(batch, seq, 2, N_KV, HEAD_DIM))
        outs["idx_p"].append(y[:, COL_KIW:COL_KIW + IDX_DIM].reshape(batch, seq, IDX_DIM))
        tails = a_tail.reshape(batch, -1, SUBLANES, D_FF_PAD)[:, -1, SUBLANES - (CONV_W - 1):, :D_FF]
        outs["conv_p"].append(tails)

        ys = _project(xsb, w_in_p, l, tabs_s)
        new_sb = ys[:, COL_KVA:COL_KVA + kv_w]
        new_dsa = ys[:, COL_KVB:COL_KVB + kv_w]
        new_ki = ys[:, COL_KIW:COL_KIW + IDX_DIM]
        qa_s = _rows_to_sample_q(ys[:, COL_QA:COL_QA + N_HEADS * HEAD_DIM], nseq, n_tok)
        qb_s = _rows_to_sample_q(ys[:, COL_QB:COL_QB + N_HEADS * HEAD_DIM], nseq, n_tok)
        oa_s = _sb_sample(page_table, qa_s, _pad_kv_page(new_sb, nseq, n_tok, page), sb_pages, l, pps)
        qi_s = ys[:, COL_QI:COL_QI + IDX_HEADS * IDX_DIM].reshape(nseq, n_tok, IDX_HEADS, IDX_DIM)
        qi_s = jnp.pad(qi_s.transpose(0, 2, 1, 3), ((0, 0), (0, 0), (0, TOK_PAD - n_tok), (0, 0)))
        qi_s = qi_s.reshape(nseq, IDX_HEADS * TOK_PAD, IDX_DIM)
        wi_s = ys[:, COL_KIW + IDX_DIM:COL_KIW + IDX_DIM + IDX_HEADS].reshape(nseq, n_tok, IDX_HEADS)
        wi_s = wi_s * ((IDX_DIM ** -0.5) * (IDX_HEADS ** -0.5))
        wi_s = jnp.pad(wi_s.transpose(0, 2, 1), ((0, 0), (0, 0), (0, TOK_PAD - n_tok)))
        wmat = jnp.broadcast_to(wi_s.reshape(nseq, IDX_HEADS * TOK_PAD, 1), (nseq, IDX_HEADS * TOK_PAD, LANES))
        new_ki_t = jnp.swapaxes(_pad_page(new_ki, nseq, n_tok, page), 1, 2)
        bias = _idx_sample(page_table, qi_s, wmat, new_ki_t, idx_pages_t, l, pps_idx, top_s)
        ob_s = _dsa_sample(page_table, qb_s, _pad_kv_page(new_dsa, nseq, n_tok, page), bias,
                           dsa_pages, l, pps)
        oa_r = _sample_out_to_rows(oa_s, nseq, n_tok).astype(BF16)
        ob_r = _sample_out_to_rows(ob_s, nseq, n_tok).astype(BF16)
        hs, hsb = _merge(oa_r, ob_r, ys, xs, wa, wb, wo, g1, b1, l, alpha)
        st = jnp.pad(state_ffn_conv[l], ((0, 0), (0, 0), (0, ff_pad)))
        s1 = jnp.repeat(st[:, 1], n_tok, axis=0)
        s2 = jnp.pad(st, ((0, 0), (0, n_tok - (CONV_W - 1)), (0, 0))).reshape(nseq * n_tok, D_FF_PAD)
        h2s, h2sb, a_s = _ffn(hs, hsb, w1a, w1u, cw, cb, w2, g2, b2, l, alpha,
                              n_tok, state=(s1, s2))
        xs, xsb = _ple(h2s, h2sb, psb, wg, wp, l)
        outs["sb_s"].append(new_sb.reshape(nseq, n_tok, 2, N_KV, HEAD_DIM))
        outs["dsa_s"].append(new_dsa.reshape(nseq, n_tok, 2, N_KV, HEAD_DIM))
        outs["idx_s"].append(new_ki.reshape(nseq, n_tok, IDX_DIM))
        outs["conv_s"].append(a_s.reshape(nseq, n_tok, D_FF_PAD)[:, n_tok - (CONV_W - 1):, :D_FF])

    return (xp.reshape(batch, seq, D_MODEL), xs.reshape(nseq, n_tok, D_MODEL),
            jnp.stack(outs["sb_p"]), jnp.stack(outs["dsa_p"]), jnp.stack(outs["idx_p"]),
            jnp.stack(outs["conv_p"]), jnp.stack(outs["sb_s"]), jnp.stack(outs["dsa_s"]),
            jnp.stack(outs["idx_s"]), jnp.stack(outs["conv_s"]))
```

```python
import functools

import jax
import jax.numpy as jnp
import numpy as np
from jax import lax
from jax.experimental import pallas as pl
from jax.experimental.pallas import tpu as pltpu

F32 = jnp.float32
BF16 = jnp.bfloat16
I32 = jnp.int32

D_MODEL = 2048
HEAD_DIM = 128
N_HEADS = 8
N_KV = 2
GROUP = N_HEADS // N_KV
IDX_HEADS = 16
IDX_DIM = 64
TOPK_MAX = 256
ROPE_THETA = 500000.0
D_FF = 5504
CONV_W = 3
PLE_DIM = 256
LN_EPS = 1e-5

LANES = 128
SUBLANES = 8
VMEM_LIMIT = 56 * 1024 * 1024

TN = 512
COL_QA = 0
COL_KVA = 1024
COL_QB = 1536
COL_KVB = 2560
COL_QI = 3072
COL_KIW = 4096
COL_GA = 4608
COL_GB = 6656
IN_COLS = 8704
N_IN_TILES = IN_COLS // TN
D_FF_PAD = 5632
TF = 512

SB_Q_SCALE = (HEAD_DIM ** -0.5) * float(np.log2(np.e))
SB_STACK_ROWS = 512
NEG_BIG = -1e30
KEY_NEG_INF = np.int32(np.array(0xFF800000, dtype=np.uint32).view(np.int32) ^ 0x7FFFFFFF)
INT_MIN = np.int32(-2 ** 31)


def _params(sem):
    return pltpu.CompilerParams(dimension_semantics=sem, vmem_limit_bytes=VMEM_LIMIT)


def _dot_t(a, b):
    return lax.dot_general(a, b, (((1,), (1,)), ((), ())), preferred_element_type=F32)


def _dot(a, b):
    return jnp.dot(a, b, preferred_element_type=F32)


def _rope(y, c, s, half):
    w = y.shape[1]
    reps = w // LANES
    if reps > 1:
        c = jnp.concatenate([c] * reps, axis=1)
        s = jnp.concatenate([s] * reps, axis=1)
    lane = lax.broadcasted_iota(I32, y.shape, 1)
    first = (lane & (2 * half - 1)) < half
    partner = jnp.where(first, pltpu.roll(y, w - half, 1), pltpu.roll(y, half, 1))
    return y * c + partner * s


def _proj_kernel(x_ref, w_ref, c128_ref, s128_ref, c64_ref, s64_ref, y_ref):
    n = pl.program_id(1)

    def product():
        return _dot(x_ref[...], w_ref[...])

    @pl.when(n < COL_QB // TN)
    def _():
        y_ref[...] = product()

    @pl.when(jnp.logical_and(n >= COL_QB // TN, n < COL_KVB // TN))
    def _():
        y_ref[...] = _rope(product(), c128_ref[...], s128_ref[...], 16)

    @pl.when(n == COL_KVB // TN)
    def _():
        y = product()
        y_ref[:, :256] = _rope(y[:, :256], c128_ref[...], s128_ref[...], 16)
        y_ref[:, 256:] = y[:, 256:]

    @pl.when(jnp.logical_and(n >= COL_QI // TN, n < COL_KIW // TN))
    def _():
        y_ref[...] = _rope(product(), c64_ref[...], s64_ref[...], 8)

    @pl.when(n == COL_KIW // TN)
    def _():
        y = product()
        lane = lax.broadcasted_iota(I32, c64_ref.shape, 1)
        c = jnp.where(lane < IDX_DIM, c64_ref[...], 1.0)
        s = jnp.where(lane < IDX_DIM, s64_ref[...], 0.0)
        y_ref[:, :LANES] = _rope(y[:, :LANES], c, s, 8)
        y_ref[:, LANES:] = y[:, LANES:]

    @pl.when(n >= COL_GA // TN)
    def _():
        y_ref[...] = jax.nn.sigmoid(product())


def _project(xb, w, layer, tabs):
    rows = xb.shape[0]
    tr = min(rows, 1024)
    tab_spec = pl.BlockSpec((tr, LANES), lambda r, n: (r, 0))
    return pl.pallas_call(
        _proj_kernel,
        out_shape=jax.ShapeDtypeStruct((rows, IN_COLS), F32),
        grid=(rows // tr, N_IN_TILES),
        in_specs=[pl.BlockSpec((tr, D_MODEL), lambda r, n: (r, 0)),
                  pl.BlockSpec((None, D_MODEL, TN), lambda r, n: (layer, 0, n)),
                  tab_spec, tab_spec, tab_spec, tab_spec],
        out_specs=pl.BlockSpec((tr, TN), lambda r, n: (r, n)),
        compiler_params=_params(("parallel", "arbitrary")),
        name="in_proj",
    )(xb, w, *tabs)


def _suffix_matrix():
    j = lax.broadcasted_iota(I32, (2 * LANES, 2 * LANES), 0) & (LANES - 1)
    s = lax.broadcasted_iota(I32, (2 * LANES, 2 * LANES), 1)
    return jnp.where(jnp.logical_or(j > s, s >= LANES), -1.0, 0.0).astype(BF16)


def _sb_block(q, k, v, u, carry, mask):
    n_sub = k.shape[0] // LANES
    z = _dot_t(q, k)
    neg_abs = pltpu.bitcast(pltpu.bitcast(z, I32) | INT_MIN, F32)
    sp = jnp.maximum(z, 0.0) + jnp.log2(1.0 + jnp.exp2(neg_abs))
    spm = sp if mask is None else jnp.where(mask, sp, 0.0)
    hi = spm.astype(BF16)
    lo = (spm - hi.astype(F32)).astype(BF16)
    m_rows = q.shape[0]
    subs = [jnp.concatenate([hi[:, i * LANES:(i + 1) * LANES], lo[:, i * LANES:(i + 1) * LANES]], axis=1)
            for i in range(n_sub)]
    if n_sub * m_rows <= SB_STACK_ROWS:
        r_all = _dot(subs[0] if n_sub == 1 else jnp.concatenate(subs, axis=0), u)
        rs = [r_all[i * m_rows:(i + 1) * m_rows] for i in range(n_sub)]
    else:
        rs = [_dot(s, u) for s in subs]
    afters = [None] * n_sub
    for i in reversed(range(n_sub)):
        afters[i] = rs[i][:, :LANES] + carry
        carry = carry + rs[i][:, LANES:]
    after = afters[0] if n_sub == 1 else jnp.concatenate(afters, axis=1)
    w = jnp.exp2(z - sp + after)
    if mask is not None:
        w = jnp.where(mask, w, 0.0)
    return _dot(w.astype(BF16), v), carry


def _softmax_block(q, k, v, bias, m_old, l_old, acc_old):
    logit = _dot_t(q, k) + bias
    m_new = jnp.maximum(m_old, jnp.max(logit, axis=1, keepdims=True))
    p = jnp.exp(logit - m_new[:, 0:1])
    alpha = jnp.exp(m_old - m_new)
    l_new = alpha * l_old + jnp.sum(p, axis=1, keepdims=True)
    acc_new = alpha * acc_old + _dot(p.astype(BF16), v)
    return m_new, l_new, acc_new


def _sortable(x):
    b = pltpu.bitcast(x, I32)
    return jnp.where(b < 0, b ^ jnp.int32(0x7FFFFFFF), b)


def _kth_largest(count_ge, shape, k, n_total):
    bits_per_check = 4

    def cond(state):
        i, _, cnt = state
        return jnp.logical_and(i < 32, jnp.max(cnt) > k)

    def body(state):
        i, t, cnt = state
        for _ in range(bits_per_check):
            cand = t + jnp.left_shift(jnp.int32(1), jnp.int32(31) - i)
            c = count_ge(cand)
            ok = c >= k
            i, t, cnt = i + 1, jnp.where(ok, cand, t), jnp.where(ok, c, cnt)
        return i, t, cnt

    state = (jnp.int32(0), jnp.full(shape, INT_MIN, I32), jnp.full(shape, n_total, I32))
    _, t, cnt = lax.while_loop(cond, body, state)
    return t, cnt


def _tie_cutoff(count_tie_below, shape, need, n_bits):
    def body(i, j):
        cand = j + jnp.left_shift(jnp.int32(1), jnp.int32(n_bits - 1) - i)
        return jnp.where(count_tie_below(cand) < need, cand, j)

    return lax.fori_loop(0, n_bits, body, jnp.zeros(shape, I32))


def _sb_prompt_kernel(q_ref, k_ref, v_ref, o_ref, kb_sc, vb_sc, carry_sc, acc_sc, *, tq):
    iq = pl.program_id(2)

    @pl.when(iq == 0)
    def _():
        kb_sc[...] = k_ref[...].astype(BF16)
        vb_sc[...] = v_ref[...].astype(BF16)

    scale = SB_Q_SCALE
    u = _suffix_matrix()
    n_chain = GROUP // 2
    c_rows = 2 * tq
    qs = []
    for c in range(n_chain):
        q = jnp.concatenate([q_ref[:, h * HEAD_DIM:(h + 1) * HEAD_DIM] for h in (2 * c, 2 * c + 1)], axis=0)
        qs.append((q * scale).astype(BF16))

    row = lax.broadcasted_iota(I32, (c_rows, tq), 0) & (tq - 1)
    col = lax.broadcasted_iota(I32, (c_rows, tq), 1)
    off = pl.multiple_of(iq * tq, tq)
    for c in range(n_chain):
        contrib, carry = _sb_block(qs[c], kb_sc[pl.ds(off, tq), :], vb_sc[pl.ds(off, tq), :], u,
                                   jnp.zeros((c_rows, LANES), F32), col < row)
        acc_sc[c] = contrib
        carry_sc[c] = carry

    def body(i, _):
        o2 = pl.multiple_of((iq - 1 - i) * tq, tq)
        k = kb_sc[pl.ds(o2, tq), :]
        v = vb_sc[pl.ds(o2, tq), :]
        for c in range(n_chain):
            contrib, carry = _sb_block(qs[c], k, v, u, carry_sc[c], None)
            acc_sc[c] += contrib
            carry_sc[c] = carry
        return 0

    lax.fori_loop(0, iq, body, 0)
    for h in range(GROUP):
        r0 = (h % 2) * tq
        o_ref[:, h * HEAD_DIM:(h + 1) * HEAD_DIM] = acc_sc[h // 2, r0:r0 + tq, :].astype(o_ref.dtype)


def _sb_prompt(y, batch, seq):
    tq = min(2 * LANES, seq)
    nq = seq // tq
    gw = GROUP * HEAD_DIM
    return pl.pallas_call(
        functools.partial(_sb_prompt_kernel, tq=tq),
        out_shape=jax.ShapeDtypeStruct((batch * seq, N_HEADS * HEAD_DIM), BF16),
        grid=(batch, N_KV, nq),
        in_specs=[pl.BlockSpec((tq, gw), lambda b, g, i: (b * nq + i, COL_QA // gw + g)),
                  pl.BlockSpec((seq, HEAD_DIM), lambda b, g, i: (b, COL_KVA // HEAD_DIM + g)),
                  pl.BlockSpec((seq, HEAD_DIM), lambda b, g, i: (b, COL_KVA // HEAD_DIM + N_KV + g))],
        out_specs=pl.BlockSpec((tq, gw), lambda b, g, i: (b * nq + i, g)),
        scratch_shapes=[pltpu.VMEM((seq, HEAD_DIM), BF16), pltpu.VMEM((seq, HEAD_DIM), BF16),
                        pltpu.VMEM((GROUP // 2, 2 * tq, LANES), F32),
                        pltpu.VMEM((GROUP // 2, 2 * tq, LANES), F32)],
        compiler_params=_params(("parallel", "parallel", "arbitrary")),
        name="sb_prompt",
    )(y, y, y)


TKI = 256


def _fold_rows(x, op):
    return op(x.reshape(x.shape[0] // SUBLANES, SUBLANES, x.shape[1]), axis=0)


def _dsa_prompt_kernel(qb0_ref, qb1_ref, k_ref, v_ref, qi_ref, kiw_all_ref, kiw_q_ref, o_ref,
                       kb_sc, vb_sc, ki2_sc, key_sc, bias_sc, cut_sc, m_sc, l_sc, acc_sc,
                       *, tq, n_top, idx_bits):
    iq = pl.program_id(1)
    seq = k_ref.shape[0]

    @pl.when(iq == 0)
    def _():
        kb_sc[...] = k_ref[...].astype(BF16)
        vb_sc[...] = v_ref[...].astype(BF16)
        lane = lax.broadcasted_iota(I32, (seq, LANES), 1)
        kia = jnp.where(lane < IDX_DIM, kiw_all_ref[...], 0.0)
        ki2_sc[0] = kia.astype(BF16)
        ki2_sc[1] = pltpu.roll(kia, IDX_DIM, 1).astype(BF16)

    w_t = (kiw_q_ref[...] * ((IDX_DIM ** -0.5) * (IDX_HEADS ** -0.5))).T
    w_rows = [w_t[IDX_DIM + h:IDX_DIM + h + 1, :] for h in range(IDX_HEADS)]

    q_pairs = [qi_ref[:, p * LANES:(p + 1) * LANES].astype(BF16) for p in range(IDX_HEADS // 2)]
    q_pos = iq * tq + lax.broadcasted_iota(I32, (1, tq), 1)
    n_blk = (iq * tq + tq + TKI - 1) // TKI

    def idx_body(j, _):
        off = pl.multiple_of(j * TKI, TKI)
        ki_e = ki2_sc[0, pl.ds(off, TKI), :]
        ki_o = ki2_sc[1, pl.ds(off, TKI), :]
        acc = jnp.zeros((TKI, tq), F32)
        for p in range(IDX_HEADS // 2):
            acc = acc + jnp.maximum(_dot_t(ki_e, q_pairs[p]), 0.0) * w_rows[2 * p]
            acc = acc + jnp.maximum(_dot_t(ki_o, q_pairs[p]), 0.0) * w_rows[2 * p + 1]
        k_pos = off + lax.broadcasted_iota(I32, (TKI, 1), 0)
        key_sc[pl.ds(off, TKI), :] = jnp.where(k_pos <= q_pos, _sortable(acc), KEY_NEG_INF)
        return 0

    lax.fori_loop(0, n_blk, idx_body, 0)

    def query_counts(hit_fn):
        def body(j, c):
            off = pl.multiple_of(j * TKI, TKI)
            return c + _fold_rows(hit_fn(key_sc[pl.ds(off, TKI), :], off), jnp.sum)

        c = lax.fori_loop(0, n_blk, body, jnp.zeros((SUBLANES, tq), I32))
        return jnp.sum(c, axis=0, keepdims=True)

    def count_ge(t):
        return query_counts(lambda kb, off: jnp.where(kb >= t, 1, 0))

    thr, n_ge = _kth_largest(count_ge, (1, tq), n_top, n_blk * TKI)

    cut_sc[...] = jnp.full(cut_sc.shape, seq, I32)

    @pl.when(jnp.max(n_ge) > n_top)
    def _():
        need = n_top - count_ge(thr + 1)

        def count_tie_below(jc):
            def hit(kb, off):
                idx = off + lax.broadcasted_iota(I32, kb.shape, 0)
                return jnp.where(kb == thr, jnp.where(idx < jc, 1, 0), 0)
            return query_counts(hit)

        cut = _tie_cutoff(count_tie_below, (1, tq), need, idx_bits)
        cut_sc[...] = jnp.broadcast_to(cut, cut_sc.shape)

    cut = cut_sc[0:1, :]

    def bias_body(j, _):
        off = pl.multiple_of(j * TKI, TKI)
        kb = key_sc[pl.ds(off, TKI), :]
        idx = off + lax.broadcasted_iota(I32, (TKI, tq), 0)
        tie = jnp.where(kb == thr, jnp.where(idx <= cut, 0.0, -jnp.inf), -jnp.inf)
        sel = jnp.where(kb > thr, 0.0, tie)
        sel = jnp.where(kb > KEY_NEG_INF, sel, -jnp.inf)
        bias_sc[:, pl.ds(off, TKI)] = sel.T
        return 0

    lax.fori_loop(0, n_blk, bias_body, 0)

    scale = (HEAD_DIM ** -0.5) * np.log2(np.e)
    n_chain = N_HEADS // 2
    n_sub = TKI // LANES
    qs = []
    for c in range(n_chain):
        q_ref = (qb0_ref, qb1_ref)[c // 2]
        hs = (2 * (c % 2), 2 * (c % 2) + 1)
        q = jnp.concatenate([q_ref[:, h * HEAD_DIM:(h + 1) * HEAD_DIM] for h in hs], axis=0)
        qs.append((q * scale).astype(BF16))

    def bias_rows(off):
        b = bias_sc[:, pl.ds(off, TKI)]
        return jnp.concatenate([b, b], axis=0)

    m_sc[...] = jnp.full(m_sc.shape, -jnp.inf, F32)

    def max_body(j, _):
        off = pl.multiple_of(j * TKI, TKI)
        bias = bias_rows(off)
        for c in range(n_chain):
            g = c // 2
            lg = _dot_t(qs[c], kb_sc[pl.ds(off, TKI), g * HEAD_DIM:(g + 1) * HEAD_DIM]) + bias
            m = m_sc[c]
            for i in range(n_sub):
                m = jnp.maximum(m, lg[:, i * LANES:(i + 1) * LANES])
            m_sc[c] = m
        return 0

    lax.fori_loop(0, n_blk, max_body, 0)

    for c in range(n_chain):
        m_sc[c] = jnp.broadcast_to(jnp.max(m_sc[c], axis=1, keepdims=True), (2 * tq, LANES))
    l_sc[...] = jnp.zeros(l_sc.shape, F32)
    acc_sc[...] = jnp.zeros(acc_sc.shape, F32)

    def sum_body(j, _):
        off = pl.multiple_of(j * TKI, TKI)
        bias = bias_rows(off)
        for c in range(n_chain):
            g = c // 2
            lg = _dot_t(qs[c], kb_sc[pl.ds(off, TKI), g * HEAD_DIM:(g + 1) * HEAD_DIM]) + bias
            p = jnp.exp2(lg - jnp.concatenate([m_sc[c]] * n_sub, axis=1))
            l = l_sc[c]
            for i in range(n_sub):
                l = l + p[:, i * LANES:(i + 1) * LANES]
            l_sc[c] = l
            acc_sc[c] += _dot(p.astype(BF16), vb_sc[pl.ds(off, TKI), g * HEAD_DIM:(g + 1) * HEAD_DIM])
        return 0

    lax.fori_loop(0, n_blk, sum_body, 0)
    for c in range(n_chain):
        out = acc_sc[c] / jnp.sum(l_sc[c], axis=1, keepdims=True)
        for hh in range(2):
            c0 = (2 * c + hh) * HEAD_DIM
            o_ref[:, c0:c0 + HEAD_DIM] = out[hh * tq:(hh + 1) * tq, :].astype(o_ref.dtype)


def _dsa_prompt(y, batch, seq):
    tq = min(2 * LANES, seq)
    assert tq == TKI or seq == tq
    nq = seq // tq
    gw = GROUP * HEAD_DIM
    kvw = N_KV * HEAD_DIM
    n_top = max(1, min(TOPK_MAX, seq // 4))
    idx_bits = int(seq).bit_length()
    q_rows = 2 * tq
    return pl.pallas_call(
        functools.partial(_dsa_prompt_kernel, tq=tq, n_top=n_top, idx_bits=idx_bits),
        out_shape=jax.ShapeDtypeStruct((batch * seq, N_HEADS * HEAD_DIM), BF16),
        grid=(batch, nq),
        in_specs=[pl.BlockSpec((tq, gw), lambda b, i: (b * nq + i, COL_QB // gw)),
                  pl.BlockSpec((tq, gw), lambda b, i: (b * nq + i, COL_QB // gw + 1)),
                  pl.BlockSpec((seq, kvw), lambda b, i: (b, COL_KVB // kvw)),
                  pl.BlockSpec((seq, kvw), lambda b, i: (b, COL_KVB // kvw + 1)),
                  pl.BlockSpec((tq, IDX_HEADS * IDX_DIM), lambda b, i: (b * nq + i, COL_QI // (IDX_HEADS * IDX_DIM))),
                  pl.BlockSpec((seq, LANES), lambda b, i: (b, COL_KIW // LANES)),
                  pl.BlockSpec((tq, LANES), lambda b, i: (b * nq + i, COL_KIW // LANES))],
        out_specs=pl.BlockSpec((tq, N_HEADS * HEAD_DIM), lambda b, i: (b * nq + i, 0)),
        scratch_shapes=[pltpu.VMEM((seq, kvw), BF16),
                        pltpu.VMEM((seq, kvw), BF16),
                        pltpu.VMEM((2, seq, LANES), BF16),
                        pltpu.VMEM((seq, tq), I32),
                        pltpu.VMEM((tq, seq), F32),
                        pltpu.VMEM((SUBLANES, tq), I32),
                        pltpu.VMEM((N_HEADS // 2, q_rows, LANES), F32),
                        pltpu.VMEM((N_HEADS // 2, q_rows, LANES), F32),
                        pltpu.VMEM((N_HEADS // 2, q_rows, HEAD_DIM), F32)],
        compiler_params=_params(("parallel", "arbitrary")),
        name="dsa_prompt",
    )(y, y, y, y, y, y, y)


TOK_PAD = SUBLANES
S_ROWS = GROUP * TOK_PAD


KV_SLOTS = 2 * N_KV


def _page_rows(ref, slot, page):
    return ref[pl.ds(slot, page, stride=KV_SLOTS), :].astype(BF16)


def _gather_kv(page_refs, g, page):
    k = [_page_rows(r, g, page) for r in page_refs]
    v = [_page_rows(r, N_KV + g, page) for r in page_refs]
    if len(page_refs) == 1:
        return k[0], v[0]
    return jnp.concatenate(k, axis=0), jnp.concatenate(v, axis=0)


def _stack_heads(q_blk, g):
    return jnp.concatenate([q_blk[:, (g * GROUP + h) * HEAD_DIM:(g * GROUP + h + 1) * HEAD_DIM]
                            for h in range(GROUP)], axis=0)


def _new_token_kv(new_ref, g, page):
    pad = jnp.zeros((page - TOK_PAD, HEAD_DIM), F32)
    k = jnp.concatenate([new_ref[:, g * HEAD_DIM:(g + 1) * HEAD_DIM], pad], axis=0)
    v = jnp.concatenate([new_ref[:, (N_KV + g) * HEAD_DIM:(N_KV + g + 1) * HEAD_DIM], pad], axis=0)
    return k.astype(BF16), v.astype(BF16)


def _unstack_heads(o_ref, g, acc):
    for h in range(GROUP):
        c0 = (g * GROUP + h) * HEAD_DIM
        o_ref[:, c0:c0 + HEAD_DIM] = acc[h * TOK_PAD:(h + 1) * TOK_PAD, :]


def _sb_sample_kernel(pt_ref, q_ref, new_ref, *rest, pps, page):
    page_refs = rest[:pps]
    o_ref = rest[pps]
    carry_sc, acc_sc = rest[pps + 1:]
    j = pl.program_id(1)
    u = _suffix_matrix()
    qs = [(_stack_heads(q_ref[...], g) * SB_Q_SCALE).astype(BF16) for g in range(N_KV)]

    def visit(kv_fn, mask):
        for g in range(N_KV):
            k, v = kv_fn(g)
            c, cr = _sb_block(qs[g], k, v, u, carry_sc[g], mask)
            acc_sc[g] += c
            carry_sc[g] = cr

    @pl.when(j == 0)
    def _():
        carry_sc[...] = jnp.zeros(carry_sc.shape, F32)
        acc_sc[...] = jnp.zeros(acc_sc.shape, F32)
        tok = lax.broadcasted_iota(I32, (S_ROWS, page), 0) & (TOK_PAD - 1)
        col = lax.broadcasted_iota(I32, (S_ROWS, page), 1)
        visit(lambda g: _new_token_kv(new_ref, g, page), col < tok)

    visit(lambda g: _gather_kv(page_refs, g, page), None)

    @pl.when(j == pl.num_programs(1) - 1)
    def _():
        for g in range(N_KV):
            _unstack_heads(o_ref, g, acc_sc[g])


def _sb_sample(page_table, ys, cache, layer, pps):
    nseq, n_pages = page_table.shape
    rows = cache.shape[2]
    page = rows // KV_SLOTS
    n_steps = n_pages // pps
    qw = N_HEADS * HEAD_DIM
    kvw = KV_SLOTS * HEAD_DIM

    def page_spec(i):
        return pl.BlockSpec((None, None, rows, HEAD_DIM),
                            lambda b, j, pt, i=i: (layer, pt[b, (n_steps - 1 - j) * pps + i], 0, 0))

    grid_spec = pltpu.PrefetchScalarGridSpec(
        num_scalar_prefetch=1,
        grid=(nseq, n_steps),
        in_specs=[pl.BlockSpec((TOK_PAD, qw), lambda b, j, pt: (b, COL_QA // qw)),
                  pl.BlockSpec((TOK_PAD, kvw), lambda b, j, pt: (b, COL_KVA // kvw))]
                 + [page_spec(i) for i in range(pps)],
        out_specs=pl.BlockSpec((TOK_PAD, qw), lambda b, j, pt: (b, 0)),
        scratch_shapes=[pltpu.VMEM((N_KV, S_ROWS, LANES), F32), pltpu.VMEM((N_KV, S_ROWS, LANES), F32)],
    )
    return pl.pallas_call(
        functools.partial(_sb_sample_kernel, pps=pps, page=page),
        out_shape=jax.ShapeDtypeStruct((nseq * TOK_PAD, qw), F32),
        grid_spec=grid_spec,
        compiler_params=_params(("parallel", "arbitrary")),
        name="sb_sample",
    )(page_table, ys, ys, *([cache] * pps))


def _idx_sample_kernel(pt_ref, qi_ref, kiw_ref, *rest, pps, page, n_pages, n_top, idx_bits, n_tok):
    page_refs = rest[:pps]
    bias_ref = rest[pps]
    score_sc = rest[pps + 1]
    j = pl.program_id(1)

    row = lax.broadcasted_iota(I32, (TOK_PAD, 1), 0)

    def real_rows(x):
        return jnp.where(row < n_tok, x, pltpu.roll(x, n_tok, 0))

    qi_rows = real_rows(qi_ref[...])
    kiw = real_rows(kiw_ref[...])
    qi = jnp.concatenate([qi_rows[:, h * IDX_DIM:(h + 1) * IDX_DIM] for h in range(IDX_HEADS)],
                         axis=0).astype(BF16)
    w_scale = (IDX_DIM ** -0.5) * (IDX_HEADS ** -0.5)
    wm = jnp.concatenate([jnp.broadcast_to(kiw[:, IDX_DIM + h:IDX_DIM + h + 1] * w_scale, (TOK_PAD, LANES))
                          for h in range(IDX_HEADS)], axis=0)

    def head_sum(s):
        n = s.shape[1]
        w = wm if n == LANES else jnp.concatenate([wm] * (n // LANES), axis=1)
        return jnp.sum((jnp.maximum(s, 0.0) * w).reshape(IDX_HEADS, TOK_PAD, n), axis=0)

    @pl.when(j == 0)
    def _():
        ki_new = jnp.concatenate([kiw_ref[:, :IDX_DIM], jnp.zeros((page - TOK_PAD, IDX_DIM), F32)], axis=0)
        tok = lax.broadcasted_iota(I32, (TOK_PAD, page), 0) & (n_tok - 1)
        col = lax.broadcasted_iota(I32, (TOK_PAD, page), 1)
        s_new = head_sum(_dot_t(qi, ki_new.astype(BF16)))
        score_sc[:, n_pages * page:] = jnp.where(col <= tok, s_new, -jnp.inf)

    off = pl.multiple_of(j * (pps * page), pps * page)
    ki_t = jnp.concatenate([r[...] for r in page_refs], axis=1)
    score_sc[:, pl.ds(off, pps * page)] = head_sum(_dot(qi, ki_t.astype(BF16)))

    @pl.when(j == pl.num_programs(1) - 1)
    def _():
        key = _sortable(score_sc[...])
        idx = lax.broadcasted_iota(I32, key.shape, 1)

        def count_ge(t):
            return jnp.sum(jnp.where(key >= t, 1, 0), axis=1, keepdims=True)

        thr, n_ge = _kth_largest(count_ge, (TOK_PAD, 1), n_top, key.shape[1])

        def search_cut():
            need = n_top - count_ge(thr + 1)

            def count_tie_below(jc):
                return jnp.sum(jnp.where(key == thr, jnp.where(idx < jc, 1, 0), 0), axis=1, keepdims=True)

            return _tie_cutoff(count_tie_below, (TOK_PAD, 1), need, idx_bits)

        cut = lax.cond(jnp.max(n_ge) > n_top, search_cut,
                       lambda: jnp.full((TOK_PAD, 1), key.shape[1], I32))
        tie = jnp.where(key == thr, jnp.where(idx <= cut, 0.0, -jnp.inf), -jnp.inf)
        sel = jnp.where(key > thr, 0.0, tie)
        bias_ref[0] = jnp.where(key > KEY_NEG_INF, sel, -jnp.inf)


def _idx_sample(page_table, ys, cache_t, layer, pps, n_top, n_tok):
    assert n_tok & (n_tok - 1) == 0 and TOK_PAD % n_tok == 0
    nseq, n_pages = page_table.shape
    page = cache_t.shape[3]
    n_cols = (n_pages + 1) * page
    cache = cache_t
    qiw = IDX_HEADS * IDX_DIM

    def page_spec(i):
        return pl.BlockSpec((None, None, IDX_DIM, page),
                            lambda b, j, pt, i=i: (layer, pt[b, j * pps + i], 0, 0))

    grid_spec = pltpu.PrefetchScalarGridSpec(
        num_scalar_prefetch=1,
        grid=(nseq, n_pages // pps),
        in_specs=[pl.BlockSpec((TOK_PAD, qiw), lambda b, j, pt: (b, COL_QI // qiw)),
                  pl.BlockSpec((TOK_PAD, LANES), lambda b, j, pt: (b, COL_KIW // LANES))]
                 + [page_spec(i) for i in range(pps)],
        out_specs=pl.BlockSpec((1, TOK_PAD, n_cols), lambda b, j, pt: (b, 0, 0)),
        scratch_shapes=[pltpu.VMEM((TOK_PAD, n_cols), F32)],
    )
    return pl.pallas_call(
        functools.partial(_idx_sample_kernel, pps=pps, page=page, n_pages=n_pages, n_top=n_top,
                          idx_bits=int(n_cols).bit_length(), n_tok=n_tok),
        out_shape=jax.ShapeDtypeStruct((nseq, TOK_PAD, n_cols), F32),
        grid_spec=grid_spec,
        compiler_params=_params(("parallel", "arbitrary")),
        name="idx_sample",
    )(page_table, ys, ys, *([cache] * pps))


def _dsa_sample_kernel(pt_ref, q0_ref, q1_ref, new_ref, bias_ref, *rest, pps, page, n_pages):
    page_refs = rest[:pps]
    o_ref = rest[pps]
    m_sc, l_sc, acc_sc = rest[pps + 1:]
    j = pl.program_id(1)
    scale = HEAD_DIM ** -0.5
    qs = [(_stack_heads(q_ref[...], 0) * scale).astype(BF16) for q_ref in (q0_ref, q1_ref)]

    def visit(kv_fn, b8):
        bias = jnp.concatenate([b8] * GROUP, axis=0)
        for g in range(N_KV):
            k, v = kv_fn(g)
            m, l, a = _softmax_block(qs[g], k, v, bias, m_sc[g], l_sc[g], acc_sc[g])
            m_sc[g] = m
            l_sc[g] = l
            acc_sc[g] = a

    @pl.when(j == 0)
    def _():
        m_sc[...] = jnp.full(m_sc.shape, NEG_BIG, F32)
        l_sc[...] = jnp.zeros(l_sc.shape, F32)
        acc_sc[...] = jnp.zeros(acc_sc.shape, F32)
        visit(lambda g: _new_token_kv(new_ref, g, page), bias_ref[0, :, n_pages * page:])

    off = pl.multiple_of(j * (pps * page), pps * page)
    visit(lambda g: _gather_kv(page_refs, g, page), bias_ref[0, :, pl.ds(off, pps * page)])

    @pl.when(j == pl.num_programs(1) - 1)
    def _():
        for g in range(N_KV):
            _unstack_heads(o_ref, g, acc_sc[g] / l_sc[g])


def _dsa_sample(page_table, ys, bias, cache, layer, pps):
    nseq, n_pages = page_table.shape
    rows = cache.shape[2]
    page = rows // KV_SLOTS
    n_cols = bias.shape[2]
    gw = GROUP * HEAD_DIM
    kvw = KV_SLOTS * HEAD_DIM

    def page_spec(i):
        return pl.BlockSpec((None, None, rows, HEAD_DIM),
                            lambda b, j, pt, i=i: (layer, pt[b, j * pps + i], 0, 0))

    grid_spec = pltpu.PrefetchScalarGridSpec(
        num_scalar_prefetch=1,
        grid=(nseq, n_pages // pps),
        in_specs=[pl.BlockSpec((TOK_PAD, gw), lambda b, j, pt: (b, COL_QB // gw)),
                  pl.BlockSpec((TOK_PAD, gw), lambda b, j, pt: (b, COL_QB // gw + 1)),
                  pl.BlockSpec((TOK_PAD, kvw), lambda b, j, pt: (b, COL_KVB // kvw)),
                  pl.BlockSpec((1, TOK_PAD, n_cols), lambda b, j, pt: (b, 0, 0))]
                 + [page_spec(i) for i in range(pps)],
        out_specs=pl.BlockSpec((TOK_PAD, N_HEADS * HEAD_DIM), lambda b, j, pt: (b, 0)),
        scratch_shapes=[pltpu.VMEM((N_KV, S_ROWS, LANES), F32)] * 3,
    )
    return pl.pallas_call(
        functools.partial(_dsa_sample_kernel, pps=pps, page=page, n_pages=n_pages),
        out_shape=jax.ShapeDtypeStruct((nseq * TOK_PAD, N_HEADS * HEAD_DIM), F32),
        grid_spec=grid_spec,
        compiler_params=_params(("parallel", "arbitrary")),
        name="dsa_sample",
    )(page_table, ys, ys, ys, bias, *([cache] * pps))


def _layer_norm(x, g, b):
    mu = jnp.mean(x, axis=-1, keepdims=True)
    xc = x - mu
    var = jnp.mean(xc * xc, axis=-1, keepdims=True)
    return xc * lax.rsqrt(var + LN_EPS) * g + b


def _merge_kernel(oa_ref, ob_ref, wa_ref, wb_ref, ga_ref, gb_ref, wo_ref, x_ref, g_ref, b_ref,
                  h_ref, hb_ref, acc_sc, *, alpha):
    kt = pl.program_id(1)

    @pl.when(kt == 0)
    def _():
        acc_sc[...] = jnp.zeros(acc_sc.shape, F32)

    oa = oa_ref[...].astype(BF16)
    ob = ob_ref[...].astype(BF16)
    mix = ga_ref[...] * _dot(oa, wa_ref[...]) + gb_ref[...] * _dot(ob, wb_ref[...])
    acc_sc[...] += _dot(mix.astype(BF16), wo_ref[...])

    @pl.when(kt == pl.num_programs(1) - 1)
    def _():
        h = _layer_norm(alpha * x_ref[...] + acc_sc[...], g_ref[...], b_ref[...])
        h_ref[...] = h
        hb_ref[...] = h.astype(BF16)


def _merge(oa, ob, y, x, wa, wb, wo, g, b, layer, alpha):
    rows = x.shape[0]
    tr = min(rows, 512)
    tk = 512
    kw = N_HEADS * HEAD_DIM
    return pl.pallas_call(
        functools.partial(_merge_kernel, alpha=alpha),
        out_shape=(jax.ShapeDtypeStruct((rows, D_MODEL), F32), jax.ShapeDtypeStruct((rows, D_MODEL), BF16)),
        grid=(rows // tr, D_MODEL // tk),
        in_specs=[pl.BlockSpec((tr, kw), lambda r, k: (r, 0)),
                  pl.BlockSpec((tr, kw), lambda r, k: (r, 0)),
                  pl.BlockSpec((None, kw, tk), lambda r, k: (layer, 0, k)),
                  pl.BlockSpec((None, kw, tk), lambda r, k: (layer, 0, k)),
                  pl.BlockSpec((tr, tk), lambda r, k: (r, COL_GA // tk + k)),
                  pl.BlockSpec((tr, tk), lambda r, k: (r, COL_GB // tk + k)),
                  pl.BlockSpec((None, tk, D_MODEL), lambda r, k: (layer, k, 0)),
                  pl.BlockSpec((tr, D_MODEL), lambda r, k: (r, 0)),
                  pl.BlockSpec((None, 1, D_MODEL), lambda r, k: (layer, 0, 0)),
                  pl.BlockSpec((None, 1, D_MODEL), lambda r, k: (layer, 0, 0))],
        out_specs=(pl.BlockSpec((tr, D_MODEL), lambda r, k: (r, 0)),
                   pl.BlockSpec((tr, D_MODEL), lambda r, k: (r, 0))),
        scratch_shapes=[pltpu.VMEM((tr, D_MODEL), F32)],
        compiler_params=_params(("parallel", "arbitrary")),
        name="merge_ln",
    )(oa, ob, wa, wb, y, y, wo, x, g, b)


def _gelu_tanh(x):
    return 0.5 * x * (1.0 + jnp.tanh(np.sqrt(2.0 / np.pi) * (x + 0.044715 * (x * x * x))))


HALO = 16


def _ffn_kernel(*refs, alpha, tr, seq_len, blocks_per_seq, prompt_mode):
    if prompt_mode:
        (hb_ref, halo_ref, w1a_ref, w1u_ref, cw_ref, cb_ref, w2_ref, h_ref, g_ref, b_ref,
         h2_ref, h2b_ref, a_ref, acc_sc) = refs
    else:
        (hb_ref, s1_ref, s2_ref, w1a_ref, w1u_ref, cw_ref, cb_ref, w2_ref, h_ref, g_ref, b_ref,
         h2_ref, h2b_ref, a_ref, acc_sc) = refs
    r = pl.program_id(0)
    ft = pl.program_id(1)

    @pl.when(ft == 0)
    def _():
        acc_sc[...] = jnp.zeros(acc_sc.shape, F32)

    hb = hb_ref[...]
    a = _dot(hb, w1a_ref[...])
    up = _dot(hb, w1u_ref[...])
    row = lax.broadcasted_iota(I32, a.shape, 0)
    p1 = pltpu.roll(a, 1, 0)
    p2 = pltpu.roll(a, 2, 0)
    if prompt_mode:
        a_halo = _dot(halo_ref[...], w1a_ref[...])
        keep = jnp.where(r % blocks_per_seq == 0, 0.0, 1.0)
        h6 = a_halo[HALO - 2:HALO - 1, :] * keep
        h7 = a_halo[HALO - 1:HALO, :] * keep
        p1 = jnp.where(row == 0, h7, p1)
        p2 = jnp.where(row == 0, h6, jnp.where(row == 1, h7, p2))
        a_ref[...] = a[tr - SUBLANES:, :]
    else:
        t = row & (seq_len - 1)
        p1 = jnp.where(t == 0, s1_ref[...], p1)
        p2 = jnp.where(t < 2, s2_ref[...], p2)
        a_ref[...] = a
    c = cb_ref[...] + cw_ref[0:1, :] * p2 + cw_ref[1:2, :] * p1 + cw_ref[2:3, :] * a
    hmid = (_gelu_tanh(c) * up).astype(BF16)
    acc_sc[...] += _dot(hmid, w2_ref[...])

    @pl.when(ft == pl.num_programs(1) - 1)
    def _():
        h2 = _layer_norm(alpha * h_ref[...] + acc_sc[...], g_ref[...], b_ref[...])
        h2_ref[...] = h2
        h2b_ref[...] = h2.astype(BF16)


def _ffn(h, hb, w1a, w1u, cw, cb, w2, g, b, layer, alpha, seq_len, state=None):
    rows = h.shape[0]
    prompt_mode = state is None
    tr = min(seq_len, 512) if prompt_mode else rows
    n_r = rows // tr
    n_f = D_FF_PAD // TF
    common_w = [pl.BlockSpec((None, D_MODEL, TF), lambda r, f: (layer, 0, f)),
                pl.BlockSpec((None, D_MODEL, TF), lambda r, f: (layer, 0, f)),
                pl.BlockSpec((None, SUBLANES, TF), lambda r, f: (layer, 0, f)),
                pl.BlockSpec((None, 1, TF), lambda r, f: (layer, 0, f)),
                pl.BlockSpec((None, TF, D_MODEL), lambda r, f: (layer, f, 0)),
                pl.BlockSpec((tr, D_MODEL), lambda r, f: (r, 0)),
                pl.BlockSpec((None, 1, D_MODEL), lambda r, f: (layer, 0, 0)),
                pl.BlockSpec((None, 1, D_MODEL), lambda r, f: (layer, 0, 0))]
    if prompt_mode:
        assert seq_len % tr == 0 and tr % HALO == 0
        per = tr // HALO
        extra_specs = [pl.BlockSpec((HALO, D_MODEL), lambda r, f: (jnp.maximum(r * per - 1, 0), 0))]
        extra = [hb]
        a_rows, a_blk = n_r * SUBLANES, SUBLANES
    else:
        assert seq_len & (seq_len - 1) == 0 and seq_len >= CONV_W - 1
        extra_specs = [pl.BlockSpec((tr, TF), lambda r, f: (0, f)),
                       pl.BlockSpec((tr, TF), lambda r, f: (0, f))]
        extra = list(state)
        a_rows, a_blk = rows, tr
    return pl.pallas_call(
        functools.partial(_ffn_kernel, alpha=alpha, tr=tr, seq_len=seq_len,
                          blocks_per_seq=max(seq_len // tr, 1), prompt_mode=prompt_mode),
        out_shape=(jax.ShapeDtypeStruct((rows, D_MODEL), F32), jax.ShapeDtypeStruct((rows, D_MODEL), BF16),
                   jax.ShapeDtypeStruct((a_rows, D_FF_PAD), F32)),
        grid=(n_r, n_f),
        in_specs=[pl.BlockSpec((tr, D_MODEL), lambda r, f: (r, 0))] + extra_specs + common_w,
        out_specs=(pl.BlockSpec((tr, D_MODEL), lambda r, f: (r, 0)),
                   pl.BlockSpec((tr, D_MODEL), lambda r, f: (r, 0)),
                   pl.BlockSpec((a_blk, TF), lambda r, f: (r, f))),
        scratch_shapes=[pltpu.VMEM((tr, D_MODEL), F32)],
        compiler_params=_params(("parallel", "arbitrary")),
        name="conv_ffn_ln",
    )(hb, *extra, w1a, w1u, cw, cb, w2, h, g, b)


def _ple_kernel(hb_ref, wg_ref, p_ref, wp_ref, h_ref, o_ref, ob_ref):
    gate = jax.nn.sigmoid(_dot(hb_ref[...], wg_ref[...]))
    out = h_ref[...] + gate * _dot(p_ref[...], wp_ref[...])
    o_ref[...] = out
    ob_ref[...] = out.astype(BF16)


def _ple(h2, h2b, pb, wg, wp, layer):
    rows = h2.shape[0]
    tr = min(rows, 1024)
    tn = 512
    return pl.pallas_call(
        _ple_kernel,
        out_shape=(jax.ShapeDtypeStruct((rows, D_MODEL), F32), jax.ShapeDtypeStruct((rows, D_MODEL), BF16)),
        grid=(rows // tr, D_MODEL // tn),
        in_specs=[pl.BlockSpec((tr, D_MODEL), lambda r, n: (r, 0)),
                  pl.BlockSpec((None, D_MODEL, tn), lambda r, n: (layer, 0, n)),
                  pl.BlockSpec((None, tr, PLE_DIM), lambda r, n: (layer, r, 0)),
                  pl.BlockSpec((None, PLE_DIM, tn), lambda r, n: (layer, 0, n)),
                  pl.BlockSpec((tr, tn), lambda r, n: (r, n))],
        out_specs=(pl.BlockSpec((tr, tn), lambda r, n: (r, n)),
                   pl.BlockSpec((tr, tn), lambda r, n: (r, n))),
        compiler_params=_params(("parallel", "arbitrary")),
        name="ple_gate",
    )(h2b, wg, pb, wp, h2)


def _pack_w_in(w_in):
    sizes = (1024, 256, 256, 1024, 256, 256, 1024, 64, 16, 2048, 2048)
    offs = np.concatenate([[0], np.cumsum(sizes)])
    qa, ka, va, qb, kb, vb, qi, ki, wi, ga, gb = [w_in[..., offs[i]:offs[i + 1]] for i in range(11)]
    pad = jnp.zeros(w_in.shape[:-1] + (TN - IDX_DIM - IDX_HEADS,), w_in.dtype)
    return jnp.concatenate([qa, ka, va, qb, kb, vb, qi, ki, wi, pad, ga, gb], axis=-1).astype(BF16)


def _rope_tables(pos):
    pos = pos.astype(F32)[:, None]

    def table(head_dim):
        rot = head_dim // 4
        half = rot // 2
        inv = ROPE_THETA ** (-(2.0 * jnp.arange(half, dtype=F32)) / rot)
        ang = pos * inv[None, :]
        cos, sin = jnp.cos(ang), jnp.sin(ang)
        ones = jnp.ones((pos.shape[0], head_dim - rot), F32)
        c = jnp.concatenate([cos, cos, ones], axis=1)
        s = jnp.concatenate([-sin, sin, 0.0 * ones], axis=1)
        reps = LANES // head_dim
        return jnp.tile(c, (1, reps)), jnp.tile(s, (1, reps))

    c128, s128 = table(HEAD_DIM)
    c64, s64 = table(IDX_DIM)
    return c128, s128, c64, s64


def kernel(x_prompt, x_sample, cache_sb_kv, cache_dsa_kv, cache_idx_k, state_ffn_conv, page_table,
           p_prompt, p_sample, w_in, w_branch_sb, w_branch_dsa, w_out, ln1_g, ln1_b, w_ffn_in,
           ffn_conv_w, ffn_conv_b, w_ffn_out, ln2_g, ln2_b, w_ple_gate, w_ple_proj):
    batch, seq = x_prompt.shape[:2]
    nseq, n_tok = x_sample.shape[:2]
    depth = w_in.shape[0]
    n_pool, page = cache_sb_kv.shape[1:3]
    n_pages = page_table.shape[1]
    past_len = n_pages * page
    alpha = (2 * depth) ** 0.25
    kv_w = 2 * N_KV * HEAD_DIM
    top_s = max(1, min(TOPK_MAX, (past_len + n_tok) // 4))
    pps = min(32, n_pages)
    pps_idx = min(64, n_pages)

    w_in_p = _pack_w_in(w_in)
    wa = w_branch_sb.astype(BF16)
    wb = w_branch_dsa.astype(BF16)
    wo = w_out.astype(BF16)
    ff_pad = D_FF_PAD - D_FF
    w1a = jnp.pad(w_ffn_in[..., :D_FF], ((0, 0), (0, 0), (0, ff_pad))).astype(BF16)
    w1u = jnp.pad(w_ffn_in[..., D_FF:], ((0, 0), (0, 0), (0, ff_pad))).astype(BF16)
    w2 = jnp.pad(w_ffn_out, ((0, 0), (0, ff_pad), (0, 0))).astype(BF16)
    cw = jnp.pad(ffn_conv_w, ((0, 0), (0, SUBLANES - CONV_W), (0, ff_pad)))
    cb = jnp.pad(ffn_conv_b, ((0, 0), (0, ff_pad)))[:, None, :]
    wg = w_ple_gate.astype(BF16)
    wp = w_ple_proj.astype(BF16)
    g1, b1 = ln1_g[:, None, :], ln1_b[:, None, :]
    g2, b2 = ln2_g[:, None, :], ln2_b[:, None, :]

    tabs_p = _rope_tables(jnp.tile(jnp.arange(seq, dtype=jnp.int32), batch))
    tabs_s = _rope_tables(jnp.tile(past_len + jnp.arange(TOK_PAD, dtype=jnp.int32), nseq))

    sb_pages = cache_sb_kv.reshape(depth, n_pool, page * KV_SLOTS, HEAD_DIM)
    dsa_pages = cache_dsa_kv.reshape(depth, n_pool, page * KV_SLOTS, HEAD_DIM)
    idx_pages_t = jnp.swapaxes(cache_idx_k, 2, 3)

    assert n_tok <= TOK_PAD
    tok_pad = ((0, 0), (0, TOK_PAD - n_tok), (0, 0))
    xp = x_prompt.reshape(batch * seq, D_MODEL)
    xs = jnp.pad(x_sample, tok_pad).reshape(nseq * TOK_PAD, D_MODEL)
    xpb, xsb = xp.astype(BF16), xs.astype(BF16)
    ppb = p_prompt.reshape(depth, batch * seq, PLE_DIM).astype(BF16)
    psb = jnp.pad(p_sample, ((0, 0),) + tok_pad).reshape(depth, nseq * TOK_PAD, PLE_DIM).astype(BF16)
    st = jnp.pad(state_ffn_conv, ((0, 0), (0, 0), (0, 0), (0, ff_pad)))
    conv_s1 = jnp.repeat(st[:, :, 1], TOK_PAD, axis=1)
    conv_s2 = jnp.pad(st, ((0, 0), (0, 0), (0, TOK_PAD - (CONV_W - 1)), (0, 0))).reshape(
        depth, nseq * TOK_PAD, D_FF_PAD)

    outs = {k: [] for k in ("sb_p", "dsa_p", "idx_p", "conv_p", "sb_s", "dsa_s", "idx_s", "conv_s")}
    for l in range(depth):
        y = _project(xpb, w_in_p, l, tabs_p)
        oa = _sb_prompt(y, batch, seq)
        ob = _dsa_prompt(y, batch, seq)
        h, hb = _merge(oa, ob, y, xp, wa, wb, wo, g1, b1, l, alpha)
        h2, h2b, a_tail = _ffn(h, hb, w1a, w1u, cw, cb, w2, g2, b2, l, alpha, seq)
        xp, xpb = _ple(h2, h2b, ppb, wg, wp, l)
        outs["sb_p"].append(y[:, COL_KVA:COL_KVA + kv_w].reshape(batch, seq, 2, N_KV, HEAD_DIM))
        outs["dsa_p"].append(y[:, COL_KVB:COL_KVB + kv_w].reshape(batch, seq, 2, N_KV, HEAD_DIM))
        outs["idx_p"].append(y[:, COL_KIW:COL_KIW + IDX_DIM].reshape(batch, seq, IDX_DIM))
        tails = a_tail.reshape(batch, -1, SUBLANES, D_FF_PAD)[:, -1, SUBLANES - (CONV_W - 1):, :D_FF]
        outs["conv_p"].append(tails)

        ys = _project(xsb, w_in_p, l, tabs_s)
        oa_s = _sb_sample(page_table, ys, sb_pages, l, pps)
        bias = _idx_sample(page_table, ys, idx_pages_t, l, pps_idx, top_s, n_tok)
        ob_s = _dsa_sample(page_table, ys, bias, dsa_pages, l, pps)
        hs, hsb = _merge(oa_s, ob_s, ys, xs, wa, wb, wo, g1, b1, l, alpha)
        h2s, h2sb, a_s = _ffn(hs, hsb, w1a, w1u, cw, cb, w2, g2, b2, l, alpha,
                              TOK_PAD, state=(conv_s1[l], conv_s2[l]))
        xs, xsb = _ple(h2s, h2sb, psb, wg, wp, l)
        ys_tok = ys.reshape(nseq, TOK_PAD, IN_COLS)[:, :n_tok]
        outs["sb_s"].append(ys_tok[..., COL_KVA:COL_KVA + kv_w].reshape(nseq, n_tok, 2, N_KV, HEAD_DIM))
        outs["dsa_s"].append(ys_tok[..., COL_KVB:COL_KVB + kv_w].reshape(nseq, n_tok, 2, N_KV, HEAD_DIM))
        outs["idx_s"].append(ys_tok[..., COL_KIW:COL_KIW + IDX_DIM])
        outs["conv_s"].append(a_s.reshape(nseq, TOK_PAD, D_FF_PAD)[:, n_tok - (CONV_W - 1):n_tok, :D_FF])

    return (xp.reshape(batch, seq, D_MODEL), xs.reshape(nseq, TOK_PAD, D_MODEL)[:, :n_tok],
            jnp.stack(outs["sb_p"]), jnp.stack(outs["dsa_p"]), jnp.stack(outs["idx_p"]),
            jnp.stack(outs["conv_p"]), jnp.stack(outs["sb_s"]), jnp.stack(outs["dsa_s"]),
            jnp.stack(outs["idx_s"]), jnp.stack(outs["conv_s"]))
```

```python
import functools

import jax
import jax.numpy as jnp
import numpy as np
from jax import lax
from jax.experimental import pallas as pl
from jax.experimental.pallas import tpu as pltpu

F32 = jnp.float32
BF16 = jnp.bfloat16
I32 = jnp.int32

D_MODEL = 2048
HEAD_DIM = 128
N_HEADS = 8
N_KV = 2
GROUP = N_HEADS // N_KV
IDX_HEADS = 16
IDX_DIM = 64
TOPK_MAX = 256
ROPE_THETA = 500000.0
D_FF = 5504
CONV_W = 3
PLE_DIM = 256
LN_EPS = 1e-5

LANES = 128
SUBLANES = 8
VMEM_LIMIT = 56 * 1024 * 1024

TN = 512
COL_QA = 0
COL_KVA = 1024
COL_QB = 1536
COL_KVB = 2560
COL_QI = 3072
COL_KIW = 4096
COL_GA = 4608
COL_GB = 6656
IN_COLS = 8704
N_IN_TILES = IN_COLS // TN
D_FF_PAD = 5632
TF = 512

SB_Q_SCALE = (HEAD_DIM ** -0.5) * float(np.log2(np.e))
SB_STACK_ROWS = 64
NEG_BIG = -1e30
KEY_NEG_INF = np.int32(np.array(0xFF800000, dtype=np.uint32).view(np.int32) ^ 0x7FFFFFFF)
INT_MIN = np.int32(-2 ** 31)


def _params(sem):
    return pltpu.CompilerParams(dimension_semantics=sem, vmem_limit_bytes=VMEM_LIMIT)


def _dot_t(a, b):
    return lax.dot_general(a, b, (((1,), (1,)), ((), ())), preferred_element_type=F32)


def _dot(a, b):
    return jnp.dot(a, b, preferred_element_type=F32)


def _rope(y, c, s, half):
    w = y.shape[1]
    reps = w // LANES
    if reps > 1:
        c = jnp.concatenate([c] * reps, axis=1)
        s = jnp.concatenate([s] * reps, axis=1)
    lane = lax.broadcasted_iota(I32, y.shape, 1)
    first = (lane & (2 * half - 1)) < half
    partner = jnp.where(first, pltpu.roll(y, w - half, 1), pltpu.roll(y, half, 1))
    return y * c + partner * s


def _proj_kernel(x_ref, w_ref, c128_ref, s128_ref, c64_ref, s64_ref, y_ref):
    n = pl.program_id(1)

    def product():
        return _dot(x_ref[...], w_ref[...])

    @pl.when(n < COL_QB // TN)
    def _():
        y_ref[...] = product()

    @pl.when(jnp.logical_and(n >= COL_QB // TN, n < COL_KVB // TN))
    def _():
        y_ref[...] = _rope(product(), c128_ref[...], s128_ref[...], 16)

    @pl.when(n == COL_KVB // TN)
    def _():
        y = product()
        y_ref[:, :256] = _rope(y[:, :256], c128_ref[...], s128_ref[...], 16)
        y_ref[:, 256:] = y[:, 256:]

    @pl.when(jnp.logical_and(n >= COL_QI // TN, n < COL_KIW // TN))
    def _():
        y_ref[...] = _rope(product(), c64_ref[...], s64_ref[...], 8)

    @pl.when(n == COL_KIW // TN)
    def _():
        y = product()
        lane = lax.broadcasted_iota(I32, c64_ref.shape, 1)
        c = jnp.where(lane < IDX_DIM, c64_ref[...], 1.0)
        s = jnp.where(lane < IDX_DIM, s64_ref[...], 0.0)
        y_ref[:, :LANES] = _rope(y[:, :LANES], c, s, 8)
        y_ref[:, LANES:] = y[:, LANES:]

    @pl.when(n >= COL_GA // TN)
    def _():
        y_ref[...] = jax.nn.sigmoid(product())


def _project(xb, w, layer, tabs):
    rows = xb.shape[0]
    tr = min(rows, 1024)
    tab_spec = pl.BlockSpec((tr, LANES), lambda r, n: (r, 0))
    return pl.pallas_call(
        _proj_kernel,
        out_shape=jax.ShapeDtypeStruct((rows, IN_COLS), F32),
        grid=(rows // tr, N_IN_TILES),
        in_specs=[pl.BlockSpec((tr, D_MODEL), lambda r, n: (r, 0)),
                  pl.BlockSpec((None, D_MODEL, TN), lambda r, n: (layer, 0, n)),
                  tab_spec, tab_spec, tab_spec, tab_spec],
        out_specs=pl.BlockSpec((tr, TN), lambda r, n: (r, n)),
        compiler_params=_params(("parallel", "arbitrary")),
        name="in_proj",
    )(xb, w, *tabs)


def _suffix_matrix():
    j = lax.broadcasted_iota(I32, (2 * LANES, 2 * LANES), 0) & (LANES - 1)
    s = lax.broadcasted_iota(I32, (2 * LANES, 2 * LANES), 1)
    return jnp.where(jnp.logical_or(j > s, s >= LANES), -1.0, 0.0).astype(BF16)


def _sb_scores(q, k, mask):
    n_sub = k.shape[0] // LANES
    z = _dot_t(q, k)
    neg_abs = pltpu.bitcast(pltpu.bitcast(z, I32) | INT_MIN, F32)
    sp = jnp.maximum(z, 0.0) + jnp.log2(1.0 + jnp.exp2(neg_abs))
    spm = sp if mask is None else jnp.where(mask, sp, 0.0)
    hi = spm.astype(BF16)
    lo = (spm - hi.astype(F32)).astype(BF16)
    subs = [jnp.concatenate([hi[:, i * LANES:(i + 1) * LANES], lo[:, i * LANES:(i + 1) * LANES]], axis=1)
            for i in range(n_sub)]
    return z - sp, subs


def _sb_weights(d, subs, v, u, carry, mask):
    n_sub = len(subs)
    m_rows = d.shape[0]
    if m_rows <= SB_STACK_ROWS:
        r_all = _dot(subs[0] if n_sub == 1 else jnp.concatenate(subs, axis=0), u)
        rs = [r_all[i * m_rows:(i + 1) * m_rows] for i in range(n_sub)]
    else:
        rs = [_dot(s, u) for s in subs]
    afters = [None] * n_sub
    for i in reversed(range(n_sub)):
        afters[i] = rs[i][:, :LANES] + carry
        carry = carry + rs[i][:, LANES:]
    after = afters[0] if n_sub == 1 else jnp.concatenate(afters, axis=1)
    w = jnp.exp2(d + after)
    if mask is not None:
        w = jnp.where(mask, w, 0.0)
    return _dot(w.astype(BF16), v), carry


def _sb_block(q, k, v, u, carry, mask):
    d, subs = _sb_scores(q, k, mask)
    return _sb_weights(d, subs, v, u, carry, mask)


def _softmax_block(q, k, v, bias, m_old, l_old, acc_old):
    logit = _dot_t(q, k) + bias
    m_new = jnp.maximum(m_old, jnp.max(logit, axis=1, keepdims=True))
    p = jnp.exp(logit - m_new[:, 0:1])
    alpha = jnp.exp(m_old - m_new)
    l_new = alpha * l_old + jnp.sum(p, axis=1, keepdims=True)
    acc_new = alpha[:, 0:1] * acc_old + _dot(p.astype(BF16), v)
    return m_new, l_new, acc_new


def _sortable(x):
    b = pltpu.bitcast(x, I32)
    return jnp.where(b < 0, b ^ jnp.int32(0x7FFFFFFF), b)


def _kth_largest(count_ge, shape, k, n_total):
    bits_per_check = 4

    def cond(state):
        i, _, cnt = state
        return jnp.logical_and(i < 32, jnp.max(cnt) > k)

    def body(state):
        i, t, cnt = state
        for _ in range(bits_per_check):
            cand = t + jnp.left_shift(jnp.int32(1), jnp.int32(31) - i)
            c = count_ge(cand)
            ok = c >= k
            i, t, cnt = i + 1, jnp.where(ok, cand, t), jnp.where(ok, c, cnt)
        return i, t, cnt

    state = (jnp.int32(0), jnp.full(shape, INT_MIN, I32), jnp.full(shape, n_total, I32))
    _, t, cnt = lax.while_loop(cond, body, state)
    return t, cnt


def _tie_cutoff(count_tie_below, shape, need, n_bits):
    def body(i, j):
        cand = j + jnp.left_shift(jnp.int32(1), jnp.int32(n_bits - 1) - i)
        return jnp.where(count_tie_below(cand) < need, cand, j)

    return lax.fori_loop(0, n_bits, body, jnp.zeros(shape, I32))


def _sb_prompt_kernel(q_ref, k_ref, v_ref, o_ref, kb_sc, vb_sc, carry_sc, acc_sc, *, tq):
    iq = pl.program_id(1)

    @pl.when(iq == 0)
    def _():
        kb_sc[...] = k_ref[...].astype(BF16)
        vb_sc[...] = v_ref[...].astype(BF16)

    scale = SB_Q_SCALE
    u = _suffix_matrix()
    n_chain = N_HEADS // 2
    c_rows = 2 * tq
    qs = []
    for c in range(n_chain):
        q = jnp.concatenate([q_ref[:, h * HEAD_DIM:(h + 1) * HEAD_DIM] for h in (2 * c, 2 * c + 1)], axis=0)
        qs.append((q * scale).astype(BF16))

    def kv_block(o2, c):
        g = (2 * c) // GROUP
        return (kb_sc[pl.ds(o2, tq), g * HEAD_DIM:(g + 1) * HEAD_DIM],
                vb_sc[pl.ds(o2, tq), g * HEAD_DIM:(g + 1) * HEAD_DIM])

    row = lax.broadcasted_iota(I32, (c_rows, tq), 0) & (tq - 1)
    col = lax.broadcasted_iota(I32, (c_rows, tq), 1)
    off = pl.multiple_of(iq * tq, tq)
    for c in range(n_chain):
        k, v = kv_block(off, c)
        contrib, carry = _sb_block(qs[c], k, v, u, jnp.zeros((c_rows, LANES), F32), col < row)
        acc_sc[c] = contrib
        carry_sc[c] = carry

    def body(i, _):
        o2 = pl.multiple_of((iq - 1 - i) * tq, tq)
        for c in range(n_chain):
            k, v = kv_block(o2, c)
            contrib, carry = _sb_block(qs[c], k, v, u, carry_sc[c], None)
            acc_sc[c] += contrib
            carry_sc[c] = carry
        return 0

    lax.fori_loop(0, iq, body, 0)
    for h in range(N_HEADS):
        r0 = (h % 2) * tq
        o_ref[:, h * HEAD_DIM:(h + 1) * HEAD_DIM] = acc_sc[h // 2, r0:r0 + tq, :].astype(o_ref.dtype)


def _sb_prompt(y, batch, seq):
    tq = min(2 * LANES, seq)
    nq = seq // tq
    qw = N_HEADS * HEAD_DIM
    kvw = N_KV * HEAD_DIM
    return pl.pallas_call(
        functools.partial(_sb_prompt_kernel, tq=tq),
        out_shape=jax.ShapeDtypeStruct((batch * seq, qw), BF16),
        grid=(batch, nq),
        in_specs=[pl.BlockSpec((tq, qw), lambda b, i: (b * nq + i, COL_QA // qw)),
                  pl.BlockSpec((seq, kvw), lambda b, i: (b, COL_KVA // kvw)),
                  pl.BlockSpec((seq, kvw), lambda b, i: (b, COL_KVA // kvw + 1))],
        out_specs=pl.BlockSpec((tq, qw), lambda b, i: (b * nq + i, 0)),
        scratch_shapes=[pltpu.VMEM((seq, kvw), BF16), pltpu.VMEM((seq, kvw), BF16),
                        pltpu.VMEM((N_HEADS // 2, 2 * tq, LANES), F32),
                        pltpu.VMEM((N_HEADS // 2, 2 * tq, LANES), F32)],
        compiler_params=_params(("parallel", "arbitrary")),
        name="sb_prompt",
    )(y, y, y)


TKI = 256


def _fold_rows(x, op):
    return op(x.reshape(x.shape[0] // SUBLANES, SUBLANES, x.shape[1]), axis=0)


def _dsa_prompt_kernel(qb0_ref, qb1_ref, k_ref, v_ref, qi_ref, kiw_all_ref, kiw_q_ref, o_ref,
                       kb_sc, vb_sc, ki2_sc, key_sc, bias_sc, cut_sc, m_sc, l_sc, acc_sc,
                       *, tq, n_top, idx_bits):
    iq = pl.program_id(1)
    seq = k_ref.shape[0]

    @pl.when(iq == 0)
    def _():
        kb_sc[...] = k_ref[...].astype(BF16)
        vb_sc[...] = v_ref[...].astype(BF16)
        lane = lax.broadcasted_iota(I32, (seq, LANES), 1)
        kia = jnp.where(lane < IDX_DIM, kiw_all_ref[...], 0.0)
        ki2_sc[0] = kia.astype(BF16)
        ki2_sc[1] = pltpu.roll(kia, IDX_DIM, 1).astype(BF16)

    w_t = (kiw_q_ref[...] * ((IDX_DIM ** -0.5) * (IDX_HEADS ** -0.5))).T
    w_rows = [w_t[IDX_DIM + h:IDX_DIM + h + 1, :] for h in range(IDX_HEADS)]

    q_pairs = [qi_ref[:, p * LANES:(p + 1) * LANES].astype(BF16) for p in range(IDX_HEADS // 2)]
    q_pos = iq * tq + lax.broadcasted_iota(I32, (1, tq), 1)
    n_blk = (iq * tq + tq + TKI - 1) // TKI

    def idx_body(j, _):
        off = pl.multiple_of(j * TKI, TKI)
        ki_e = ki2_sc[0, pl.ds(off, TKI), :]
        ki_o = ki2_sc[1, pl.ds(off, TKI), :]
        acc = jnp.zeros((TKI, tq), F32)
        for p in range(IDX_HEADS // 2):
            acc = acc + jnp.maximum(_dot_t(ki_e, q_pairs[p]), 0.0) * w_rows[2 * p]
            acc = acc + jnp.maximum(_dot_t(ki_o, q_pairs[p]), 0.0) * w_rows[2 * p + 1]
        k_pos = off + lax.broadcasted_iota(I32, (TKI, 1), 0)
        key_sc[pl.ds(off, TKI), :] = jnp.where(k_pos <= q_pos, _sortable(acc), KEY_NEG_INF)
        return 0

    lax.fori_loop(0, n_blk, idx_body, 0)

    def query_counts(hit_fn):
        def body(j, c):
            off = pl.multiple_of(j * TKI, TKI)
            return c + _fold_rows(hit_fn(key_sc[pl.ds(off, TKI), :], off), jnp.sum)

        c = lax.fori_loop(0, n_blk, body, jnp.zeros((SUBLANES, tq), I32))
        return jnp.sum(c, axis=0, keepdims=True)

    def count_ge(t):
        return query_counts(lambda kb, off: jnp.where(kb >= t, 1, 0))

    thr, n_ge = _kth_largest(count_ge, (1, tq), n_top, n_blk * TKI)

    cut_sc[...] = jnp.full(cut_sc.shape, seq, I32)

    @pl.when(jnp.max(n_ge) > n_top)
    def _():
        need = n_top - count_ge(thr + 1)

        def count_tie_below(jc):
            def hit(kb, off):
                idx = off + lax.broadcasted_iota(I32, kb.shape, 0)
                return jnp.where(kb == thr, jnp.where(idx < jc, 1, 0), 0)
            return query_counts(hit)

        cut = _tie_cutoff(count_tie_below, (1, tq), need, idx_bits)
        cut_sc[...] = jnp.broadcast_to(cut, cut_sc.shape)

    cut = cut_sc[0:1, :]

    def bias_body(j, _):
        off = pl.multiple_of(j * TKI, TKI)
        kb = key_sc[pl.ds(off, TKI), :]
        idx = off + lax.broadcasted_iota(I32, (TKI, tq), 0)
        tie = jnp.where(kb == thr, jnp.where(idx <= cut, 0.0, -jnp.inf), -jnp.inf)
        sel = jnp.where(kb > thr, 0.0, tie)
        sel = jnp.where(kb > KEY_NEG_INF, sel, -jnp.inf)
        bias_sc[:, pl.ds(off, TKI)] = sel.T
        return 0

    lax.fori_loop(0, n_blk, bias_body, 0)

    scale = (HEAD_DIM ** -0.5) * np.log2(np.e)
    n_chain = N_HEADS // 2
    n_sub = TKI // LANES
    qs = []
    for c in range(n_chain):
        q_ref = (qb0_ref, qb1_ref)[c // 2]
        hs = (2 * (c % 2), 2 * (c % 2) + 1)
        q = jnp.concatenate([q_ref[:, h * HEAD_DIM:(h + 1) * HEAD_DIM] for h in hs], axis=0)
        qs.append((q * scale).astype(BF16))

    def bias_rows(off):
        b = bias_sc[:, pl.ds(off, TKI)]
        return jnp.concatenate([b, b], axis=0)

    m_sc[...] = jnp.full(m_sc.shape, -jnp.inf, F32)

    def max_body(j, _):
        off = pl.multiple_of(j * TKI, TKI)
        bias = bias_rows(off)
        for c in range(n_chain):
            g = c // 2
            lg = _dot_t(qs[c], kb_sc[pl.ds(off, TKI), g * HEAD_DIM:(g + 1) * HEAD_DIM]) + bias
            m = m_sc[c]
            for i in range(n_sub):
                m = jnp.maximum(m, lg[:, i * LANES:(i + 1) * LANES])
            m_sc[c] = m
        return 0

    lax.fori_loop(0, n_blk, max_body, 0)

    for c in range(n_chain):
        m_sc[c] = jnp.broadcast_to(jnp.max(m_sc[c], axis=1, keepdims=True), (2 * tq, LANES))
    l_sc[...] = jnp.zeros(l_sc.shape, F32)
    acc_sc[...] = jnp.zeros(acc_sc.shape, F32)

    def sum_body(j, _):
        off = pl.multiple_of(j * TKI, TKI)
        bias = bias_rows(off)
        for c in range(n_chain):
            g = c // 2
            lg = _dot_t(qs[c], kb_sc[pl.ds(off, TKI), g * HEAD_DIM:(g + 1) * HEAD_DIM]) + bias
            p = jnp.exp2(lg - jnp.concatenate([m_sc[c]] * n_sub, axis=1))
            l = l_sc[c]
            for i in range(n_sub):
                l = l + p[:, i * LANES:(i + 1) * LANES]
            l_sc[c] = l
            acc_sc[c] += _dot(p.astype(BF16), vb_sc[pl.ds(off, TKI), g * HEAD_DIM:(g + 1) * HEAD_DIM])
        return 0

    lax.fori_loop(0, n_blk, sum_body, 0)
    for c in range(n_chain):
        out = acc_sc[c] / jnp.sum(l_sc[c], axis=1, keepdims=True)
        for hh in range(2):
            c0 = (2 * c + hh) * HEAD_DIM
            o_ref[:, c0:c0 + HEAD_DIM] = out[hh * tq:(hh + 1) * tq, :].astype(o_ref.dtype)


def _dsa_prompt(y, batch, seq):
    tq = min(2 * LANES, seq)
    assert tq == TKI or seq == tq
    nq = seq // tq
    gw = GROUP * HEAD_DIM
    kvw = N_KV * HEAD_DIM
    n_top = max(1, min(TOPK_MAX, seq // 4))
    idx_bits = int(seq).bit_length()
    q_rows = 2 * tq
    return pl.pallas_call(
        functools.partial(_dsa_prompt_kernel, tq=tq, n_top=n_top, idx_bits=idx_bits),
        out_shape=jax.ShapeDtypeStruct((batch * seq, N_HEADS * HEAD_DIM), BF16),
        grid=(batch, nq),
        in_specs=[pl.BlockSpec((tq, gw), lambda b, i: (b * nq + i, COL_QB // gw)),
                  pl.BlockSpec((tq, gw), lambda b, i: (b * nq + i, COL_QB // gw + 1)),
                  pl.BlockSpec((seq, kvw), lambda b, i: (b, COL_KVB // kvw)),
                  pl.BlockSpec((seq, kvw), lambda b, i: (b, COL_KVB // kvw + 1)),
                  pl.BlockSpec((tq, IDX_HEADS * IDX_DIM), lambda b, i: (b * nq + i, COL_QI // (IDX_HEADS * IDX_DIM))),
                  pl.BlockSpec((seq, LANES), lambda b, i: (b, COL_KIW // LANES)),
                  pl.BlockSpec((tq, LANES), lambda b, i: (b * nq + i, COL_KIW // LANES))],
        out_specs=pl.BlockSpec((tq, N_HEADS * HEAD_DIM), lambda b, i: (b * nq + i, 0)),
        scratch_shapes=[pltpu.VMEM((seq, kvw), BF16),
                        pltpu.VMEM((seq, kvw), BF16),
                        pltpu.VMEM((2, seq, LANES), BF16),
                        pltpu.VMEM((seq, tq), I32),
                        pltpu.VMEM((tq, seq), F32),
                        pltpu.VMEM((SUBLANES, tq), I32),
                        pltpu.VMEM((N_HEADS // 2, q_rows, LANES), F32),
                        pltpu.VMEM((N_HEADS // 2, q_rows, LANES), F32),
                        pltpu.VMEM((N_HEADS // 2, q_rows, HEAD_DIM), F32)],
        compiler_params=_params(("parallel", "arbitrary")),
        name="dsa_prompt",
    )(y, y, y, y, y, y, y)


TOK_PAD = SUBLANES
S_ROWS = GROUP * TOK_PAD


KV_SLOTS = 2 * N_KV


def _page_rows(ref, slot, page):
    return ref[pl.ds(slot, page, stride=KV_SLOTS), :].astype(BF16)


J_ROWS = N_KV * S_ROWS
J_LANES = N_KV * HEAD_DIM


def _gather_kv(page_refs, page):
    k = [jnp.concatenate([_page_rows(r, g, page) for g in range(N_KV)], axis=1) for r in page_refs]
    v = [jnp.concatenate([_page_rows(r, N_KV + g, page) for g in range(N_KV)], axis=1) for r in page_refs]
    if len(page_refs) == 1:
        return k[0], v[0]
    return jnp.concatenate(k, axis=0), jnp.concatenate(v, axis=0)


def _joint_queries(q_blks, scale):
    rows = []
    for g, q_blk in enumerate(q_blks):
        q = jnp.concatenate([q_blk[:, h * HEAD_DIM:(h + 1) * HEAD_DIM] for h in range(GROUP)], axis=0) * scale
        zero = jnp.zeros_like(q)
        rows.append(jnp.concatenate([q if gg == g else zero for gg in range(N_KV)], axis=1))
    return jnp.concatenate(rows, axis=0).astype(BF16)


def _new_token_kv(new_ref, page):
    pad = jnp.zeros((page - TOK_PAD, J_LANES), F32)
    k = jnp.concatenate([new_ref[:, :J_LANES], pad], axis=0)
    v = jnp.concatenate([new_ref[:, J_LANES:], pad], axis=0)
    return k.astype(BF16), v.astype(BF16)


def _unstack_heads(o_ref, acc):
    for g in range(N_KV):
        for h in range(GROUP):
            r0 = g * S_ROWS + h * TOK_PAD
            c0 = (g * GROUP + h) * HEAD_DIM
            o_ref[:, c0:c0 + HEAD_DIM] = acc[r0:r0 + TOK_PAD, g * HEAD_DIM:(g + 1) * HEAD_DIM]


def _sb_sample_kernel(pt_ref, q_ref, new_ref, *rest, pps, page):
    page_refs = rest[:pps]
    o_ref = rest[pps]
    carry_sc, acc_sc = rest[pps + 1:]
    j = pl.program_id(1)
    u = _suffix_matrix()
    gw = GROUP * HEAD_DIM
    q = _joint_queries([q_ref[:, g * gw:(g + 1) * gw] for g in range(N_KV)], SB_Q_SCALE)

    def visit(k, v, mask):
        c, cr = _sb_block(q, k, v, u, carry_sc[...], mask)
        acc_sc[...] += c
        carry_sc[...] = cr

    @pl.when(j == 0)
    def _():
        carry_sc[...] = jnp.zeros(carry_sc.shape, F32)
        acc_sc[...] = jnp.zeros(acc_sc.shape, F32)
        tok = lax.broadcasted_iota(I32, (J_ROWS, page), 0) & (TOK_PAD - 1)
        col = lax.broadcasted_iota(I32, (J_ROWS, page), 1)
        visit(*_new_token_kv(new_ref, page), col < tok)

    visit(*_gather_kv(page_refs, page), None)

    @pl.when(j == pl.num_programs(1) - 1)
    def _():
        _unstack_heads(o_ref, acc_sc[...])


def _sb_sample(page_table, ys, cache, layer, pps):
    nseq, n_pages = page_table.shape
    rows = cache.shape[2]
    page = rows // KV_SLOTS
    n_steps = n_pages // pps
    qw = N_HEADS * HEAD_DIM
    kvw = KV_SLOTS * HEAD_DIM

    def page_spec(i):
        return pl.BlockSpec((None, None, rows, HEAD_DIM),
                            lambda b, j, pt, i=i: (layer, pt[b, (n_steps - 1 - j) * pps + i], 0, 0))

    grid_spec = pltpu.PrefetchScalarGridSpec(
        num_scalar_prefetch=1,
        grid=(nseq, n_steps),
        in_specs=[pl.BlockSpec((TOK_PAD, qw), lambda b, j, pt: (b, COL_QA // qw)),
                  pl.BlockSpec((TOK_PAD, kvw), lambda b, j, pt: (b, COL_KVA // kvw))]
                 + [page_spec(i) for i in range(pps)],
        out_specs=pl.BlockSpec((TOK_PAD, qw), lambda b, j, pt: (b, 0)),
        scratch_shapes=[pltpu.VMEM((J_ROWS, LANES), F32), pltpu.VMEM((J_ROWS, J_LANES), F32)],
    )
    return pl.pallas_call(
        functools.partial(_sb_sample_kernel, pps=pps, page=page),
        out_shape=jax.ShapeDtypeStruct((nseq * TOK_PAD, qw), F32),
        grid_spec=grid_spec,
        compiler_params=_params(("parallel", "arbitrary")),
        name="sb_sample",
    )(page_table, ys, ys, *([cache] * pps))


def _idx_sample_kernel(pt_ref, qi_ref, kiw_ref, *rest, pps, page, n_pages, n_top, idx_bits, n_tok):
    page_refs = rest[:pps]
    bias_ref = rest[pps]
    score_sc = rest[pps + 1]
    j = pl.program_id(1)

    row = lax.broadcasted_iota(I32, (TOK_PAD, 1), 0)

    def real_rows(x):
        return jnp.where(row < n_tok, x, pltpu.roll(x, n_tok, 0))

    qi_rows = real_rows(qi_ref[...])
    kiw = real_rows(kiw_ref[...])
    qi = jnp.concatenate([qi_rows[:, h * IDX_DIM:(h + 1) * IDX_DIM] for h in range(IDX_HEADS)],
                         axis=0).astype(BF16)
    w_scale = (IDX_DIM ** -0.5) * (IDX_HEADS ** -0.5)
    wm = jnp.concatenate([jnp.broadcast_to(kiw[:, IDX_DIM + h:IDX_DIM + h + 1] * w_scale, (TOK_PAD, LANES))
                          for h in range(IDX_HEADS)], axis=0)

    def head_sum(s):
        n = s.shape[1]
        w = wm if n == LANES else jnp.concatenate([wm] * (n // LANES), axis=1)
        return jnp.sum((jnp.maximum(s, 0.0) * w).reshape(IDX_HEADS, TOK_PAD, n), axis=0)

    @pl.when(j == 0)
    def _():
        ki_new = jnp.concatenate([kiw_ref[:, :IDX_DIM], jnp.zeros((page - TOK_PAD, IDX_DIM), F32)], axis=0)
        tok = lax.broadcasted_iota(I32, (TOK_PAD, page), 0) & (n_tok - 1)
        col = lax.broadcasted_iota(I32, (TOK_PAD, page), 1)
        s_new = head_sum(_dot_t(qi, ki_new.astype(BF16)))
        score_sc[:, n_pages * page:] = jnp.where(col <= tok, s_new, -jnp.inf)

    off = pl.multiple_of(j * (pps * page), pps * page)
    ki_t = jnp.concatenate([r[...] for r in page_refs], axis=1)
    score_sc[:, pl.ds(off, pps * page)] = head_sum(_dot(qi, ki_t.astype(BF16)))

    @pl.when(j == pl.num_programs(1) - 1)
    def _():
        key = _sortable(score_sc[...])
        idx = lax.broadcasted_iota(I32, key.shape, 1)

        def count_ge(t):
            return jnp.sum(jnp.where(key >= t, 1, 0), axis=1, keepdims=True)

        thr, n_ge = _kth_largest(count_ge, (TOK_PAD, 1), n_top, key.shape[1])

        def search_cut():
            need = n_top - count_ge(thr + 1)

            def count_tie_below(jc):
                return jnp.sum(jnp.where(key == thr, jnp.where(idx < jc, 1, 0), 0), axis=1, keepdims=True)

            return _tie_cutoff(count_tie_below, (TOK_PAD, 1), need, idx_bits)

        cut = lax.cond(jnp.max(n_ge) > n_top, search_cut,
                       lambda: jnp.full((TOK_PAD, 1), key.shape[1], I32))
        tie = jnp.where(key == thr, jnp.where(idx <= cut, 0.0, -jnp.inf), -jnp.inf)
        sel = jnp.where(key > thr, 0.0, tie)
        bias_ref[0] = jnp.where(key > KEY_NEG_INF, sel, -jnp.inf)


def _idx_sample(page_table, ys, cache_t, layer, pps, n_top, n_tok):
    assert n_tok & (n_tok - 1) == 0 and TOK_PAD % n_tok == 0
    nseq, n_pages = page_table.shape
    page = cache_t.shape[3]
    n_cols = (n_pages + 1) * page
    cache = cache_t
    qiw = IDX_HEADS * IDX_DIM

    def page_spec(i):
        return pl.BlockSpec((None, None, IDX_DIM, page),
                            lambda b, j, pt, i=i: (layer, pt[b, j * pps + i], 0, 0))

    grid_spec = pltpu.PrefetchScalarGridSpec(
        num_scalar_prefetch=1,
        grid=(nseq, n_pages // pps),
        in_specs=[pl.BlockSpec((TOK_PAD, qiw), lambda b, j, pt: (b, COL_QI // qiw)),
                  pl.BlockSpec((TOK_PAD, LANES), lambda b, j, pt: (b, COL_KIW // LANES))]
                 + [page_spec(i) for i in range(pps)],
        out_specs=pl.BlockSpec((1, TOK_PAD, n_cols), lambda b, j, pt: (b, 0, 0)),
        scratch_shapes=[pltpu.VMEM((TOK_PAD, n_cols), F32)],
    )
    return pl.pallas_call(
        functools.partial(_idx_sample_kernel, pps=pps, page=page, n_pages=n_pages, n_top=n_top,
                          idx_bits=int(n_cols).bit_length(), n_tok=n_tok),
        out_shape=jax.ShapeDtypeStruct((nseq, TOK_PAD, n_cols), F32),
        grid_spec=grid_spec,
        compiler_params=_params(("parallel", "arbitrary")),
        name="idx_sample",
    )(page_table, ys, ys, *([cache] * pps))


def _dsa_sample_kernel(pt_ref, q0_ref, q1_ref, new_ref, bias_ref, *rest, pps, page, n_pages):
    page_refs = rest[:pps]
    o_ref = rest[pps]
    m_sc, l_sc, acc_sc = rest[pps + 1:]
    j = pl.program_id(1)
    q = _joint_queries([q0_ref[...], q1_ref[...]], HEAD_DIM ** -0.5)

    def visit(k, v, b8):
        bias = jnp.concatenate([b8] * (J_ROWS // TOK_PAD), axis=0)
        m, l, a = _softmax_block(q, k, v, bias, m_sc[...], l_sc[...], acc_sc[...])
        m_sc[...] = m
        l_sc[...] = l
        acc_sc[...] = a

    @pl.when(j == 0)
    def _():
        m_sc[...] = jnp.full(m_sc.shape, NEG_BIG, F32)
        l_sc[...] = jnp.zeros(l_sc.shape, F32)
        acc_sc[...] = jnp.zeros(acc_sc.shape, F32)
        visit(*_new_token_kv(new_ref, page), bias_ref[0, :, n_pages * page:])

    off = pl.multiple_of(j * (pps * page), pps * page)
    visit(*_gather_kv(page_refs, page), bias_ref[0, :, pl.ds(off, pps * page)])

    @pl.when(j == pl.num_programs(1) - 1)
    def _():
        _unstack_heads(o_ref, acc_sc[...] / l_sc[:, 0:1])


def _dsa_sample(page_table, ys, bias, cache, layer, pps):
    nseq, n_pages = page_table.shape
    rows = cache.shape[2]
    page = rows // KV_SLOTS
    n_cols = bias.shape[2]
    gw = GROUP * HEAD_DIM
    kvw = KV_SLOTS * HEAD_DIM

    def page_spec(i):
        return pl.BlockSpec((None, None, rows, HEAD_DIM),
                            lambda b, j, pt, i=i: (layer, pt[b, j * pps + i], 0, 0))

    grid_spec = pltpu.PrefetchScalarGridSpec(
        num_scalar_prefetch=1,
        grid=(nseq, n_pages // pps),
        in_specs=[pl.BlockSpec((TOK_PAD, gw), lambda b, j, pt: (b, COL_QB // gw)),
                  pl.BlockSpec((TOK_PAD, gw), lambda b, j, pt: (b, COL_QB // gw + 1)),
                  pl.BlockSpec((TOK_PAD, kvw), lambda b, j, pt: (b, COL_KVB // kvw)),
                  pl.BlockSpec((1, TOK_PAD, n_cols), lambda b, j, pt: (b, 0, 0))]
                 + [page_spec(i) for i in range(pps)],
        out_specs=pl.BlockSpec((TOK_PAD, N_HEADS * HEAD_DIM), lambda b, j, pt: (b, 0)),
        scratch_shapes=[pltpu.VMEM((J_ROWS, LANES), F32), pltpu.VMEM((J_ROWS, LANES), F32),
                        pltpu.VMEM((J_ROWS, J_LANES), F32)],
    )
    return pl.pallas_call(
        functools.partial(_dsa_sample_kernel, pps=pps, page=page, n_pages=n_pages),
        out_shape=jax.ShapeDtypeStruct((nseq * TOK_PAD, N_HEADS * HEAD_DIM), F32),
        grid_spec=grid_spec,
        compiler_params=_params(("parallel", "arbitrary")),
        name="dsa_sample",
    )(page_table, ys, ys, ys, bias, *([cache] * pps))


def _layer_norm(x, g, b):
    mu = jnp.mean(x, axis=-1, keepdims=True)
    xc = x - mu
    var = jnp.mean(xc * xc, axis=-1, keepdims=True)
    return xc * lax.rsqrt(var + LN_EPS) * g + b


def _merge_kernel(oa_ref, ob_ref, wa_ref, wb_ref, ga_ref, gb_ref, wo_ref, x_ref, g_ref, b_ref,
                  h_ref, hb_ref, acc_sc, *, alpha):
    kt = pl.program_id(1)

    @pl.when(kt == 0)
    def _():
        acc_sc[...] = jnp.zeros(acc_sc.shape, F32)

    oa = oa_ref[...].astype(BF16)
    ob = ob_ref[...].astype(BF16)
    mix = ga_ref[...] * _dot(oa, wa_ref[...]) + gb_ref[...] * _dot(ob, wb_ref[...])
    acc_sc[...] += _dot(mix.astype(BF16), wo_ref[...])

    @pl.when(kt == pl.num_programs(1) - 1)
    def _():
        h = _layer_norm(alpha * x_ref[...] + acc_sc[...], g_ref[...], b_ref[...])
        h_ref[...] = h
        hb_ref[...] = h.astype(BF16)


def _merge(oa, ob, y, x, wa, wb, wo, g, b, layer, alpha):
    rows = x.shape[0]
    tr = min(rows, 512)
    tk = 512
    kw = N_HEADS * HEAD_DIM
    return pl.pallas_call(
        functools.partial(_merge_kernel, alpha=alpha),
        out_shape=(jax.ShapeDtypeStruct((rows, D_MODEL), F32), jax.ShapeDtypeStruct((rows, D_MODEL), BF16)),
        grid=(rows // tr, D_MODEL // tk),
        in_specs=[pl.BlockSpec((tr, kw), lambda r, k: (r, 0)),
                  pl.BlockSpec((tr, kw), lambda r, k: (r, 0)),
                  pl.BlockSpec((None, kw, tk), lambda r, k: (layer, 0, k)),
                  pl.BlockSpec((None, kw, tk), lambda r, k: (layer, 0, k)),
                  pl.BlockSpec((tr, tk), lambda r, k: (r, COL_GA // tk + k)),
                  pl.BlockSpec((tr, tk), lambda r, k: (r, COL_GB // tk + k)),
                  pl.BlockSpec((None, tk, D_MODEL), lambda r, k: (layer, k, 0)),
                  pl.BlockSpec((tr, D_MODEL), lambda r, k: (r, 0)),
                  pl.BlockSpec((None, 1, D_MODEL), lambda r, k: (layer, 0, 0)),
                  pl.BlockSpec((None, 1, D_MODEL), lambda r, k: (layer, 0, 0))],
        out_specs=(pl.BlockSpec((tr, D_MODEL), lambda r, k: (r, 0)),
                   pl.BlockSpec((tr, D_MODEL), lambda r, k: (r, 0))),
        scratch_shapes=[pltpu.VMEM((tr, D_MODEL), F32)],
        compiler_params=_params(("parallel", "arbitrary")),
        name="merge_ln",
    )(oa, ob, wa, wb, y, y, wo, x, g, b)


def _gelu_tanh(x):
    return 0.5 * x * (1.0 + jnp.tanh(np.sqrt(2.0 / np.pi) * (x + 0.044715 * (x * x * x))))


HALO = 16


def _ffn_kernel(*refs, alpha, tr, seq_len, blocks_per_seq, prompt_mode):
    if prompt_mode:
        (hb_ref, halo_ref, w1a_ref, w1u_ref, cw_ref, cb_ref, w2_ref, h_ref, g_ref, b_ref,
         h2_ref, h2b_ref, a_ref, acc_sc) = refs
    else:
        (hb_ref, s1_ref, s2_ref, w1a_ref, w1u_ref, cw_ref, cb_ref, w2_ref, h_ref, g_ref, b_ref,
         h2_ref, h2b_ref, a_ref, acc_sc) = refs
    r = pl.program_id(0)
    ft = pl.program_id(1)

    @pl.when(ft == 0)
    def _():
        acc_sc[...] = jnp.zeros(acc_sc.shape, F32)

    hb = hb_ref[...]
    a = _dot(hb, w1a_ref[...])
    up = _dot(hb, w1u_ref[...])
    row = lax.broadcasted_iota(I32, a.shape, 0)
    p1 = pltpu.roll(a, 1, 0)
    p2 = pltpu.roll(a, 2, 0)
    if prompt_mode:
        a_halo = _dot(halo_ref[...], w1a_ref[...])
        keep = jnp.where(r % blocks_per_seq == 0, 0.0, 1.0)
        h6 = a_halo[HALO - 2:HALO - 1, :] * keep
        h7 = a_halo[HALO - 1:HALO, :] * keep
        p1 = jnp.where(row == 0, h7, p1)
        p2 = jnp.where(row == 0, h6, jnp.where(row == 1, h7, p2))
        a_ref[...] = a[tr - SUBLANES:, :]
    else:
        t = row & (seq_len - 1)
        p1 = jnp.where(t == 0, s1_ref[...], p1)
        p2 = jnp.where(t < 2, s2_ref[...], p2)
        a_ref[...] = a
    c = cb_ref[...] + cw_ref[0:1, :] * p2 + cw_ref[1:2, :] * p1 + cw_ref[2:3, :] * a
    hmid = (_gelu_tanh(c) * up).astype(BF16)
    acc_sc[...] += _dot(hmid, w2_ref[...])

    @pl.when(ft == pl.num_programs(1) - 1)
    def _():
        h2 = _layer_norm(alpha * h_ref[...] + acc_sc[...], g_ref[...], b_ref[...])
        h2_ref[...] = h2
        h2b_ref[...] = h2.astype(BF16)


def _ffn(h, hb, w1a, w1u, cw, cb, w2, g, b, layer, alpha, seq_len, state=None):
    rows = h.shape[0]
    prompt_mode = state is None
    tr = min(seq_len, 512) if prompt_mode else rows
    n_r = rows // tr
    n_f = D_FF_PAD // TF
    common_w = [pl.BlockSpec((None, D_MODEL, TF), lambda r, f: (layer, 0, f)),
                pl.BlockSpec((None, D_MODEL, TF), lambda r, f: (layer, 0, f)),
                pl.BlockSpec((None, SUBLANES, TF), lambda r, f: (layer, 0, f)),
                pl.BlockSpec((None, 1, TF), lambda r, f: (layer, 0, f)),
                pl.BlockSpec((None, TF, D_MODEL), lambda r, f: (layer, f, 0)),
                pl.BlockSpec((tr, D_MODEL), lambda r, f: (r, 0)),
                pl.BlockSpec((None, 1, D_MODEL), lambda r, f: (layer, 0, 0)),
                pl.BlockSpec((None, 1, D_MODEL), lambda r, f: (layer, 0, 0))]
    if prompt_mode:
        assert seq_len % tr == 0 and tr % HALO == 0
        per = tr // HALO
        extra_specs = [pl.BlockSpec((HALO, D_MODEL), lambda r, f: (jnp.maximum(r * per - 1, 0), 0))]
        extra = [hb]
        a_rows, a_blk = n_r * SUBLANES, SUBLANES
    else:
        assert seq_len & (seq_len - 1) == 0 and seq_len >= CONV_W - 1
        extra_specs = [pl.BlockSpec((tr, TF), lambda r, f: (0, f)),
                       pl.BlockSpec((tr, TF), lambda r, f: (0, f))]
        extra = list(state)
        a_rows, a_blk = rows, tr
    return pl.pallas_call(
        functools.partial(_ffn_kernel, alpha=alpha, tr=tr, seq_len=seq_len,
                          blocks_per_seq=max(seq_len // tr, 1), prompt_mode=prompt_mode),
        out_shape=(jax.ShapeDtypeStruct((rows, D_MODEL), F32), jax.ShapeDtypeStruct((rows, D_MODEL), BF16),
                   jax.ShapeDtypeStruct((a_rows, D_FF_PAD), F32)),
        grid=(n_r, n_f),
        in_specs=[pl.BlockSpec((tr, D_MODEL), lambda r, f: (r, 0))] + extra_specs + common_w,
        out_specs=(pl.BlockSpec((tr, D_MODEL), lambda r, f: (r, 0)),
                   pl.BlockSpec((tr, D_MODEL), lambda r, f: (r, 0)),
                   pl.BlockSpec((a_blk, TF), lambda r, f: (r, f))),
        scratch_shapes=[pltpu.VMEM((tr, D_MODEL), F32)],
        compiler_params=_params(("parallel", "arbitrary")),
        name="conv_ffn_ln",
    )(hb, *extra, w1a, w1u, cw, cb, w2, h, g, b)


def _ple_kernel(hb_ref, wg_ref, p_ref, wp_ref, h_ref, o_ref, ob_ref):
    gate = jax.nn.sigmoid(_dot(hb_ref[...], wg_ref[...]))
    out = h_ref[...] + gate * _dot(p_ref[...], wp_ref[...])
    o_ref[...] = out
    ob_ref[...] = out.astype(BF16)


def _ple(h2, h2b, pb, wg, wp, layer):
    rows = h2.shape[0]
    tr = min(rows, 1024)
    tn = 512
    return pl.pallas_call(
        _ple_kernel,
        out_shape=(jax.ShapeDtypeStruct((rows, D_MODEL), F32), jax.ShapeDtypeStruct((rows, D_MODEL), BF16)),
        grid=(rows // tr, D_MODEL // tn),
        in_specs=[pl.BlockSpec((tr, D_MODEL), lambda r, n: (r, 0)),
                  pl.BlockSpec((None, D_MODEL, tn), lambda r, n: (layer, 0, n)),
                  pl.BlockSpec((None, tr, PLE_DIM), lambda r, n: (layer, r, 0)),
                  pl.BlockSpec((None, PLE_DIM, tn), lambda r, n: (layer, 0, n)),
                  pl.BlockSpec((tr, tn), lambda r, n: (r, n))],
        out_specs=(pl.BlockSpec((tr, tn), lambda r, n: (r, n)),
                   pl.BlockSpec((tr, tn), lambda r, n: (r, n))),
        compiler_params=_params(("parallel", "arbitrary")),
        name="ple_gate",
    )(h2b, wg, pb, wp, h2)


def _pack_w_in(w_in):
    sizes = (1024, 256, 256, 1024, 256, 256, 1024, 64, 16, 2048, 2048)
    offs = np.concatenate([[0], np.cumsum(sizes)])
    qa, ka, va, qb, kb, vb, qi, ki, wi, ga, gb = [w_in[..., offs[i]:offs[i + 1]] for i in range(11)]
    pad = jnp.zeros(w_in.shape[:-1] + (TN - IDX_DIM - IDX_HEADS,), w_in.dtype)
    return jnp.concatenate([qa, ka, va, qb, kb, vb, qi, ki, wi, pad, ga, gb], axis=-1).astype(BF16)


def _rope_tables(pos):
    pos = pos.astype(F32)[:, None]

    def table(head_dim):
        rot = head_dim // 4
        half = rot // 2
        inv = ROPE_THETA ** (-(2.0 * jnp.arange(half, dtype=F32)) / rot)
        ang = pos * inv[None, :]
        cos, sin = jnp.cos(ang), jnp.sin(ang)
        ones = jnp.ones((pos.shape[0], head_dim - rot), F32)
        c = jnp.concatenate([cos, cos, ones], axis=1)
        s = jnp.concatenate([-sin, sin, 0.0 * ones], axis=1)
        reps = LANES // head_dim
        return jnp.tile(c, (1, reps)), jnp.tile(s, (1, reps))

    c128, s128 = table(HEAD_DIM)
    c64, s64 = table(IDX_DIM)
    return c128, s128, c64, s64


def kernel(x_prompt, x_sample, cache_sb_kv, cache_dsa_kv, cache_idx_k, state_ffn_conv, page_table,
           p_prompt, p_sample, w_in, w_branch_sb, w_branch_dsa, w_out, ln1_g, ln1_b, w_ffn_in,
           ffn_conv_w, ffn_conv_b, w_ffn_out, ln2_g, ln2_b, w_ple_gate, w_ple_proj):
    batch, seq = x_prompt.shape[:2]
    nseq, n_tok = x_sample.shape[:2]
    depth = w_in.shape[0]
    n_pool, page = cache_sb_kv.shape[1:3]
    n_pages = page_table.shape[1]
    past_len = n_pages * page
    alpha = (2 * depth) ** 0.25
    kv_w = 2 * N_KV * HEAD_DIM
    top_s = max(1, min(TOPK_MAX, (past_len + n_tok) // 4))
    pps = min(32, n_pages)
    pps_idx = min(64, n_pages)

    w_in_p = _pack_w_in(w_in)
    wa = w_branch_sb.astype(BF16)
    wb = w_branch_dsa.astype(BF16)
    wo = w_out.astype(BF16)
    ff_pad = D_FF_PAD - D_FF
    w1a = jnp.pad(w_ffn_in[..., :D_FF], ((0, 0), (0, 0), (0, ff_pad))).astype(BF16)
    w1u = jnp.pad(w_ffn_in[..., D_FF:], ((0, 0), (0, 0), (0, ff_pad))).astype(BF16)
    w2 = jnp.pad(w_ffn_out, ((0, 0), (0, ff_pad), (0, 0))).astype(BF16)
    cw = jnp.pad(ffn_conv_w, ((0, 0), (0, SUBLANES - CONV_W), (0, ff_pad)))
    cb = jnp.pad(ffn_conv_b, ((0, 0), (0, ff_pad)))[:, None, :]
    wg = w_ple_gate.astype(BF16)
    wp = w_ple_proj.astype(BF16)
    g1, b1 = ln1_g[:, None, :], ln1_b[:, None, :]
    g2, b2 = ln2_g[:, None, :], ln2_b[:, None, :]

    tabs_p = _rope_tables(jnp.tile(jnp.arange(seq, dtype=jnp.int32), batch))
    tabs_s = _rope_tables(jnp.tile(past_len + jnp.arange(TOK_PAD, dtype=jnp.int32), nseq))

    sb_pages = cache_sb_kv.reshape(depth, n_pool, page * KV_SLOTS, HEAD_DIM)
    dsa_pages = cache_dsa_kv.reshape(depth, n_pool, page * KV_SLOTS, HEAD_DIM)
    idx_pages_t = jnp.swapaxes(cache_idx_k, 2, 3)

    assert n_tok <= TOK_PAD
    tok_pad = ((0, 0), (0, TOK_PAD - n_tok), (0, 0))
    xp = x_prompt.reshape(batch * seq, D_MODEL)
    xs = jnp.pad(x_sample, tok_pad).reshape(nseq * TOK_PAD, D_MODEL)
    xpb, xsb = xp.astype(BF16), xs.astype(BF16)
    ppb = p_prompt.reshape(depth, batch * seq, PLE_DIM).astype(BF16)
    psb = jnp.pad(p_sample, ((0, 0),) + tok_pad).reshape(depth, nseq * TOK_PAD, PLE_DIM).astype(BF16)
    st = jnp.pad(state_ffn_conv, ((0, 0), (0, 0), (0, 0), (0, ff_pad)))
    conv_s1 = jnp.repeat(st[:, :, 1], TOK_PAD, axis=1)
    conv_s2 = jnp.pad(st, ((0, 0), (0, 0), (0, TOK_PAD - (CONV_W - 1)), (0, 0))).reshape(
        depth, nseq * TOK_PAD, D_FF_PAD)

    outs = {k: [] for k in ("sb_p", "dsa_p", "idx_p", "conv_p", "sb_s", "dsa_s", "idx_s", "conv_s")}
    for l in range(depth):
        y = _project(xpb, w_in_p, l, tabs_p)
        oa = _sb_prompt(y, batch, seq)
        ob = _dsa_prompt(y, batch, seq)
        h, hb = _merge(oa, ob, y, xp, wa, wb, wo, g1, b1, l, alpha)
        h2, h2b, a_tail = _ffn(h, hb, w1a, w1u, cw, cb, w2, g2, b2, l, alpha, seq)
        xp, xpb = _ple(h2, h2b, ppb, wg, wp, l)
        outs["sb_p"].append(y[:, COL_KVA:COL_KVA + kv_w].reshape(batch, seq, 2, N_KV, HEAD_DIM))
        outs["dsa_p"].append(y[:, COL_KVB:COL_KVB + kv_w].reshape(batch, seq, 2, N_KV, HEAD_DIM))
        outs["idx_p"].append(y[:, COL_KIW:COL_KIW + IDX_DIM].reshape(batch, seq, IDX_DIM))
        tails = a_tail.reshape(batch, -1, SUBLANES, D_FF_PAD)[:, -1, SUBLANES - (CONV_W - 1):, :D_FF]
        outs["conv_p"].append(tails)

        ys = _project(xsb, w_in_p, l, tabs_s)
        oa_s = _sb_sample(page_table, ys, sb_pages, l, pps)
        bias = _idx_sample(page_table, ys, idx_pages_t, l, pps_idx, top_s, n_tok)
        ob_s = _dsa_sample(page_table, ys, bias, dsa_pages, l, pps)
        hs, hsb = _merge(oa_s, ob_s, ys, xs, wa, wb, wo, g1, b1, l, alpha)
        h2s, h2sb, a_s = _ffn(hs, hsb, w1a, w1u, cw, cb, w2, g2, b2, l, alpha,
                              TOK_PAD, state=(conv_s1[l], conv_s2[l]))
        xs, xsb = _ple(h2s, h2sb, psb, wg, wp, l)
        ys_tok = ys.reshape(nseq, TOK_PAD, IN_COLS)[:, :n_tok]
        outs["sb_s"].append(ys_tok[..., COL_KVA:COL_KVA + kv_w].reshape(nseq, n_tok, 2, N_KV, HEAD_DIM))
        outs["dsa_s"].append(ys_tok[..., COL_KVB:COL_KVB + kv_w].reshape(nseq, n_tok, 2, N_KV, HEAD_DIM))
        outs["idx_s"].append(ys_tok[..., COL_KIW:COL_KIW + IDX_DIM])
        outs["conv_s"].append(a_s.reshape(nseq, TOK_PAD, D_FF_PAD)[:, n_tok - (CONV_W - 1):n_tok, :D_FF])

    return (xp.reshape(batch, seq, D_MODEL), xs.reshape(nseq, TOK_PAD, D_MODEL)[:, :n_tok],
            jnp.stack(outs["sb_p"]), jnp.stack(outs["dsa_p"]), jnp.stack(outs["idx_p"]),
            jnp.stack(outs["conv_p"]), jnp.stack(outs["sb_s"]), jnp.stack(outs["dsa_s"]),
            jnp.stack(outs["idx_s"]), jnp.stack(outs["conv_s"]))
```

```python
import functools

import jax
import jax.numpy as jnp
import numpy as np
from jax import lax
from jax.experimental import pallas as pl
from jax.experimental.pallas import tpu as pltpu

F32 = jnp.float32
BF16 = jnp.bfloat16
I32 = jnp.int32

D_MODEL = 2048
HEAD_DIM = 128
N_HEADS = 8
N_KV = 2
GROUP = N_HEADS // N_KV
IDX_HEADS = 16
IDX_DIM = 64
TOPK_MAX = 256
ROPE_THETA = 500000.0
D_FF = 5504
CONV_W = 3
PLE_DIM = 256
LN_EPS = 1e-5

LANES = 128
SUBLANES = 8
VMEM_LIMIT = 56 * 1024 * 1024

TN = 512
COL_QA = 0
COL_KVA = 1024
COL_QB = 1536
COL_KVB = 2560
COL_QI = 3072
COL_KIW = 4096
COL_GA = 4608
COL_GB = 6656
IN_COLS = 8704
N_IN_TILES = IN_COLS // TN
D_FF_PAD = 5632
TF = 512

SB_Q_SCALE = (HEAD_DIM ** -0.5) * float(np.log2(np.e))
SB_STACK_ROWS = 64
NEG_BIG = -1e30
KEY_NEG_INF = np.int32(np.array(0xFF800000, dtype=np.uint32).view(np.int32) ^ 0x7FFFFFFF)
INT_MIN = np.int32(-2 ** 31)


def _params(sem):
    return pltpu.CompilerParams(dimension_semantics=sem, vmem_limit_bytes=VMEM_LIMIT)


def _dot_t(a, b):
    return lax.dot_general(a, b, (((1,), (1,)), ((), ())), preferred_element_type=F32)


def _dot(a, b):
    return jnp.dot(a, b, preferred_element_type=F32)


def _rope(y, c, s, half):
    w = y.shape[1]
    reps = w // LANES
    if reps > 1:
        c = jnp.concatenate([c] * reps, axis=1)
        s = jnp.concatenate([s] * reps, axis=1)
    lane = lax.broadcasted_iota(I32, y.shape, 1)
    first = (lane & (2 * half - 1)) < half
    partner = jnp.where(first, pltpu.roll(y, w - half, 1), pltpu.roll(y, half, 1))
    return y * c + partner * s


def _proj_kernel(x_ref, w_ref, c128_ref, s128_ref, c64_ref, s64_ref, y_ref):
    n = pl.program_id(1)

    def product():
        return _dot(x_ref[...], w_ref[...])

    @pl.when(n < COL_QB // TN)
    def _():
        y_ref[...] = product()

    @pl.when(jnp.logical_and(n >= COL_QB // TN, n < COL_KVB // TN))
    def _():
        y_ref[...] = _rope(product(), c128_ref[...], s128_ref[...], 16)

    @pl.when(n == COL_KVB // TN)
    def _():
        y = product()
        y_ref[:, :256] = _rope(y[:, :256], c128_ref[...], s128_ref[...], 16)
        y_ref[:, 256:] = y[:, 256:]

    @pl.when(jnp.logical_and(n >= COL_QI // TN, n < COL_KIW // TN))
    def _():
        y_ref[...] = _rope(product(), c64_ref[...], s64_ref[...], 8)

    @pl.when(n == COL_KIW // TN)
    def _():
        y = product()
        lane = lax.broadcasted_iota(I32, c64_ref.shape, 1)
        c = jnp.where(lane < IDX_DIM, c64_ref[...], 1.0)
        s = jnp.where(lane < IDX_DIM, s64_ref[...], 0.0)
        y_ref[:, :LANES] = _rope(y[:, :LANES], c, s, 8)
        y_ref[:, LANES:] = y[:, LANES:]

    @pl.when(n >= COL_GA // TN)
    def _():
        y_ref[...] = jax.nn.sigmoid(product())


def _project(xb, w, layer, tabs):
    rows = xb.shape[0]
    tr = min(rows, 1024)
    tab_spec = pl.BlockSpec((tr, LANES), lambda r, n: (r, 0))
    return pl.pallas_call(
        _proj_kernel,
        out_shape=jax.ShapeDtypeStruct((rows, IN_COLS), F32),
        grid=(rows // tr, N_IN_TILES),
        in_specs=[pl.BlockSpec((tr, D_MODEL), lambda r, n: (r, 0)),
                  pl.BlockSpec((None, D_MODEL, TN), lambda r, n: (layer, 0, n)),
                  tab_spec, tab_spec, tab_spec, tab_spec],
        out_specs=pl.BlockSpec((tr, TN), lambda r, n: (r, n)),
        compiler_params=_params(("parallel", "arbitrary")),
        name="in_proj",
    )(xb, w, *tabs)


def _suffix_matrix():
    j = lax.broadcasted_iota(I32, (2 * LANES, 2 * LANES), 0) & (LANES - 1)
    s = lax.broadcasted_iota(I32, (2 * LANES, 2 * LANES), 1)
    return jnp.where(jnp.logical_or(j > s, s >= LANES), -1.0, 0.0).astype(BF16)


def _sb_scores(q, k, mask):
    n_sub = k.shape[0] // LANES
    z = _dot_t(q, k)
    neg_abs = pltpu.bitcast(pltpu.bitcast(z, I32) | INT_MIN, F32)
    sp = jnp.maximum(z, 0.0) + jnp.log2(1.0 + jnp.exp2(neg_abs))
    spm = sp if mask is None else jnp.where(mask, sp, 0.0)
    hi = spm.astype(BF16)
    lo = (spm - hi.astype(F32)).astype(BF16)
    subs = [jnp.concatenate([hi[:, i * LANES:(i + 1) * LANES], lo[:, i * LANES:(i + 1) * LANES]], axis=1)
            for i in range(n_sub)]
    return z - sp, subs


def _sb_weights(d, subs, v, u, carry, mask):
    n_sub = len(subs)
    m_rows = d.shape[0]
    if m_rows <= SB_STACK_ROWS:
        r_all = _dot(subs[0] if n_sub == 1 else jnp.concatenate(subs, axis=0), u)
        rs = [r_all[i * m_rows:(i + 1) * m_rows] for i in range(n_sub)]
    else:
        rs = [_dot(s, u) for s in subs]
    afters = [None] * n_sub
    for i in reversed(range(n_sub)):
        afters[i] = rs[i][:, :LANES] + carry
        carry = carry + rs[i][:, LANES:]
    after = afters[0] if n_sub == 1 else jnp.concatenate(afters, axis=1)
    w = jnp.exp2(d + after)
    if mask is not None:
        w = jnp.where(mask, w, 0.0)
    return _dot(w.astype(BF16), v), carry


def _sb_block(q, k, v, u, carry, mask):
    d, subs = _sb_scores(q, k, mask)
    return _sb_weights(d, subs, v, u, carry, mask)


def _softmax_block(q, k, v, bias, m_old, l_old, acc_old):
    logit = _dot_t(q, k) + bias
    m_new = jnp.maximum(m_old, jnp.max(logit, axis=1, keepdims=True))
    p = jnp.exp(logit - m_new[:, 0:1])
    alpha = jnp.exp(m_old - m_new)
    l_new = alpha * l_old + jnp.sum(p, axis=1, keepdims=True)
    acc_new = alpha[:, 0:1] * acc_old + _dot(p.astype(BF16), v)
    return m_new, l_new, acc_new


def _sortable(x):
    b = pltpu.bitcast(x, I32)
    return jnp.where(b < 0, b ^ jnp.int32(0x7FFFFFFF), b)


def _kth_largest(count_ge, shape, k, n_total):
    bits_per_check = 4

    def cond(state):
        i, _, cnt = state
        return jnp.logical_and(i < 32, jnp.max(cnt) > k)

    def body(state):
        i, t, cnt = state
        for _ in range(bits_per_check):
            cand = t + jnp.left_shift(jnp.int32(1), jnp.int32(31) - i)
            c = count_ge(cand)
            ok = c >= k
            i, t, cnt = i + 1, jnp.where(ok, cand, t), jnp.where(ok, c, cnt)
        return i, t, cnt

    state = (jnp.int32(0), jnp.full(shape, INT_MIN, I32), jnp.full(shape, n_total, I32))
    _, t, cnt = lax.while_loop(cond, body, state)
    return t, cnt


def _tie_cutoff(count_tie_below, shape, need, n_bits):
    def body(i, j):
        cand = j + jnp.left_shift(jnp.int32(1), jnp.int32(n_bits - 1) - i)
        return jnp.where(count_tie_below(cand) < need, cand, j)

    return lax.fori_loop(0, n_bits, body, jnp.zeros(shape, I32))


def _sb_prompt_kernel(q_ref, k_ref, v_ref, o_ref, kb_sc, vb_sc, carry_sc, acc_sc, *, tq):
    iq = pl.program_id(1)

    @pl.when(iq == 0)
    def _():
        kb_sc[...] = k_ref[...].astype(BF16)
        vb_sc[...] = v_ref[...].astype(BF16)

    scale = SB_Q_SCALE
    u = _suffix_matrix()
    n_chain = N_HEADS // 2
    c_rows = 2 * tq
    qs = []
    for c in range(n_chain):
        q = jnp.concatenate([q_ref[:, h * HEAD_DIM:(h + 1) * HEAD_DIM] for h in (2 * c, 2 * c + 1)], axis=0)
        qs.append((q * scale).astype(BF16))

    def kv_block(o2, c):
        g = (2 * c) // GROUP
        return (kb_sc[pl.ds(o2, tq), g * HEAD_DIM:(g + 1) * HEAD_DIM],
                vb_sc[pl.ds(o2, tq), g * HEAD_DIM:(g + 1) * HEAD_DIM])

    row = lax.broadcasted_iota(I32, (c_rows, tq), 0) & (tq - 1)
    col = lax.broadcasted_iota(I32, (c_rows, tq), 1)
    off = pl.multiple_of(iq * tq, tq)
    for c in range(n_chain):
        k, v = kv_block(off, c)
        contrib, carry = _sb_block(qs[c], k, v, u, jnp.zeros((c_rows, LANES), F32), col < row)
        acc_sc[c] = contrib
        carry_sc[c] = carry

    def body(i, _):
        o2 = pl.multiple_of((iq - 1 - i) * tq, tq)
        for c in range(n_chain):
            k, v = kv_block(o2, c)
            contrib, carry = _sb_block(qs[c], k, v, u, carry_sc[c], None)
            acc_sc[c] += contrib
            carry_sc[c] = carry
        return 0

    lax.fori_loop(0, iq, body, 0)
    for h in range(N_HEADS):
        r0 = (h % 2) * tq
        o_ref[:, h * HEAD_DIM:(h + 1) * HEAD_DIM] = acc_sc[h // 2, r0:r0 + tq, :].astype(o_ref.dtype)


def _sb_prompt(y, batch, seq):
    tq = min(2 * LANES, seq)
    nq = seq // tq
    qw = N_HEADS * HEAD_DIM
    kvw = N_KV * HEAD_DIM
    return pl.pallas_call(
        functools.partial(_sb_prompt_kernel, tq=tq),
        out_shape=jax.ShapeDtypeStruct((batch * seq, qw), BF16),
        grid=(batch, nq),
        in_specs=[pl.BlockSpec((tq, qw), lambda b, i: (b * nq + i, COL_QA // qw)),
                  pl.BlockSpec((seq, kvw), lambda b, i: (b, COL_KVA // kvw)),
                  pl.BlockSpec((seq, kvw), lambda b, i: (b, COL_KVA // kvw + 1))],
        out_specs=pl.BlockSpec((tq, qw), lambda b, i: (b * nq + i, 0)),
        scratch_shapes=[pltpu.VMEM((seq, kvw), BF16), pltpu.VMEM((seq, kvw), BF16),
                        pltpu.VMEM((N_HEADS // 2, 2 * tq, LANES), F32),
                        pltpu.VMEM((N_HEADS // 2, 2 * tq, LANES), F32)],
        compiler_params=_params(("parallel", "arbitrary")),
        name="sb_prompt",
    )(y, y, y)


TKI = 256


def _fold_rows(x, op):
    return op(x.reshape(x.shape[0] // SUBLANES, SUBLANES, x.shape[1]), axis=0)


def _dsa_prompt_kernel(qb0_ref, qb1_ref, k_ref, v_ref, qi_ref, kiw_all_ref, kiw_q_ref, o_ref,
                       kb_sc, vb_sc, ki2_sc, key_sc, bias_sc, cut_sc, m_sc, l_sc, acc_sc,
                       *, tq, n_top, idx_bits):
    iq = pl.program_id(1)
    seq = k_ref.shape[0]

    @pl.when(iq == 0)
    def _():
        kb_sc[...] = k_ref[...].astype(BF16)
        vb_sc[...] = v_ref[...].astype(BF16)
        lane = lax.broadcasted_iota(I32, (seq, LANES), 1)
        kia = jnp.where(lane < IDX_DIM, kiw_all_ref[...], 0.0)
        ki2_sc[0] = kia.astype(BF16)
        ki2_sc[1] = pltpu.roll(kia, IDX_DIM, 1).astype(BF16)

    w_t = (kiw_q_ref[...] * ((IDX_DIM ** -0.5) * (IDX_HEADS ** -0.5))).T
    w_rows = [w_t[IDX_DIM + h:IDX_DIM + h + 1, :] for h in range(IDX_HEADS)]

    q_pairs = [qi_ref[:, p * LANES:(p + 1) * LANES].astype(BF16) for p in range(IDX_HEADS // 2)]
    q_pos = iq * tq + lax.broadcasted_iota(I32, (1, tq), 1)
    n_blk = (iq * tq + tq + TKI - 1) // TKI

    def idx_body(j, _):
        off = pl.multiple_of(j * TKI, TKI)
        ki_e = ki2_sc[0, pl.ds(off, TKI), :]
        ki_o = ki2_sc[1, pl.ds(off, TKI), :]
        acc = jnp.zeros((TKI, tq), F32)
        for p in range(IDX_HEADS // 2):
            acc = acc + jnp.maximum(_dot_t(ki_e, q_pairs[p]), 0.0) * w_rows[2 * p]
            acc = acc + jnp.maximum(_dot_t(ki_o, q_pairs[p]), 0.0) * w_rows[2 * p + 1]
        k_pos = off + lax.broadcasted_iota(I32, (TKI, 1), 0)
        key_sc[pl.ds(off, TKI), :] = jnp.where(k_pos <= q_pos, _sortable(acc), KEY_NEG_INF)
        return 0

    lax.fori_loop(0, n_blk, idx_body, 0)

    def query_counts(hit_fn):
        def body(j, c):
            off = pl.multiple_of(j * TKI, TKI)
            return c + _fold_rows(hit_fn(key_sc[pl.ds(off, TKI), :], off), jnp.sum)

        c = lax.fori_loop(0, n_blk, body, jnp.zeros((SUBLANES, tq), I32))
        return jnp.sum(c, axis=0, keepdims=True)

    def count_ge(t):
        return query_counts(lambda kb, off: jnp.where(kb >= t, 1, 0))

    thr, n_ge = _kth_largest(count_ge, (1, tq), n_top, n_blk * TKI)

    cut_sc[...] = jnp.full(cut_sc.shape, seq, I32)

    @pl.when(jnp.max(n_ge) > n_top)
    def _():
        need = n_top - count_ge(thr + 1)

        def count_tie_below(jc):
            def hit(kb, off):
                idx = off + lax.broadcasted_iota(I32, kb.shape, 0)
                return jnp.where(kb == thr, jnp.where(idx < jc, 1, 0), 0)
            return query_counts(hit)

        cut = _tie_cutoff(count_tie_below, (1, tq), need, idx_bits)
        cut_sc[...] = jnp.broadcast_to(cut, cut_sc.shape)

    cut = cut_sc[0:1, :]

    def bias_body(j, _):
        off = pl.multiple_of(j * TKI, TKI)
        kb = key_sc[pl.ds(off, TKI), :]
        idx = off + lax.broadcasted_iota(I32, (TKI, tq), 0)
        tie = jnp.where(kb == thr, jnp.where(idx <= cut, 0.0, -jnp.inf), -jnp.inf)
        sel = jnp.where(kb > thr, 0.0, tie)
        sel = jnp.where(kb > KEY_NEG_INF, sel, -jnp.inf)
        bias_sc[:, pl.ds(off, TKI)] = sel.T
        return 0

    lax.fori_loop(0, n_blk, bias_body, 0)

    scale = (HEAD_DIM ** -0.5) * np.log2(np.e)
    n_chain = N_HEADS // 2
    n_sub = TKI // LANES
    qs = []
    for c in range(n_chain):
        q_ref = (qb0_ref, qb1_ref)[c // 2]
        hs = (2 * (c % 2), 2 * (c % 2) + 1)
        q = jnp.concatenate([q_ref[:, h * HEAD_DIM:(h + 1) * HEAD_DIM] for h in hs], axis=0)
        qs.append((q * scale).astype(BF16))

    def bias_rows(off):
        b = bias_sc[:, pl.ds(off, TKI)]
        return jnp.concatenate([b, b], axis=0)

    m_sc[...] = jnp.full(m_sc.shape, -jnp.inf, F32)

    def max_body(j, _):
        off = pl.multiple_of(j * TKI, TKI)
        bias = bias_rows(off)
        for c in range(n_chain):
            g = c // 2
            lg = _dot_t(qs[c], kb_sc[pl.ds(off, TKI), g * HEAD_DIM:(g + 1) * HEAD_DIM]) + bias
            m = m_sc[c]
            for i in range(n_sub):
                m = jnp.maximum(m, lg[:, i * LANES:(i + 1) * LANES])
            m_sc[c] = m
        return 0

    lax.fori_loop(0, n_blk, max_body, 0)

    for c in range(n_chain):
        m_sc[c] = jnp.broadcast_to(jnp.max(m_sc[c], axis=1, keepdims=True), (2 * tq, LANES))
    l_sc[...] = jnp.zeros(l_sc.shape, F32)
    acc_sc[...] = jnp.zeros(acc_sc.shape, F32)

    def sum_body(j, _):
        off = pl.multiple_of(j * TKI, TKI)
        bias = bias_rows(off)
        for c in range(n_chain):
            g = c // 2
            lg = _dot_t(qs[c], kb_sc[pl.ds(off, TKI), g * HEAD_DIM:(g + 1) * HEAD_DIM]) + bias
            p = jnp.exp2(lg - jnp.concatenate([m_sc[c]] * n_sub, axis=1))
            l = l_sc[c]
            for i in range(n_sub):
                l = l + p[:, i * LANES:(i + 1) * LANES]
            l_sc[c] = l
            acc_sc[c] += _dot(p.astype(BF16), vb_sc[pl.ds(off, TKI), g * HEAD_DIM:(g + 1) * HEAD_DIM])
        return 0

    lax.fori_loop(0, n_blk, sum_body, 0)
    for c in range(n_chain):
        out = acc_sc[c] / jnp.sum(l_sc[c], axis=1, keepdims=True)
        for hh in range(2):
            c0 = (2 * c + hh) * HEAD_DIM
            o_ref[:, c0:c0 + HEAD_DIM] = out[hh * tq:(hh + 1) * tq, :].astype(o_ref.dtype)


def _dsa_prompt(y, batch, seq):
    tq = min(2 * LANES, seq)
    assert tq == TKI or seq == tq
    nq = seq // tq
    gw = GROUP * HEAD_DIM
    kvw = N_KV * HEAD_DIM
    n_top = max(1, min(TOPK_MAX, seq // 4))
    idx_bits = int(seq).bit_length()
    q_rows = 2 * tq
    return pl.pallas_call(
        functools.partial(_dsa_prompt_kernel, tq=tq, n_top=n_top, idx_bits=idx_bits),
        out_shape=jax.ShapeDtypeStruct((batch * seq, N_HEADS * HEAD_DIM), BF16),
        grid=(batch, nq),
        in_specs=[pl.BlockSpec((tq, gw), lambda b, i: (b * nq + i, COL_QB // gw)),
                  pl.BlockSpec((tq, gw), lambda b, i: (b * nq + i, COL_QB // gw + 1)),
                  pl.BlockSpec((seq, kvw), lambda b, i: (b, COL_KVB // kvw)),
                  pl.BlockSpec((seq, kvw), lambda b, i: (b, COL_KVB // kvw + 1)),
                  pl.BlockSpec((tq, IDX_HEADS * IDX_DIM), lambda b, i: (b * nq + i, COL_QI // (IDX_HEADS * IDX_DIM))),
                  pl.BlockSpec((seq, LANES), lambda b, i: (b, COL_KIW // LANES)),
                  pl.BlockSpec((tq, LANES), lambda b, i: (b * nq + i, COL_KIW // LANES))],
        out_specs=pl.BlockSpec((tq, N_HEADS * HEAD_DIM), lambda b, i: (b * nq + i, 0)),
        scratch_shapes=[pltpu.VMEM((seq, kvw), BF16),
                        pltpu.VMEM((seq, kvw), BF16),
                        pltpu.VMEM((2, seq, LANES), BF16),
                        pltpu.VMEM((seq, tq), I32),
                        pltpu.VMEM((tq, seq), F32),
                        pltpu.VMEM((SUBLANES, tq), I32),
                        pltpu.VMEM((N_HEADS // 2, q_rows, LANES), F32),
                        pltpu.VMEM((N_HEADS // 2, q_rows, LANES), F32),
                        pltpu.VMEM((N_HEADS // 2, q_rows, HEAD_DIM), F32)],
        compiler_params=_params(("parallel", "arbitrary")),
        name="dsa_prompt",
    )(y, y, y, y, y, y, y)


TOK_PAD = SUBLANES
S_ROWS = GROUP * TOK_PAD


KV_SLOTS = 2 * N_KV


def _page_rows(ref, slot, page):
    return ref[pl.ds(slot, page, stride=KV_SLOTS), :].astype(BF16)


J_ROWS = N_KV * S_ROWS
J_LANES = N_KV * HEAD_DIM


def _gather_kv(page_refs, page):
    k = [jnp.concatenate([_page_rows(r, g, page) for g in range(N_KV)], axis=1) for r in page_refs]
    v = [jnp.concatenate([_page_rows(r, N_KV + g, page) for g in range(N_KV)], axis=1) for r in page_refs]
    if len(page_refs) == 1:
        return k[0], v[0]
    return jnp.concatenate(k, axis=0), jnp.concatenate(v, axis=0)


def _joint_queries(q_blks, scale):
    rows = []
    for g, q_blk in enumerate(q_blks):
        q = jnp.concatenate([q_blk[:, h * HEAD_DIM:(h + 1) * HEAD_DIM] for h in range(GROUP)], axis=0) * scale
        zero = jnp.zeros_like(q)
        rows.append(jnp.concatenate([q if gg == g else zero for gg in range(N_KV)], axis=1))
    return jnp.concatenate(rows, axis=0).astype(BF16)


def _new_token_kv(new_ref, page):
    pad = jnp.zeros((page - TOK_PAD, J_LANES), F32)
    k = jnp.concatenate([new_ref[:, :J_LANES], pad], axis=0)
    v = jnp.concatenate([new_ref[:, J_LANES:], pad], axis=0)
    return k.astype(BF16), v.astype(BF16)


def _unstack_heads(o_ref, acc):
    for g in range(N_KV):
        for h in range(GROUP):
            r0 = g * S_ROWS + h * TOK_PAD
            c0 = (g * GROUP + h) * HEAD_DIM
            o_ref[:, c0:c0 + HEAD_DIM] = acc[r0:r0 + TOK_PAD, g * HEAD_DIM:(g + 1) * HEAD_DIM]


def _sb_sample_kernel(pt_ref, q_ref, new_ref, *rest, pps, page):
    page_refs = rest[:pps]
    o_ref = rest[pps]
    carry_sc, acc_sc = rest[pps + 1:]
    j = pl.program_id(1)
    u = _suffix_matrix()
    gw = GROUP * HEAD_DIM
    q = _joint_queries([q_ref[:, g * gw:(g + 1) * gw] for g in range(N_KV)], SB_Q_SCALE)

    def visit(k, v, mask):
        c, cr = _sb_block(q, k, v, u, carry_sc[...], mask)
        acc_sc[...] += c
        carry_sc[...] = cr

    @pl.when(j == 0)
    def _():
        carry_sc[...] = jnp.zeros(carry_sc.shape, F32)
        acc_sc[...] = jnp.zeros(acc_sc.shape, F32)
        tok = lax.broadcasted_iota(I32, (J_ROWS, page), 0) & (TOK_PAD - 1)
        col = lax.broadcasted_iota(I32, (J_ROWS, page), 1)
        visit(*_new_token_kv(new_ref, page), col < tok)

    visit(*_gather_kv(page_refs, page), None)

    @pl.when(j == pl.num_programs(1) - 1)
    def _():
        _unstack_heads(o_ref, acc_sc[...])


def _sb_sample(page_table, ys, cache, layer, pps):
    nseq, n_pages = page_table.shape
    rows = cache.shape[2]
    page = rows // KV_SLOTS
    n_steps = n_pages // pps
    qw = N_HEADS * HEAD_DIM
    kvw = KV_SLOTS * HEAD_DIM

    def page_spec(i):
        return pl.BlockSpec((None, None, rows, HEAD_DIM),
                            lambda b, j, pt, i=i: (layer, pt[b, (n_steps - 1 - j) * pps + i], 0, 0))

    grid_spec = pltpu.PrefetchScalarGridSpec(
        num_scalar_prefetch=1,
        grid=(nseq, n_steps),
        in_specs=[pl.BlockSpec((TOK_PAD, qw), lambda b, j, pt: (b, COL_QA // qw)),
                  pl.BlockSpec((TOK_PAD, kvw), lambda b, j, pt: (b, COL_KVA // kvw))]
                 + [page_spec(i) for i in range(pps)],
        out_specs=pl.BlockSpec((TOK_PAD, qw), lambda b, j, pt: (b, 0)),
        scratch_shapes=[pltpu.VMEM((J_ROWS, LANES), F32), pltpu.VMEM((J_ROWS, J_LANES), F32)],
    )
    return pl.pallas_call(
        functools.partial(_sb_sample_kernel, pps=pps, page=page),
        out_shape=jax.ShapeDtypeStruct((nseq * TOK_PAD, qw), F32),
        grid_spec=grid_spec,
        compiler_params=_params(("parallel", "arbitrary")),
        name="sb_sample",
    )(page_table, ys, ys, *([cache] * pps))


def _idx_sample_kernel(pt_ref, qi_ref, kiw_ref, *rest, pps, page, n_pages, n_top, idx_bits, n_tok):
    page_refs = rest[:pps]
    bias_ref = rest[pps]
    score_sc = rest[pps + 1]
    j = pl.program_id(1)

    row = lax.broadcasted_iota(I32, (TOK_PAD, 1), 0)

    def real_rows(x):
        return jnp.where(row < n_tok, x, pltpu.roll(x, n_tok, 0))

    qi_rows = real_rows(qi_ref[...])
    kiw = real_rows(kiw_ref[...])
    qi = jnp.concatenate([qi_rows[:, h * IDX_DIM:(h + 1) * IDX_DIM] for h in range(IDX_HEADS)],
                         axis=0).astype(BF16)
    w_scale = (IDX_DIM ** -0.5) * (IDX_HEADS ** -0.5)
    wm = jnp.concatenate([jnp.broadcast_to(kiw[:, IDX_DIM + h:IDX_DIM + h + 1] * w_scale, (TOK_PAD, LANES))
                          for h in range(IDX_HEADS)], axis=0)

    def head_sum(s):
        n = s.shape[1]
        w = wm if n == LANES else jnp.concatenate([wm] * (n // LANES), axis=1)
        return jnp.sum((jnp.maximum(s, 0.0) * w).reshape(IDX_HEADS, TOK_PAD, n), axis=0)

    @pl.when(j == 0)
    def _():
        ki_new = jnp.concatenate([kiw_ref[:, :IDX_DIM], jnp.zeros((page - TOK_PAD, IDX_DIM), F32)], axis=0)
        tok = lax.broadcasted_iota(I32, (TOK_PAD, page), 0) & (n_tok - 1)
        col = lax.broadcasted_iota(I32, (TOK_PAD, page), 1)
        s_new = head_sum(_dot_t(qi, ki_new.astype(BF16)))
        score_sc[:, n_pages * page:] = jnp.where(col <= tok, s_new, -jnp.inf)

    off = pl.multiple_of(j * (pps * page), pps * page)
    ki_t = jnp.concatenate([r[...] for r in page_refs], axis=1)
    score_sc[:, pl.ds(off, pps * page)] = head_sum(_dot(qi, ki_t.astype(BF16)))

    @pl.when(j == pl.num_programs(1) - 1)
    def _():
        key = _sortable(score_sc[...])
        idx = lax.broadcasted_iota(I32, key.shape, 1)

        def count_ge(t):
            return jnp.sum(jnp.where(key >= t, 1, 0), axis=1, keepdims=True)

        thr, n_ge = _kth_largest(count_ge, (TOK_PAD, 1), n_top, key.shape[1])

        def search_cut():
            need = n_top - count_ge(thr + 1)

            def count_tie_below(jc):
                return jnp.sum(jnp.where(key == thr, jnp.where(idx < jc, 1, 0), 0), axis=1, keepdims=True)

            return _tie_cutoff(count_tie_below, (TOK_PAD, 1), need, idx_bits)

        cut = lax.cond(jnp.max(n_ge) > n_top, search_cut,
                       lambda: jnp.full((TOK_PAD, 1), key.shape[1], I32))
        tie = jnp.where(key == thr, jnp.where(idx <= cut, 0.0, -jnp.inf), -jnp.inf)
        sel = jnp.where(key > thr, 0.0, tie)
        bias_ref[0] = jnp.where(key > KEY_NEG_INF, sel, -jnp.inf)


def _idx_sample(page_table, ys, cache_t, layer, pps, n_top, n_tok):
    assert n_tok & (n_tok - 1) == 0 and TOK_PAD % n_tok == 0
    nseq, n_pages = page_table.shape
    page = cache_t.shape[3]
    n_cols = (n_pages + 1) * page
    cache = cache_t
    qiw = IDX_HEADS * IDX_DIM

    def page_spec(i):
        return pl.BlockSpec((None, None, IDX_DIM, page),
                            lambda b, j, pt, i=i: (layer, pt[b, j * pps + i], 0, 0))

    grid_spec = pltpu.PrefetchScalarGridSpec(
        num_scalar_prefetch=1,
        grid=(nseq, n_pages // pps),
        in_specs=[pl.BlockSpec((TOK_PAD, qiw), lambda b, j, pt: (b, COL_QI // qiw)),
                  pl.BlockSpec((TOK_PAD, LANES), lambda b, j, pt: (b, COL_KIW // LANES))]
                 + [page_spec(i) for i in range(pps)],
        out_specs=pl.BlockSpec((1, TOK_PAD, n_cols), lambda b, j, pt: (b, 0, 0)),
        scratch_shapes=[pltpu.VMEM((TOK_PAD, n_cols), F32)],
    )
    return pl.pallas_call(
        functools.partial(_idx_sample_kernel, pps=pps, page=page, n_pages=n_pages, n_top=n_top,
                          idx_bits=int(n_cols).bit_length(), n_tok=n_tok),
        out_shape=jax.ShapeDtypeStruct((nseq, TOK_PAD, n_cols), F32),
        grid_spec=grid_spec,
        compiler_params=_params(("parallel", "arbitrary")),
        name="idx_sample",
    )(page_table, ys, ys, *([cache] * pps))


def _dsa_sample_kernel(pt_ref, q0_ref, q1_ref, new_ref, bias_ref, *rest, pps, page, n_pages):
    page_refs = rest[:pps]
    o_ref = rest[pps]
    m_sc, l_sc, acc_sc = rest[pps + 1:]
    j = pl.program_id(1)
    q = _joint_queries([q0_ref[...], q1_ref[...]], HEAD_DIM ** -0.5)

    def visit(k, v, b8):
        bias = jnp.concatenate([b8] * (J_ROWS // TOK_PAD), axis=0)
        m, l, a = _softmax_block(q, k, v, bias, m_sc[...], l_sc[...], acc_sc[...])
        m_sc[...] = m
        l_sc[...] = l
        acc_sc[...] = a

    @pl.when(j == 0)
    def _():
        m_sc[...] = jnp.full(m_sc.shape, NEG_BIG, F32)
        l_sc[...] = jnp.zeros(l_sc.shape, F32)
        acc_sc[...] = jnp.zeros(acc_sc.shape, F32)
        visit(*_new_token_kv(new_ref, page), bias_ref[0, :, n_pages * page:])

    off = pl.multiple_of(j * (pps * page), pps * page)
    visit(*_gather_kv(page_refs, page), bias_ref[0, :, pl.ds(off, pps * page)])

    @pl.when(j == pl.num_programs(1) - 1)
    def _():
        _unstack_heads(o_ref, acc_sc[...] / l_sc[:, 0:1])


def _dsa_sample(page_table, ys, bias, cache, layer, pps):
    nseq, n_pages = page_table.shape
    rows = cache.shape[2]
    page = rows // KV_SLOTS
    n_cols = bias.shape[2]
    gw = GROUP * HEAD_DIM
    kvw = KV_SLOTS * HEAD_DIM

    def page_spec(i):
        return pl.BlockSpec((None, None, rows, HEAD_DIM),
                            lambda b, j, pt, i=i: (layer, pt[b, j * pps + i], 0, 0))

    grid_spec = pltpu.PrefetchScalarGridSpec(
        num_scalar_prefetch=1,
        grid=(nseq, n_pages // pps),
        in_specs=[pl.BlockSpec((TOK_PAD, gw), lambda b, j, pt: (b, COL_QB // gw)),
                  pl.BlockSpec((TOK_PAD, gw), lambda b, j, pt: (b, COL_QB // gw + 1)),
                  pl.BlockSpec((TOK_PAD, kvw), lambda b, j, pt: (b, COL_KVB // kvw)),
                  pl.BlockSpec((1, TOK_PAD, n_cols), lambda b, j, pt: (b, 0, 0))]
                 + [page_spec(i) for i in range(pps)],
        out_specs=pl.BlockSpec((TOK_PAD, N_HEADS * HEAD_DIM), lambda b, j, pt: (b, 0)),
        scratch_shapes=[pltpu.VMEM((J_ROWS, LANES), F32), pltpu.VMEM((J_ROWS, LANES), F32),
                        pltpu.VMEM((J_ROWS, J_LANES), F32)],
    )
    return pl.pallas_call(
        functools.partial(_dsa_sample_kernel, pps=pps, page=page, n_pages=n_pages),
        out_shape=jax.ShapeDtypeStruct((nseq * TOK_PAD, N_HEADS * HEAD_DIM), F32),
        grid_spec=grid_spec,
        compiler_params=_params(("parallel", "arbitrary")),
        name="dsa_sample",
    )(page_table, ys, ys, ys, bias, *([cache] * pps))


def _layer_norm(x, g, b):
    mu = jnp.mean(x, axis=-1, keepdims=True)
    xc = x - mu
    var = jnp.mean(xc * xc, axis=-1, keepdims=True)
    return xc * lax.rsqrt(var + LN_EPS) * g + b


def _merge_kernel(oa_ref, ob_ref, wa_ref, wb_ref, ga_ref, gb_ref, wo_ref, x_ref, g_ref, b_ref,
                  h_ref, hb_ref, acc_sc, *, alpha):
    kt = pl.program_id(1)

    @pl.when(kt == 0)
    def _():
        acc_sc[...] = jnp.zeros(acc_sc.shape, F32)

    oa = oa_ref[...].astype(BF16)
    ob = ob_ref[...].astype(BF16)
    mix = ga_ref[...] * _dot(oa, wa_ref[...]) + gb_ref[...] * _dot(ob, wb_ref[...])
    acc_sc[...] += _dot(mix.astype(BF16), wo_ref[...])

    @pl.when(kt == pl.num_programs(1) - 1)
    def _():
        h = _layer_norm(alpha * x_ref[...] + acc_sc[...], g_ref[...], b_ref[...])
        h_ref[...] = h
        hb_ref[...] = h.astype(BF16)


def _merge(oa, ob, y, x, wa, wb, wo, g, b, layer, alpha):
    rows = x.shape[0]
    tr = min(rows, 512)
    tk = 512
    kw = N_HEADS * HEAD_DIM
    return pl.pallas_call(
        functools.partial(_merge_kernel, alpha=alpha),
        out_shape=(jax.ShapeDtypeStruct((rows, D_MODEL), F32), jax.ShapeDtypeStruct((rows, D_MODEL), BF16)),
        grid=(rows // tr, D_MODEL // tk),
        in_specs=[pl.BlockSpec((tr, kw), lambda r, k: (r, 0)),
                  pl.BlockSpec((tr, kw), lambda r, k: (r, 0)),
                  pl.BlockSpec((None, kw, tk), lambda r, k: (layer, 0, k)),
                  pl.BlockSpec((None, kw, tk), lambda r, k: (layer, 0, k)),
                  pl.BlockSpec((tr, tk), lambda r, k: (r, COL_GA // tk + k)),
                  pl.BlockSpec((tr, tk), lambda r, k: (r, COL_GB // tk + k)),
                  pl.BlockSpec((None, tk, D_MODEL), lambda r, k: (layer, k, 0)),
                  pl.BlockSpec((tr, D_MODEL), lambda r, k: (r, 0)),
                  pl.BlockSpec((None, 1, D_MODEL), lambda r, k: (layer, 0, 0)),
                  pl.BlockSpec((None, 1, D_MODEL), lambda r, k: (layer, 0, 0))],
        out_specs=(pl.BlockSpec((tr, D_MODEL), lambda r, k: (r, 0)),
                   pl.BlockSpec((tr, D_MODEL), lambda r, k: (r, 0))),
        scratch_shapes=[pltpu.VMEM((tr, D_MODEL), F32)],
        compiler_params=_params(("parallel", "arbitrary")),
        name="merge_ln",
    )(oa, ob, wa, wb, y, y, wo, x, g, b)


def _gelu_tanh(x):
    return 0.5 * x * (1.0 + jnp.tanh(np.sqrt(2.0 / np.pi) * (x + 0.044715 * (x * x * x))))


HALO = 16


def _ffn_kernel(*refs, alpha, tr, seq_len, blocks_per_seq, prompt_mode):
    if prompt_mode:
        (hb_ref, halo_ref, w1a_ref, w1u_ref, cw_ref, cb_ref, w2_ref, h_ref, g_ref, b_ref,
         h2_ref, h2b_ref, a_ref, acc_sc) = refs
    else:
        (hb_ref, s1_ref, s2_ref, w1a_ref, w1u_ref, cw_ref, cb_ref, w2_ref, h_ref, g_ref, b_ref,
         h2_ref, h2b_ref, a_ref, acc_sc) = refs
    r = pl.program_id(0)
    ft = pl.program_id(1)

    @pl.when(ft == 0)
    def _():
        acc_sc[...] = jnp.zeros(acc_sc.shape, F32)

    hb = hb_ref[...]
    a = _dot(hb, w1a_ref[...])
    up = _dot(hb, w1u_ref[...])
    row = lax.broadcasted_iota(I32, a.shape, 0)
    p1 = pltpu.roll(a, 1, 0)
    p2 = pltpu.roll(a, 2, 0)
    if prompt_mode:
        a_halo = _dot(halo_ref[...], w1a_ref[...])
        keep = jnp.where(r % blocks_per_seq == 0, 0.0, 1.0)
        h6 = a_halo[HALO - 2:HALO - 1, :] * keep
        h7 = a_halo[HALO - 1:HALO, :] * keep
        p1 = jnp.where(row == 0, h7, p1)
        p2 = jnp.where(row == 0, h6, jnp.where(row == 1, h7, p2))
        a_ref[...] = a[tr - SUBLANES:, :]
    else:
        t = row & (seq_len - 1)
        p1 = jnp.where(t == 0, s1_ref[...], p1)
        p2 = jnp.where(t < 2, s2_ref[...], p2)
        a_ref[...] = a
    c = cb_ref[...] + cw_ref[0:1, :] * p2 + cw_ref[1:2, :] * p1 + cw_ref[2:3, :] * a
    hmid = (_gelu_tanh(c) * up).astype(BF16)
    acc_sc[...] += _dot(hmid, w2_ref[...])

    @pl.when(ft == pl.num_programs(1) - 1)
    def _():
        h2 = _layer_norm(alpha * h_ref[...] + acc_sc[...], g_ref[...], b_ref[...])
        h2_ref[...] = h2
        h2b_ref[...] = h2.astype(BF16)


def _ffn(h, hb, w1a, w1u, cw, cb, w2, g, b, layer, alpha, seq_len, state=None):
    rows = h.shape[0]
    prompt_mode = state is None
    tr = min(seq_len, 512) if prompt_mode else rows
    n_r = rows // tr
    n_f = D_FF_PAD // TF
    common_w = [pl.BlockSpec((None, D_MODEL, TF), lambda r, f: (layer, 0, f)),
                pl.BlockSpec((None, D_MODEL, TF), lambda r, f: (layer, 0, f)),
                pl.BlockSpec((None, SUBLANES, TF), lambda r, f: (layer, 0, f)),
                pl.BlockSpec((None, 1, TF), lambda r, f: (layer, 0, f)),
                pl.BlockSpec((None, TF, D_MODEL), lambda r, f: (layer, f, 0)),
                pl.BlockSpec((tr, D_MODEL), lambda r, f: (r, 0)),
                pl.BlockSpec((None, 1, D_MODEL), lambda r, f: (layer, 0, 0)),
                pl.BlockSpec((None, 1, D_MODEL), lambda r, f: (layer, 0, 0))]
    if prompt_mode:
        assert seq_len % tr == 0 and tr % HALO == 0
        per = tr // HALO
        extra_specs = [pl.BlockSpec((HALO, D_MODEL), lambda r, f: (jnp.maximum(r * per - 1, 0), 0))]
        extra = [hb]
        a_rows, a_blk = n_r * SUBLANES, SUBLANES
    else:
        assert seq_len & (seq_len - 1) == 0 and seq_len >= CONV_W - 1
        extra_specs = [pl.BlockSpec((tr, TF), lambda r, f: (0, f)),
                       pl.BlockSpec((tr, TF), lambda r, f: (0, f))]
        extra = list(state)
        a_rows, a_blk = rows, tr
    return pl.pallas_call(
        functools.partial(_ffn_kernel, alpha=alpha, tr=tr, seq_len=seq_len,
                          blocks_per_seq=max(seq_len // tr, 1), prompt_mode=prompt_mode),
        out_shape=(jax.ShapeDtypeStruct((rows, D_MODEL), F32), jax.ShapeDtypeStruct((rows, D_MODEL), BF16),
                   jax.ShapeDtypeStruct((a_rows, D_FF_PAD), F32)),
        grid=(n_r, n_f),
        in_specs=[pl.BlockSpec((tr, D_MODEL), lambda r, f: (r, 0))] + extra_specs + common_w,
        out_specs=(pl.BlockSpec((tr, D_MODEL), lambda r, f: (r, 0)),
                   pl.BlockSpec((tr, D_MODEL), lambda r, f: (r, 0)),
                   pl.BlockSpec((a_blk, TF), lambda r, f: (r, f))),
        scratch_shapes=[pltpu.VMEM((tr, D_MODEL), F32)],
        compiler_params=_params(("parallel", "arbitrary")),
        name="conv_ffn_ln",
    )(hb, *extra, w1a, w1u, cw, cb, w2, h, g, b)


def _ple_kernel(hb_ref, wg_ref, p_ref, wp_ref, h_ref, o_ref, ob_ref):
    gate = jax.nn.sigmoid(_dot(hb_ref[...], wg_ref[...]))
    out = h_ref[...] + gate * _dot(p_ref[...], wp_ref[...])
    o_ref[...] = out
    ob_ref[...] = out.astype(BF16)


def _ple(h2, h2b, pb, wg, wp, layer):
    rows = h2.shape[0]
    tr = min(rows, 1024)
    tn = 512
    return pl.pallas_call(
        _ple_kernel,
        out_shape=(jax.ShapeDtypeStruct((rows, D_MODEL), F32), jax.ShapeDtypeStruct((rows, D_MODEL), BF16)),
        grid=(rows // tr, D_MODEL // tn),
        in_specs=[pl.BlockSpec((tr, D_MODEL), lambda r, n: (r, 0)),
                  pl.BlockSpec((None, D_MODEL, tn), lambda r, n: (layer, 0, n)),
                  pl.BlockSpec((None, tr, PLE_DIM), lambda r, n: (layer, r, 0)),
                  pl.BlockSpec((None, PLE_DIM, tn), lambda r, n: (layer, 0, n)),
                  pl.BlockSpec((tr, tn), lambda r, n: (r, n))],
        out_specs=(pl.BlockSpec((tr, tn), lambda r, n: (r, n)),
                   pl.BlockSpec((tr, tn), lambda r, n: (r, n))),
        compiler_params=_params(("parallel", "arbitrary")),
        name="ple_gate",
    )(h2b, wg, pb, wp, h2)


IN_WIDTH = COL_KIW + IDX_DIM + IDX_HEADS + 2 * D_MODEL
PACK_ROWS = 128


SMALL_COLS = IDX_DIM + IDX_HEADS
KIW_TILE = COL_KIW // TN


def _pack_w_in_kernel(prev_ref, cur_ref, o_ref):
    n = pl.program_id(1)

    @pl.when(n < KIW_TILE)
    def _():
        o_ref[...] = cur_ref[...].T.astype(BF16)

    @pl.when(n == KIW_TILE)
    def _():
        t = jnp.concatenate([cur_ref[:SMALL_COLS, :], jnp.zeros((TN - SMALL_COLS, D_MODEL), F32)], axis=0)
        o_ref[...] = t.T.astype(BF16)

    @pl.when(n > KIW_TILE)
    def _():
        t = jnp.concatenate([prev_ref[SMALL_COLS:, :], cur_ref[:SMALL_COLS, :]], axis=0)
        o_ref[...] = t.T.astype(BF16)


def _pack_w_in(w_in):
    depth, rows, width = w_in.shape
    assert width == IN_WIDTH and rows == D_MODEL and SMALL_COLS % 16 == 0
    w_t = jnp.swapaxes(w_in, 1, 2)
    return pl.pallas_call(
        _pack_w_in_kernel,
        out_shape=jax.ShapeDtypeStruct((depth, rows, IN_COLS), BF16),
        grid=(depth, N_IN_TILES),
        in_specs=[pl.BlockSpec((None, TN, D_MODEL), lambda l, n: (l, jnp.maximum(n - 1, KIW_TILE), 0)),
                  pl.BlockSpec((None, TN, D_MODEL), lambda l, n: (l, n, 0))],
        out_specs=pl.BlockSpec((None, D_MODEL, TN), lambda l, n: (l, 0, n)),
        compiler_params=_params(("parallel", "arbitrary")),
        name="pack_w_in",
    )(w_t, w_t)


def _pack_ffn_in_kernel(w_ref, a_ref, u_ref):
    x = w_ref[...]
    zeros = jnp.zeros((x.shape[0], D_FF_PAD - D_FF), BF16)
    a_ref[:, :D_FF] = x[:, :D_FF].astype(BF16)
    a_ref[:, D_FF:] = zeros
    u_ref[:, :D_FF] = x[:, D_FF:].astype(BF16)
    u_ref[:, D_FF:] = zeros


def _pack_ffn_in(w_ffn_in):
    depth, rows, width = w_ffn_in.shape
    out = jax.ShapeDtypeStruct((depth, rows, D_FF_PAD), BF16)
    spec = pl.BlockSpec((None, PACK_ROWS, D_FF_PAD), lambda l, r: (l, r, 0))
    return pl.pallas_call(
        _pack_ffn_in_kernel,
        out_shape=(out, out),
        grid=(depth, rows // PACK_ROWS),
        in_specs=[pl.BlockSpec((None, PACK_ROWS, width), lambda l, r: (l, r, 0))],
        out_specs=(spec, spec),
        compiler_params=_params(("parallel", "parallel")),
        name="pack_ffn_in",
    )(w_ffn_in)


def _pack_ffn_out_kernel(w_ref, o_ref):
    row = pl.program_id(1) * TF + lax.broadcasted_iota(I32, w_ref.shape, 0)
    o_ref[...] = jnp.where(row < D_FF, w_ref[...], 0.0).astype(BF16)


def _pack_ffn_out(w_ffn_out):
    depth, _, cols = w_ffn_out.shape
    return pl.pallas_call(
        _pack_ffn_out_kernel,
        out_shape=jax.ShapeDtypeStruct((depth, D_FF_PAD, cols), BF16),
        grid=(depth, D_FF_PAD // TF),
        in_specs=[pl.BlockSpec((None, TF, cols), lambda l, r: (l, r, 0))],
        out_specs=pl.BlockSpec((None, TF, cols), lambda l, r: (l, r, 0)),
        compiler_params=_params(("parallel", "parallel")),
        name="pack_ffn_out",
    )(w_ffn_out)


def _rope_tables(pos):
    pos = pos.astype(F32)[:, None]

    def table(head_dim):
        rot = head_dim // 4
        half = rot // 2
        inv = ROPE_THETA ** (-(2.0 * jnp.arange(half, dtype=F32)) / rot)
        ang = pos * inv[None, :]
        cos, sin = jnp.cos(ang), jnp.sin(ang)
        ones = jnp.ones((pos.shape[0], head_dim - rot), F32)
        c = jnp.concatenate([cos, cos, ones], axis=1)
        s = jnp.concatenate([-sin, sin, 0.0 * ones], axis=1)
        reps = LANES // head_dim
        return jnp.tile(c, (1, reps)), jnp.tile(s, (1, reps))

    c128, s128 = table(HEAD_DIM)
    c64, s64 = table(IDX_DIM)
    return c128, s128, c64, s64


def kernel(x_prompt, x_sample, cache_sb_kv, cache_dsa_kv, cache_idx_k, state_ffn_conv, page_table,
           p_prompt, p_sample, w_in, w_branch_sb, w_branch_dsa, w_out, ln1_g, ln1_b, w_ffn_in,
           ffn_conv_w, ffn_conv_b, w_ffn_out, ln2_g, ln2_b, w_ple_gate, w_ple_proj):
    batch, seq = x_prompt.shape[:2]
    nseq, n_tok = x_sample.shape[:2]
    depth = w_in.shape[0]
    n_pool, page = cache_sb_kv.shape[1:3]
    n_pages = page_table.shape[1]
    past_len = n_pages * page
    alpha = (2 * depth) ** 0.25
    kv_w = 2 * N_KV * HEAD_DIM
    top_s = max(1, min(TOPK_MAX, (past_len + n_tok) // 4))
    pps = min(32, n_pages)
    pps_idx = min(64, n_pages)

    w_in_p = _pack_w_in(w_in)
    wa = w_branch_sb.astype(BF16)
    wb = w_branch_dsa.astype(BF16)
    wo = w_out.astype(BF16)
    ff_pad = D_FF_PAD - D_FF
    w1a, w1u = _pack_ffn_in(w_ffn_in)
    w2 = _pack_ffn_out(w_ffn_out)
    cw = jnp.pad(ffn_conv_w, ((0, 0), (0, SUBLANES - CONV_W), (0, ff_pad)))
    cb = jnp.pad(ffn_conv_b, ((0, 0), (0, ff_pad)))[:, None, :]
    wg = w_ple_gate.astype(BF16)
    wp = w_ple_proj.astype(BF16)
    g1, b1 = ln1_g[:, None, :], ln1_b[:, None, :]
    g2, b2 = ln2_g[:, None, :], ln2_b[:, None, :]

    tabs_p = _rope_tables(jnp.tile(jnp.arange(seq, dtype=jnp.int32), batch))
    tabs_s = _rope_tables(jnp.tile(past_len + jnp.arange(TOK_PAD, dtype=jnp.int32), nseq))

    sb_pages = cache_sb_kv.reshape(depth, n_pool, page * KV_SLOTS, HEAD_DIM)
    dsa_pages = cache_dsa_kv.reshape(depth, n_pool, page * KV_SLOTS, HEAD_DIM)
    idx_pages_t = jnp.swapaxes(cache_idx_k, 2, 3)

    assert n_tok <= TOK_PAD
    tok_pad = ((0, 0), (0, TOK_PAD - n_tok), (0, 0))
    xp = x_prompt.reshape(batch * seq, D_MODEL)
    xs = jnp.pad(x_sample, tok_pad).reshape(nseq * TOK_PAD, D_MODEL)
    xpb, xsb = xp.astype(BF16), xs.astype(BF16)
    ppb = p_prompt.reshape(depth, batch * seq, PLE_DIM).astype(BF16)
    psb = jnp.pad(p_sample, ((0, 0),) + tok_pad).reshape(depth, nseq * TOK_PAD, PLE_DIM).astype(BF16)
    st = jnp.pad(state_ffn_conv, ((0, 0), (0, 0), (0, 0), (0, ff_pad)))
    conv_s1 = jnp.repeat(st[:, :, 1], TOK_PAD, axis=1)
    conv_s2 = jnp.pad(st, ((0, 0), (0, 0), (0, TOK_PAD - (CONV_W - 1)), (0, 0))).reshape(
        depth, nseq * TOK_PAD, D_FF_PAD)

    outs = {k: [] for k in ("sb_p", "dsa_p", "idx_p", "conv_p", "sb_s", "dsa_s", "idx_s", "conv_s")}
    for l in range(depth):
        y = _project(xpb, w_in_p, l, tabs_p)
        oa = _sb_prompt(y, batch, seq)
        ob = _dsa_prompt(y, batch, seq)
        h, hb = _merge(oa, ob, y, xp, wa, wb, wo, g1, b1, l, alpha)
        h2, h2b, a_tail = _ffn(h, hb, w1a, w1u, cw, cb, w2, g2, b2, l, alpha, seq)
        xp, xpb = _ple(h2, h2b, ppb, wg, wp, l)
        outs["sb_p"].append(y[:, COL_KVA:COL_KVA + kv_w].reshape(batch, seq, 2, N_KV, HEAD_DIM))
        outs["dsa_p"].append(y[:, COL_KVB:COL_KVB + kv_w].reshape(batch, seq, 2, N_KV, HEAD_DIM))
        outs["idx_p"].append(y[:, COL_KIW:COL_KIW + IDX_DIM].reshape(batch, seq, IDX_DIM))
        tails = a_tail.reshape(batch, -1, SUBLANES, D_FF_PAD)[:, -1, SUBLANES - (CONV_W - 1):, :D_FF]
        outs["conv_p"].append(tails)

        ys = _project(xsb, w_in_p, l, tabs_s)
        oa_s = _sb_sample(page_table, ys, sb_pages, l, pps)
        bias = _idx_sample(page_table, ys, idx_pages_t, l, pps_idx, top_s, n_tok)
        ob_s = _dsa_sample(page_table, ys, bias, dsa_pages, l, pps)
        hs, hsb = _merge(oa_s, ob_s, ys, xs, wa, wb, wo, g1, b1, l, alpha)
        h2s, h2sb, a_s = _ffn(hs, hsb, w1a, w1u, cw, cb, w2, g2, b2, l, alpha,
                              TOK_PAD, state=(conv_s1[l], conv_s2[l]))
        xs, xsb = _ple(h2s, h2sb, psb, wg, wp, l)
        ys_tok = ys.reshape(nseq, TOK_PAD, IN_COLS)[:, :n_tok]
        outs["sb_s"].append(ys_tok[..., COL_KVA:COL_KVA + kv_w].reshape(nseq, n_tok, 2, N_KV, HEAD_DIM))
        outs["dsa_s"].append(ys_tok[..., COL_KVB:COL_KVB + kv_w].reshape(nseq, n_tok, 2, N_KV, HEAD_DIM))
        outs["idx_s"].append(ys_tok[..., COL_KIW:COL_KIW + IDX_DIM])
        outs["conv_s"].append(a_s.reshape(nseq, TOK_PAD, D_FF_PAD)[:, n_tok - (CONV_W - 1):n_tok, :D_FF])

    return (xp.reshape(batch, seq, D_MODEL), xs.reshape(nseq, TOK_PAD, D_MODEL)[:, :n_tok],
            jnp.stack(outs["sb_p"]), jnp.stack(outs["dsa_p"]), jnp.stack(outs["idx_p"]),
            jnp.stack(outs["conv_p"]), jnp.stack(outs["sb_s"]), jnp.stack(outs["dsa_s"]),
            jnp.stack(outs["idx_s"]), jnp.stack(outs["conv_s"]))
```

```python
import functools

import jax
import jax.numpy as jnp
import numpy as np
from jax import lax
from jax.experimental import pallas as pl
from jax.experimental.pallas import tpu as pltpu

F32 = jnp.float32
BF16 = jnp.bfloat16
I32 = jnp.int32

D_MODEL = 2048
HEAD_DIM = 128
N_HEADS = 8
N_KV = 2
GROUP = N_HEADS // N_KV
IDX_HEADS = 16
IDX_DIM = 64
TOPK_MAX = 256
ROPE_THETA = 500000.0
D_FF = 5504
CONV_W = 3
PLE_DIM = 256
LN_EPS = 1e-5

LANES = 128
SUBLANES = 8
VMEM_LIMIT = 56 * 1024 * 1024

TN = 512
COL_QA = 0
COL_KVA = 1024
COL_QB = 1536
COL_KVB = 2560
COL_QI = 3072
COL_KIW = 4096
COL_GA = 4608
COL_GB = 6656
IN_COLS = 8704
N_IN_TILES = IN_COLS // TN
D_FF_PAD = 5632
TF = 512

SB_Q_SCALE = (HEAD_DIM ** -0.5) * float(np.log2(np.e))
SB_STACK_ROWS = 64
NEG_BIG = -1e30
KEY_NEG_INF = np.int32(np.array(0xFF800000, dtype=np.uint32).view(np.int32) ^ 0x7FFFFFFF)
INT_MIN = np.int32(-2 ** 31)


def _params(sem):
    return pltpu.CompilerParams(dimension_semantics=sem, vmem_limit_bytes=VMEM_LIMIT)


def _dot_t(a, b):
    return lax.dot_general(a, b, (((1,), (1,)), ((), ())), preferred_element_type=F32)


def _dot(a, b):
    return jnp.dot(a, b, preferred_element_type=F32)


def _rope(y, c, s, half):
    w = y.shape[1]
    reps = w // LANES
    if reps > 1:
        c = jnp.concatenate([c] * reps, axis=1)
        s = jnp.concatenate([s] * reps, axis=1)
    lane = lax.broadcasted_iota(I32, y.shape, 1)
    first = (lane & (2 * half - 1)) < half
    partner = jnp.where(first, pltpu.roll(y, w - half, 1), pltpu.roll(y, half, 1))
    return y * c + partner * s


def _proj_kernel(x_ref, w_ref, c128_ref, s128_ref, c64_ref, s64_ref, y_ref):
    n = pl.program_id(1)

    def product():
        return _dot(x_ref[...], w_ref[...])

    @pl.when(n < COL_QB // TN)
    def _():
        y_ref[...] = product()

    @pl.when(jnp.logical_and(n >= COL_QB // TN, n < COL_KVB // TN))
    def _():
        y_ref[...] = _rope(product(), c128_ref[...], s128_ref[...], 16)

    @pl.when(n == COL_KVB // TN)
    def _():
        y = product()
        y_ref[:, :256] = _rope(y[:, :256], c128_ref[...], s128_ref[...], 16)
        y_ref[:, 256:] = y[:, 256:]

    @pl.when(jnp.logical_and(n >= COL_QI // TN, n < COL_KIW // TN))
    def _():
        y_ref[...] = _rope(product(), c64_ref[...], s64_ref[...], 8)

    @pl.when(n == COL_KIW // TN)
    def _():
        y = product()
        lane = lax.broadcasted_iota(I32, c64_ref.shape, 1)
        c = jnp.where(lane < IDX_DIM, c64_ref[...], 1.0)
        s = jnp.where(lane < IDX_DIM, s64_ref[...], 0.0)
        y_ref[:, :LANES] = _rope(y[:, :LANES], c, s, 8)
        y_ref[:, LANES:] = y[:, LANES:]

    @pl.when(n >= COL_GA // TN)
    def _():
        y_ref[...] = jax.nn.sigmoid(product())


def _project(xb, w, layer, tabs):
    rows = xb.shape[0]
    tr = min(rows, 1024)
    tab_spec = pl.BlockSpec((tr, LANES), lambda r, n: (r, 0))
    return pl.pallas_call(
        _proj_kernel,
        out_shape=jax.ShapeDtypeStruct((rows, IN_COLS), F32),
        grid=(rows // tr, N_IN_TILES),
        in_specs=[pl.BlockSpec((tr, D_MODEL), lambda r, n: (r, 0)),
                  pl.BlockSpec((None, D_MODEL, TN), lambda r, n: (layer, 0, n)),
                  tab_spec, tab_spec, tab_spec, tab_spec],
        out_specs=pl.BlockSpec((tr, TN), lambda r, n: (r, n)),
        compiler_params=_params(("parallel", "arbitrary")),
        name="in_proj",
    )(xb, w, *tabs)


def _suffix_matrix():
    j = lax.broadcasted_iota(I32, (2 * LANES, 2 * LANES), 0) & (LANES - 1)
    s = lax.broadcasted_iota(I32, (2 * LANES, 2 * LANES), 1)
    return jnp.where(jnp.logical_or(j > s, s >= LANES), -1.0, 0.0).astype(BF16)


def _sb_scores(q, k, mask):
    n_sub = k.shape[0] // LANES
    z = _dot_t(q, k)
    neg_abs = pltpu.bitcast(pltpu.bitcast(z, I32) | INT_MIN, F32)
    sp = jnp.maximum(z, 0.0) + jnp.log2(1.0 + jnp.exp2(neg_abs))
    spm = sp if mask is None else jnp.where(mask, sp, 0.0)
    hi = spm.astype(BF16)
    lo = (spm - hi.astype(F32)).astype(BF16)
    subs = [jnp.concatenate([hi[:, i * LANES:(i + 1) * LANES], lo[:, i * LANES:(i + 1) * LANES]], axis=1)
            for i in range(n_sub)]
    return z - sp, subs


def _sb_weights(d, subs, v, u, carry, mask):
    n_sub = len(subs)
    m_rows = d.shape[0]
    if m_rows <= SB_STACK_ROWS:
        r_all = _dot(subs[0] if n_sub == 1 else jnp.concatenate(subs, axis=0), u)
        rs = [r_all[i * m_rows:(i + 1) * m_rows] for i in range(n_sub)]
    else:
        rs = [_dot(s, u) for s in subs]
    afters = [None] * n_sub
    for i in reversed(range(n_sub)):
        afters[i] = rs[i][:, :LANES] + carry
        carry = carry + rs[i][:, LANES:]
    after = afters[0] if n_sub == 1 else jnp.concatenate(afters, axis=1)
    w = jnp.exp2(d + after)
    if mask is not None:
        w = jnp.where(mask, w, 0.0)
    return _dot(w.astype(BF16), v), carry


def _sb_block(q, k, v, u, carry, mask):
    d, subs = _sb_scores(q, k, mask)
    return _sb_weights(d, subs, v, u, carry, mask)


def _softmax_block(q, k, v, bias, m_old, l_old, acc_old):
    logit = _dot_t(q, k) + bias
    m_new = jnp.maximum(m_old, jnp.max(logit, axis=1, keepdims=True))
    p = jnp.exp(logit - m_new[:, 0:1])
    alpha = jnp.exp(m_old - m_new)
    l_new = alpha * l_old + jnp.sum(p, axis=1, keepdims=True)
    acc_new = alpha[:, 0:1] * acc_old + _dot(p.astype(BF16), v)
    return m_new, l_new, acc_new


def _sortable(x):
    b = pltpu.bitcast(x, I32)
    return jnp.where(b < 0, b ^ jnp.int32(0x7FFFFFFF), b)


def _kth_largest(count_ge, shape, k, n_total):
    bits_per_check = 4

    def cond(state):
        i, _, cnt = state
        return jnp.logical_and(i < 32, jnp.max(cnt) > k)

    def body(state):
        i, t, cnt = state
        for _ in range(bits_per_check):
            cand = t + jnp.left_shift(jnp.int32(1), jnp.int32(31) - i)
            c = count_ge(cand)
            ok = c >= k
            i, t, cnt = i + 1, jnp.where(ok, cand, t), jnp.where(ok, c, cnt)
        return i, t, cnt

    state = (jnp.int32(0), jnp.full(shape, INT_MIN, I32), jnp.full(shape, n_total, I32))
    _, t, cnt = lax.while_loop(cond, body, state)
    return t, cnt


def _tie_cutoff(count_tie_below, shape, need, n_bits):
    def body(i, j):
        cand = j + jnp.left_shift(jnp.int32(1), jnp.int32(n_bits - 1) - i)
        return jnp.where(count_tie_below(cand) < need, cand, j)

    return lax.fori_loop(0, n_bits, body, jnp.zeros(shape, I32))


def _sb_prompt_kernel(q_ref, k_ref, v_ref, o_ref, kb_sc, vb_sc, carry_sc, acc_sc, *, tq):
    iq = pl.program_id(1)

    @pl.when(iq == 0)
    def _():
        kb_sc[...] = k_ref[...].astype(BF16)
        vb_sc[...] = v_ref[...].astype(BF16)

    scale = SB_Q_SCALE
    u = _suffix_matrix()
    n_chain = N_HEADS // 2
    c_rows = 2 * tq
    qs = []
    for c in range(n_chain):
        q = jnp.concatenate([q_ref[:, h * HEAD_DIM:(h + 1) * HEAD_DIM] for h in (2 * c, 2 * c + 1)], axis=0)
        qs.append((q * scale).astype(BF16))

    def kv_block(o2, c):
        g = (2 * c) // GROUP
        return (kb_sc[pl.ds(o2, tq), g * HEAD_DIM:(g + 1) * HEAD_DIM],
                vb_sc[pl.ds(o2, tq), g * HEAD_DIM:(g + 1) * HEAD_DIM])

    row = lax.broadcasted_iota(I32, (c_rows, tq), 0) & (tq - 1)
    col = lax.broadcasted_iota(I32, (c_rows, tq), 1)
    off = pl.multiple_of(iq * tq, tq)
    for c in range(n_chain):
        k, v = kv_block(off, c)
        contrib, carry = _sb_block(qs[c], k, v, u, jnp.zeros((c_rows, LANES), F32), col < row)
        acc_sc[c] = contrib
        carry_sc[c] = carry

    def body(i, _):
        o2 = pl.multiple_of((iq - 1 - i) * tq, tq)
        for c in range(n_chain):
            k, v = kv_block(o2, c)
            contrib, carry = _sb_block(qs[c], k, v, u, carry_sc[c], None)
            acc_sc[c] += contrib
            carry_sc[c] = carry
        return 0

    lax.fori_loop(0, iq, body, 0)
    for h in range(N_HEADS):
        r0 = (h % 2) * tq
        o_ref[:, h * HEAD_DIM:(h + 1) * HEAD_DIM] = acc_sc[h // 2, r0:r0 + tq, :].astype(o_ref.dtype)


def _sb_prompt(y, batch, seq):
    tq = min(2 * LANES, seq)
    nq = seq // tq
    qw = N_HEADS * HEAD_DIM
    kvw = N_KV * HEAD_DIM
    return pl.pallas_call(
        functools.partial(_sb_prompt_kernel, tq=tq),
        out_shape=jax.ShapeDtypeStruct((batch * seq, qw), BF16),
        grid=(batch, nq),
        in_specs=[pl.BlockSpec((tq, qw), lambda b, i: (b * nq + i, COL_QA // qw)),
                  pl.BlockSpec((seq, kvw), lambda b, i: (b, COL_KVA // kvw)),
                  pl.BlockSpec((seq, kvw), lambda b, i: (b, COL_KVA // kvw + 1))],
        out_specs=pl.BlockSpec((tq, qw), lambda b, i: (b * nq + i, 0)),
        scratch_shapes=[pltpu.VMEM((seq, kvw), BF16), pltpu.VMEM((seq, kvw), BF16),
                        pltpu.VMEM((N_HEADS // 2, 2 * tq, LANES), F32),
                        pltpu.VMEM((N_HEADS // 2, 2 * tq, LANES), F32)],
        compiler_params=_params(("parallel", "arbitrary")),
        name="sb_prompt",
    )(y, y, y)


TKI = 256
LOGIT_SAFE = 60.0
NORM_SLACK = 1.02


def _fold_rows(x, op):
    return op(x.reshape(x.shape[0] // SUBLANES, SUBLANES, x.shape[1]), axis=0)


def _dsa_prompt_kernel(qb0_ref, qb1_ref, k_ref, v_ref, qi_ref, kiw_all_ref, kiw_q_ref, o_ref,
                       kb_sc, vb_sc, ki2_sc, key_sc, bias_sc, cut_sc, kn_sc, m_sc, l_sc, acc_sc,
                       *, tq, n_top, idx_bits):
    iq = pl.program_id(1)
    seq = k_ref.shape[0]

    @pl.when(iq == 0)
    def _():
        kb_sc[...] = k_ref[...].astype(BF16)
        vb_sc[...] = v_ref[...].astype(BF16)
        k_sq = k_ref[...] * k_ref[...]
        k_sq_max = jnp.float32(0.0)
        for g in range(N_KV):
            norms = jnp.sum(k_sq[:, g * HEAD_DIM:(g + 1) * HEAD_DIM], axis=1, keepdims=True)
            k_sq_max = jnp.maximum(k_sq_max, jnp.max(norms))
        kn_sc[...] = jnp.zeros(kn_sc.shape, F32) + k_sq_max
        lane = lax.broadcasted_iota(I32, (seq, LANES), 1)
        kia = jnp.where(lane < IDX_DIM, kiw_all_ref[...], 0.0)
        ki2_sc[0] = kia.astype(BF16)
        ki2_sc[1] = pltpu.roll(kia, IDX_DIM, 1).astype(BF16)

    w_t = (kiw_q_ref[...] * ((IDX_DIM ** -0.5) * (IDX_HEADS ** -0.5))).T
    w_rows = [w_t[IDX_DIM + h:IDX_DIM + h + 1, :] for h in range(IDX_HEADS)]

    q_pairs = [qi_ref[:, p * LANES:(p + 1) * LANES].astype(BF16) for p in range(IDX_HEADS // 2)]
    q_pos = iq * tq + lax.broadcasted_iota(I32, (1, tq), 1)
    n_blk = (iq * tq + tq + TKI - 1) // TKI

    def idx_body(j, _):
        off = pl.multiple_of(j * TKI, TKI)
        ki_e = ki2_sc[0, pl.ds(off, TKI), :]
        ki_o = ki2_sc[1, pl.ds(off, TKI), :]
        acc = jnp.zeros((TKI, tq), F32)
        for p in range(IDX_HEADS // 2):
            acc = acc + jnp.maximum(_dot_t(ki_e, q_pairs[p]), 0.0) * w_rows[2 * p]
            acc = acc + jnp.maximum(_dot_t(ki_o, q_pairs[p]), 0.0) * w_rows[2 * p + 1]
        k_pos = off + lax.broadcasted_iota(I32, (TKI, 1), 0)
        key_sc[pl.ds(off, TKI), :] = jnp.where(k_pos <= q_pos, _sortable(acc), KEY_NEG_INF)
        return 0

    lax.fori_loop(0, n_blk, idx_body, 0)

    def query_counts(hit_fn):
        def body(j, c):
            off = pl.multiple_of(j * TKI, TKI)
            return c + _fold_rows(hit_fn(key_sc[pl.ds(off, TKI), :], off), jnp.sum)

        c = lax.fori_loop(0, n_blk, body, jnp.zeros((SUBLANES, tq), I32))
        return jnp.sum(c, axis=0, keepdims=True)

    def count_ge(t):
        return query_counts(lambda kb, off: jnp.where(kb >= t, 1, 0))

    thr, n_ge = _kth_largest(count_ge, (1, tq), n_top, n_blk * TKI)

    cut_sc[...] = jnp.full(cut_sc.shape, seq, I32)

    @pl.when(jnp.max(n_ge) > n_top)
    def _():
        need = n_top - count_ge(thr + 1)

        def count_tie_below(jc):
            def hit(kb, off):
                idx = off + lax.broadcasted_iota(I32, kb.shape, 0)
                return jnp.where(kb == thr, jnp.where(idx < jc, 1, 0), 0)
            return query_counts(hit)

        cut = _tie_cutoff(count_tie_below, (1, tq), need, idx_bits)
        cut_sc[...] = jnp.broadcast_to(cut, cut_sc.shape)

    cut = cut_sc[0:1, :]

    def bias_body(j, _):
        off = pl.multiple_of(j * TKI, TKI)
        kb = key_sc[pl.ds(off, TKI), :]
        idx = off + lax.broadcasted_iota(I32, (TKI, tq), 0)
        tie = jnp.where(kb == thr, jnp.where(idx <= cut, 0.0, -jnp.inf), -jnp.inf)
        sel = jnp.where(kb > thr, 0.0, tie)
        sel = jnp.where(kb > KEY_NEG_INF, sel, -jnp.inf)
        bias_sc[:, pl.ds(off, TKI)] = sel.T
        return 0

    lax.fori_loop(0, n_blk, bias_body, 0)

    scale = (HEAD_DIM ** -0.5) * np.log2(np.e)
    n_chain = N_HEADS // 2
    n_sub = TKI // LANES
    qs = []
    ones = jnp.ones((HEAD_DIM, LANES), BF16)
    q_sq = jnp.zeros((SUBLANES, LANES), F32)
    for c in range(n_chain):
        q_ref = (qb0_ref, qb1_ref)[c // 2]
        hs = (2 * (c % 2), 2 * (c % 2) + 1)
        q = jnp.concatenate([q_ref[:, h * HEAD_DIM:(h + 1) * HEAD_DIM] for h in hs], axis=0) * scale
        q_sq = jnp.maximum(q_sq, _fold_rows(_dot((q * q).astype(BF16), ones), jnp.max))
        qs.append(q.astype(BF16))
    q_sq_max = jnp.max(q_sq) * NORM_SLACK

    def bias_rows(off):
        b = bias_sc[:, pl.ds(off, TKI)]
        return jnp.concatenate([b, b], axis=0)

    small_logits = q_sq_max * jnp.max(kn_sc[...]) <= LOGIT_SAFE * LOGIT_SAFE

    @pl.when(small_logits)
    def _():
        m_sc[...] = jnp.zeros(m_sc.shape, F32)

    @pl.when(jnp.logical_not(small_logits))
    def _():
        m_sc[...] = jnp.full(m_sc.shape, -jnp.inf, F32)

        def max_body(j, _):
            off = pl.multiple_of(j * TKI, TKI)
            bias = bias_rows(off)
            for c in range(n_chain):
                g = c // 2
                lg = _dot_t(qs[c], kb_sc[pl.ds(off, TKI), g * HEAD_DIM:(g + 1) * HEAD_DIM]) + bias
                m = m_sc[c]
                for i in range(n_sub):
                    m = jnp.maximum(m, lg[:, i * LANES:(i + 1) * LANES])
                m_sc[c] = m
            return 0

        lax.fori_loop(0, n_blk, max_body, 0)
        for c in range(n_chain):
            m_sc[c] = jnp.broadcast_to(jnp.max(m_sc[c], axis=1, keepdims=True), (2 * tq, LANES))

    l_sc[...] = jnp.zeros(l_sc.shape, F32)
    acc_sc[...] = jnp.zeros(acc_sc.shape, F32)

    def sum_body(j, _):
        off = pl.multiple_of(j * TKI, TKI)
        bias = bias_rows(off)
        for c in range(n_chain):
            g = c // 2
            lg = _dot_t(qs[c], kb_sc[pl.ds(off, TKI), g * HEAD_DIM:(g + 1) * HEAD_DIM]) + bias
            p = jnp.exp2(lg - jnp.concatenate([m_sc[c]] * n_sub, axis=1))
            l = l_sc[c]
            for i in range(n_sub):
                l = l + p[:, i * LANES:(i + 1) * LANES]
            l_sc[c] = l
            acc_sc[c] += _dot(p.astype(BF16), vb_sc[pl.ds(off, TKI), g * HEAD_DIM:(g + 1) * HEAD_DIM])
        return 0

    lax.fori_loop(0, n_blk, sum_body, 0)
    for c in range(n_chain):
        out = acc_sc[c] / jnp.sum(l_sc[c], axis=1, keepdims=True)
        for hh in range(2):
            c0 = (2 * c + hh) * HEAD_DIM
            o_ref[:, c0:c0 + HEAD_DIM] = out[hh * tq:(hh + 1) * tq, :].astype(o_ref.dtype)


def _dsa_prompt(y, batch, seq):
    tq = min(2 * LANES, seq)
    assert tq == TKI or seq == tq
    nq = seq // tq
    gw = GROUP * HEAD_DIM
    kvw = N_KV * HEAD_DIM
    n_top = max(1, min(TOPK_MAX, seq // 4))
    idx_bits = int(seq).bit_length()
    q_rows = 2 * tq
    return pl.pallas_call(
        functools.partial(_dsa_prompt_kernel, tq=tq, n_top=n_top, idx_bits=idx_bits),
        out_shape=jax.ShapeDtypeStruct((batch * seq, N_HEADS * HEAD_DIM), BF16),
        grid=(batch, nq),
        in_specs=[pl.BlockSpec((tq, gw), lambda b, i: (b * nq + i, COL_QB // gw)),
                  pl.BlockSpec((tq, gw), lambda b, i: (b * nq + i, COL_QB // gw + 1)),
                  pl.BlockSpec((seq, kvw), lambda b, i: (b, COL_KVB // kvw)),
                  pl.BlockSpec((seq, kvw), lambda b, i: (b, COL_KVB // kvw + 1)),
                  pl.BlockSpec((tq, IDX_HEADS * IDX_DIM), lambda b, i: (b * nq + i, COL_QI // (IDX_HEADS * IDX_DIM))),
                  pl.BlockSpec((seq, LANES), lambda b, i: (b, COL_KIW // LANES)),
                  pl.BlockSpec((tq, LANES), lambda b, i: (b * nq + i, COL_KIW // LANES))],
        out_specs=pl.BlockSpec((tq, N_HEADS * HEAD_DIM), lambda b, i: (b * nq + i, 0)),
        scratch_shapes=[pltpu.VMEM((seq, kvw), BF16),
                        pltpu.VMEM((seq, kvw), BF16),
                        pltpu.VMEM((2, seq, LANES), BF16),
                        pltpu.VMEM((seq, tq), I32),
                        pltpu.VMEM((tq, seq), F32),
                        pltpu.VMEM((SUBLANES, tq), I32),
                        pltpu.VMEM((SUBLANES, LANES), F32),
                        pltpu.VMEM((N_HEADS // 2, q_rows, LANES), F32),
                        pltpu.VMEM((N_HEADS // 2, q_rows, LANES), F32),
                        pltpu.VMEM((N_HEADS // 2, q_rows, HEAD_DIM), F32)],
        compiler_params=_params(("parallel", "arbitrary")),
        name="dsa_prompt",
    )(y, y, y, y, y, y, y)


TOK_PAD = SUBLANES
S_ROWS = GROUP * TOK_PAD


KV_SLOTS = 2 * N_KV


def _page_rows(ref, slot, page):
    return ref[pl.ds(slot, page, stride=KV_SLOTS), :].astype(BF16)


J_ROWS = N_KV * S_ROWS
J_LANES = N_KV * HEAD_DIM


def _gather_kv(page_refs, page):
    k = [jnp.concatenate([_page_rows(r, g, page) for g in range(N_KV)], axis=1) for r in page_refs]
    v = [jnp.concatenate([_page_rows(r, N_KV + g, page) for g in range(N_KV)], axis=1) for r in page_refs]
    if len(page_refs) == 1:
        return k[0], v[0]
    return jnp.concatenate(k, axis=0), jnp.concatenate(v, axis=0)


def _joint_queries(q_blks, scale):
    rows = []
    for g, q_blk in enumerate(q_blks):
        q = jnp.concatenate([q_blk[:, h * HEAD_DIM:(h + 1) * HEAD_DIM] for h in range(GROUP)], axis=0) * scale
        zero = jnp.zeros_like(q)
        rows.append(jnp.concatenate([q if gg == g else zero for gg in range(N_KV)], axis=1))
    return jnp.concatenate(rows, axis=0).astype(BF16)


def _new_token_kv(new_ref, page):
    pad = jnp.zeros((page - TOK_PAD, J_LANES), F32)
    k = jnp.concatenate([new_ref[:, :J_LANES], pad], axis=0)
    v = jnp.concatenate([new_ref[:, J_LANES:], pad], axis=0)
    return k.astype(BF16), v.astype(BF16)


def _unstack_heads(o_ref, acc):
    for g in range(N_KV):
        for h in range(GROUP):
            r0 = g * S_ROWS + h * TOK_PAD
            c0 = (g * GROUP + h) * HEAD_DIM
            o_ref[:, c0:c0 + HEAD_DIM] = acc[r0:r0 + TOK_PAD, g * HEAD_DIM:(g + 1) * HEAD_DIM]


def _sb_sample_kernel(pt_ref, q_ref, new_ref, *rest, pps, page):
    page_refs = rest[:pps]
    o_ref = rest[pps]
    carry_sc, acc_sc = rest[pps + 1:]
    j = pl.program_id(1)
    u = _suffix_matrix()
    gw = GROUP * HEAD_DIM
    q = _joint_queries([q_ref[:, g * gw:(g + 1) * gw] for g in range(N_KV)], SB_Q_SCALE)

    def visit(k, v, mask):
        c, cr = _sb_block(q, k, v, u, carry_sc[...], mask)
        acc_sc[...] += c
        carry_sc[...] = cr

    @pl.when(j == 0)
    def _():
        carry_sc[...] = jnp.zeros(carry_sc.shape, F32)
        acc_sc[...] = jnp.zeros(acc_sc.shape, F32)
        tok = lax.broadcasted_iota(I32, (J_ROWS, page), 0) & (TOK_PAD - 1)
        col = lax.broadcasted_iota(I32, (J_ROWS, page), 1)
        visit(*_new_token_kv(new_ref, page), col < tok)

    visit(*_gather_kv(page_refs, page), None)

    @pl.when(j == pl.num_programs(1) - 1)
    def _():
        _unstack_heads(o_ref, acc_sc[...])


def _sb_sample(page_table, ys, cache, layer, pps):
    nseq, n_pages = page_table.shape
    rows = cache.shape[2]
    page = rows // KV_SLOTS
    n_steps = n_pages // pps
    qw = N_HEADS * HEAD_DIM
    kvw = KV_SLOTS * HEAD_DIM

    def page_spec(i):
        return pl.BlockSpec((None, None, rows, HEAD_DIM),
                            lambda b, j, pt, i=i: (layer, pt[b, (n_steps - 1 - j) * pps + i], 0, 0))

    grid_spec = pltpu.PrefetchScalarGridSpec(
        num_scalar_prefetch=1,
        grid=(nseq, n_steps),
        in_specs=[pl.BlockSpec((TOK_PAD, qw), lambda b, j, pt: (b, COL_QA // qw)),
                  pl.BlockSpec((TOK_PAD, kvw), lambda b, j, pt: (b, COL_KVA // kvw))]
                 + [page_spec(i) for i in range(pps)],
        out_specs=pl.BlockSpec((TOK_PAD, qw), lambda b, j, pt: (b, 0)),
        scratch_shapes=[pltpu.VMEM((J_ROWS, LANES), F32), pltpu.VMEM((J_ROWS, J_LANES), F32)],
    )
    return pl.pallas_call(
        functools.partial(_sb_sample_kernel, pps=pps, page=page),
        out_shape=jax.ShapeDtypeStruct((nseq * TOK_PAD, qw), F32),
        grid_spec=grid_spec,
        compiler_params=_params(("parallel", "arbitrary")),
        name="sb_sample",
    )(page_table, ys, ys, *([cache] * pps))


def _idx_sample_kernel(pt_ref, qi_ref, kiw_ref, *rest, pps, page, n_pages, n_top, idx_bits, n_tok):
    page_refs = rest[:pps]
    bias_ref = rest[pps]
    score_sc = rest[pps + 1]
    j = pl.program_id(1)

    row = lax.broadcasted_iota(I32, (TOK_PAD, 1), 0)

    def real_rows(x):
        return jnp.where(row < n_tok, x, pltpu.roll(x, n_tok, 0))

    qi_rows = real_rows(qi_ref[...])
    kiw = real_rows(kiw_ref[...])
    qi = jnp.concatenate([qi_rows[:, h * IDX_DIM:(h + 1) * IDX_DIM] for h in range(IDX_HEADS)],
                         axis=0).astype(BF16)
    w_scale = (IDX_DIM ** -0.5) * (IDX_HEADS ** -0.5)
    wm = jnp.concatenate([jnp.broadcast_to(kiw[:, IDX_DIM + h:IDX_DIM + h + 1] * w_scale, (TOK_PAD, LANES))
                          for h in range(IDX_HEADS)], axis=0)

    def head_sum(s):
        n = s.shape[1]
        w = wm if n == LANES else jnp.concatenate([wm] * (n // LANES), axis=1)
        return jnp.sum((jnp.maximum(s, 0.0) * w).reshape(IDX_HEADS, TOK_PAD, n), axis=0)

    @pl.when(j == 0)
    def _():
        ki_new = jnp.concatenate([kiw_ref[:, :IDX_DIM], jnp.zeros((page - TOK_PAD, IDX_DIM), F32)], axis=0)
        tok = lax.broadcasted_iota(I32, (TOK_PAD, page), 0) & (n_tok - 1)
        col = lax.broadcasted_iota(I32, (TOK_PAD, page), 1)
        s_new = head_sum(_dot_t(qi, ki_new.astype(BF16)))
        score_sc[:, n_pages * page:] = jnp.where(col <= tok, s_new, -jnp.inf)

    off = pl.multiple_of(j * (pps * page), pps * page)
    ki_t = jnp.concatenate([r[...] for r in page_refs], axis=1)
    score_sc[:, pl.ds(off, pps * page)] = head_sum(_dot(qi, ki_t.astype(BF16)))

    @pl.when(j == pl.num_programs(1) - 1)
    def _():
        key = _sortable(score_sc[...])
        idx = lax.broadcasted_iota(I32, key.shape, 1)

        def count_ge(t):
            return jnp.sum(jnp.where(key >= t, 1, 0), axis=1, keepdims=True)

        thr, n_ge = _kth_largest(count_ge, (TOK_PAD, 1), n_top, key.shape[1])

        def search_cut():
            need = n_top - count_ge(thr + 1)

            def count_tie_below(jc):
                return jnp.sum(jnp.where(key == thr, jnp.where(idx < jc, 1, 0), 0), axis=1, keepdims=True)

            return _tie_cutoff(count_tie_below, (TOK_PAD, 1), need, idx_bits)

        cut = lax.cond(jnp.max(n_ge) > n_top, search_cut,
                       lambda: jnp.full((TOK_PAD, 1), key.shape[1], I32))
        tie = jnp.where(key == thr, jnp.where(idx <= cut, 0.0, -jnp.inf), -jnp.inf)
        sel = jnp.where(key > thr, 0.0, tie)
        bias_ref[0] = jnp.where(key > KEY_NEG_INF, sel, -jnp.inf)


def _idx_sample(page_table, ys, cache_t, layer, pps, n_top, n_tok):
    assert n_tok & (n_tok - 1) == 0 and TOK_PAD % n_tok == 0
    nseq, n_pages = page_table.shape
    page = cache_t.shape[3]
    n_cols = (n_pages + 1) * page
    cache = cache_t
    qiw = IDX_HEADS * IDX_DIM

    def page_spec(i):
        return pl.BlockSpec((None, None, IDX_DIM, page),
                            lambda b, j, pt, i=i: (layer, pt[b, j * pps + i], 0, 0))

    grid_spec = pltpu.PrefetchScalarGridSpec(
        num_scalar_prefetch=1,
        grid=(nseq, n_pages // pps),
        in_specs=[pl.BlockSpec((TOK_PAD, qiw), lambda b, j, pt: (b, COL_QI // qiw)),
                  pl.BlockSpec((TOK_PAD, LANES), lambda b, j, pt: (b, COL_KIW // LANES))]
                 + [page_spec(i) for i in range(pps)],
        out_specs=pl.BlockSpec((1, TOK_PAD, n_cols), lambda b, j, pt: (b, 0, 0)),
        scratch_shapes=[pltpu.VMEM((TOK_PAD, n_cols), F32)],
    )
    return pl.pallas_call(
        functools.partial(_idx_sample_kernel, pps=pps, page=page, n_pages=n_pages, n_top=n_top,
                          idx_bits=int(n_cols).bit_length(), n_tok=n_tok),
        out_shape=jax.ShapeDtypeStruct((nseq, TOK_PAD, n_cols), F32),
        grid_spec=grid_spec,
        compiler_params=_params(("parallel", "arbitrary")),
        name="idx_sample",
    )(page_table, ys, ys, *([cache] * pps))


def _dsa_sample_kernel(pt_ref, q0_ref, q1_ref, new_ref, bias_ref, *rest, pps, page, n_pages):
    page_refs = rest[:pps]
    o_ref = rest[pps]
    m_sc, l_sc, acc_sc = rest[pps + 1:]
    j = pl.program_id(1)
    q = _joint_queries([q0_ref[...], q1_ref[...]], HEAD_DIM ** -0.5)

    def visit(k, v, b8):
        bias = jnp.concatenate([b8] * (J_ROWS // TOK_PAD), axis=0)
        m, l, a = _softmax_block(q, k, v, bias, m_sc[...], l_sc[...], acc_sc[...])
        m_sc[...] = m
        l_sc[...] = l
        acc_sc[...] = a

    @pl.when(j == 0)
    def _():
        m_sc[...] = jnp.full(m_sc.shape, NEG_BIG, F32)
        l_sc[...] = jnp.zeros(l_sc.shape, F32)
        acc_sc[...] = jnp.zeros(acc_sc.shape, F32)
        visit(*_new_token_kv(new_ref, page), bias_ref[0, :, n_pages * page:])

    off = pl.multiple_of(j * (pps * page), pps * page)
    visit(*_gather_kv(page_refs, page), bias_ref[0, :, pl.ds(off, pps * page)])

    @pl.when(j == pl.num_programs(1) - 1)
    def _():
        _unstack_heads(o_ref, acc_sc[...] / l_sc[:, 0:1])


def _dsa_sample(page_table, ys, bias, cache, layer, pps):
    nseq, n_pages = page_table.shape
    rows = cache.shape[2]
    page = rows // KV_SLOTS
    n_cols = bias.shape[2]
    gw = GROUP * HEAD_DIM
    kvw = KV_SLOTS * HEAD_DIM

    def page_spec(i):
        return pl.BlockSpec((None, None, rows, HEAD_DIM),
                            lambda b, j, pt, i=i: (layer, pt[b, j * pps + i], 0, 0))

    grid_spec = pltpu.PrefetchScalarGridSpec(
        num_scalar_prefetch=1,
        grid=(nseq, n_pages // pps),
        in_specs=[pl.BlockSpec((TOK_PAD, gw), lambda b, j, pt: (b, COL_QB // gw)),
                  pl.BlockSpec((TOK_PAD, gw), lambda b, j, pt: (b, COL_QB // gw + 1)),
                  pl.BlockSpec((TOK_PAD, kvw), lambda b, j, pt: (b, COL_KVB // kvw)),
                  pl.BlockSpec((1, TOK_PAD, n_cols), lambda b, j, pt: (b, 0, 0))]
                 + [page_spec(i) for i in range(pps)],
        out_specs=pl.BlockSpec((TOK_PAD, N_HEADS * HEAD_DIM), lambda b, j, pt: (b, 0)),
        scratch_shapes=[pltpu.VMEM((J_ROWS, LANES), F32), pltpu.VMEM((J_ROWS, LANES), F32),
                        pltpu.VMEM((J_ROWS, J_LANES), F32)],
    )
    return pl.pallas_call(
        functools.partial(_dsa_sample_kernel, pps=pps, page=page, n_pages=n_pages),
        out_shape=jax.ShapeDtypeStruct((nseq * TOK_PAD, N_HEADS * HEAD_DIM), F32),
        grid_spec=grid_spec,
        compiler_params=_params(("parallel", "arbitrary")),
        name="dsa_sample",
    )(page_table, ys, ys, ys, bias, *([cache] * pps))


def _layer_norm(x, g, b):
    mu = jnp.mean(x, axis=-1, keepdims=True)
    xc = x - mu
    var = jnp.mean(xc * xc, axis=-1, keepdims=True)
    return xc * lax.rsqrt(var + LN_EPS) * g + b


def _merge_kernel(oa_ref, ob_ref, wa_ref, wb_ref, ga_ref, gb_ref, wo_ref, x_ref, g_ref, b_ref,
                  h_ref, hb_ref, acc_sc, *, alpha):
    kt = pl.program_id(1)

    @pl.when(kt == 0)
    def _():
        acc_sc[...] = jnp.zeros(acc_sc.shape, F32)

    oa = oa_ref[...].astype(BF16)
    ob = ob_ref[...].astype(BF16)
    mix = ga_ref[...] * _dot(oa, wa_ref[...]) + gb_ref[...] * _dot(ob, wb_ref[...])
    acc_sc[...] += _dot(mix.astype(BF16), wo_ref[...])

    @pl.when(kt == pl.num_programs(1) - 1)
    def _():
        h = _layer_norm(alpha * x_ref[...] + acc_sc[...], g_ref[...], b_ref[...])
        h_ref[...] = h
        hb_ref[...] = h.astype(BF16)


def _merge(oa, ob, y, x, wa, wb, wo, g, b, layer, alpha):
    rows = x.shape[0]
    tr = min(rows, 512)
    tk = 512
    kw = N_HEADS * HEAD_DIM
    return pl.pallas_call(
        functools.partial(_merge_kernel, alpha=alpha),
        out_shape=(jax.ShapeDtypeStruct((rows, D_MODEL), F32), jax.ShapeDtypeStruct((rows, D_MODEL), BF16)),
        grid=(rows // tr, D_MODEL // tk),
        in_specs=[pl.BlockSpec((tr, kw), lambda r, k: (r, 0)),
                  pl.BlockSpec((tr, kw), lambda r, k: (r, 0)),
                  pl.BlockSpec((None, kw, tk), lambda r, k: (layer, 0, k)),
                  pl.BlockSpec((None, kw, tk), lambda r, k: (layer, 0, k)),
                  pl.BlockSpec((tr, tk), lambda r, k: (r, COL_GA // tk + k)),
                  pl.BlockSpec((tr, tk), lambda r, k: (r, COL_GB // tk + k)),
                  pl.BlockSpec((None, tk, D_MODEL), lambda r, k: (layer, k, 0)),
                  pl.BlockSpec((tr, D_MODEL), lambda r, k: (r, 0)),
                  pl.BlockSpec((None, 1, D_MODEL), lambda r, k: (layer, 0, 0)),
                  pl.BlockSpec((None, 1, D_MODEL), lambda r, k: (layer, 0, 0))],
        out_specs=(pl.BlockSpec((tr, D_MODEL), lambda r, k: (r, 0)),
                   pl.BlockSpec((tr, D_MODEL), lambda r, k: (r, 0))),
        scratch_shapes=[pltpu.VMEM((tr, D_MODEL), F32)],
        compiler_params=_params(("parallel", "arbitrary")),
        name="merge_ln",
    )(oa, ob, wa, wb, y, y, wo, x, g, b)


def _gelu_tanh(x):
    return 0.5 * x * (1.0 + jnp.tanh(np.sqrt(2.0 / np.pi) * (x + 0.044715 * (x * x * x))))


HALO = 16


def _ffn_kernel(*refs, alpha, tr, seq_len, blocks_per_seq, prompt_mode):
    if prompt_mode:
        (hb_ref, halo_ref, w1a_ref, w1u_ref, cw_ref, cb_ref, w2_ref, h_ref, g_ref, b_ref,
         h2_ref, h2b_ref, a_ref, acc_sc) = refs
    else:
        (hb_ref, s1_ref, s2_ref, w1a_ref, w1u_ref, cw_ref, cb_ref, w2_ref, h_ref, g_ref, b_ref,
         h2_ref, h2b_ref, a_ref, acc_sc) = refs
    r = pl.program_id(0)
    ft = pl.program_id(1)

    @pl.when(ft == 0)
    def _():
        acc_sc[...] = jnp.zeros(acc_sc.shape, F32)

    hb = hb_ref[...]
    a = _dot(hb, w1a_ref[...])
    up = _dot(hb, w1u_ref[...])
    row = lax.broadcasted_iota(I32, a.shape, 0)
    p1 = pltpu.roll(a, 1, 0)
    p2 = pltpu.roll(a, 2, 0)
    if prompt_mode:
        a_halo = _dot(halo_ref[...], w1a_ref[...])
        keep = jnp.where(r % blocks_per_seq == 0, 0.0, 1.0)
        h6 = a_halo[HALO - 2:HALO - 1, :] * keep
        h7 = a_halo[HALO - 1:HALO, :] * keep
        p1 = jnp.where(row == 0, h7, p1)
        p2 = jnp.where(row == 0, h6, jnp.where(row == 1, h7, p2))
        a_ref[...] = a[tr - SUBLANES:, :]
    else:
        t = row & (seq_len - 1)
        p1 = jnp.where(t == 0, s1_ref[...], p1)
        p2 = jnp.where(t < 2, s2_ref[...], p2)
        a_ref[...] = a
    c = cb_ref[...] + cw_ref[0:1, :] * p2 + cw_ref[1:2, :] * p1 + cw_ref[2:3, :] * a
    hmid = (_gelu_tanh(c) * up).astype(BF16)
    acc_sc[...] += _dot(hmid, w2_ref[...])

    @pl.when(ft == pl.num_programs(1) - 1)
    def _():
        h2 = _layer_norm(alpha * h_ref[...] + acc_sc[...], g_ref[...], b_ref[...])
        h2_ref[...] = h2
        h2b_ref[...] = h2.astype(BF16)


def _ffn(h, hb, w1a, w1u, cw, cb, w2, g, b, layer, alpha, seq_len, state=None):
    rows = h.shape[0]
    prompt_mode = state is None
    tr = min(seq_len, 512) if prompt_mode else rows
    n_r = rows // tr
    n_f = D_FF_PAD // TF
    common_w = [pl.BlockSpec((None, D_MODEL, TF), lambda r, f: (layer, 0, f)),
                pl.BlockSpec((None, D_MODEL, TF), lambda r, f: (layer, 0, f)),
                pl.BlockSpec((None, SUBLANES, TF), lambda r, f: (layer, 0, f)),
                pl.BlockSpec((None, 1, TF), lambda r, f: (layer, 0, f)),
                pl.BlockSpec((None, TF, D_MODEL), lambda r, f: (layer, f, 0)),
                pl.BlockSpec((tr, D_MODEL), lambda r, f: (r, 0)),
                pl.BlockSpec((None, 1, D_MODEL), lambda r, f: (layer, 0, 0)),
                pl.BlockSpec((None, 1, D_MODEL), lambda r, f: (layer, 0, 0))]
    if prompt_mode:
        assert seq_len % tr == 0 and tr % HALO == 0
        per = tr // HALO
        extra_specs = [pl.BlockSpec((HALO, D_MODEL), lambda r, f: (jnp.maximum(r * per - 1, 0), 0))]
        extra = [hb]
        a_rows, a_blk = n_r * SUBLANES, SUBLANES
    else:
        assert seq_len & (seq_len - 1) == 0 and seq_len >= CONV_W - 1
        extra_specs = [pl.BlockSpec((tr, TF), lambda r, f: (0, f)),
                       pl.BlockSpec((tr, TF), lambda r, f: (0, f))]
        extra = list(state)
        a_rows, a_blk = rows, tr
    return pl.pallas_call(
        functools.partial(_ffn_kernel, alpha=alpha, tr=tr, seq_len=seq_len,
                          blocks_per_seq=max(seq_len // tr, 1), prompt_mode=prompt_mode),
        out_shape=(jax.ShapeDtypeStruct((rows, D_MODEL), F32), jax.ShapeDtypeStruct((rows, D_MODEL), BF16),
                   jax.ShapeDtypeStruct((a_rows, D_FF_PAD), F32)),
        grid=(n_r, n_f),
        in_specs=[pl.BlockSpec((tr, D_MODEL), lambda r, f: (r, 0))] + extra_specs + common_w,
        out_specs=(pl.BlockSpec((tr, D_MODEL), lambda r, f: (r, 0)),
                   pl.BlockSpec((tr, D_MODEL), lambda r, f: (r, 0)),
                   pl.BlockSpec((a_blk, TF), lambda r, f: (r, f))),
        scratch_shapes=[pltpu.VMEM((tr, D_MODEL), F32)],
        compiler_params=_params(("parallel", "arbitrary")),
        name="conv_ffn_ln",
    )(hb, *extra, w1a, w1u, cw, cb, w2, h, g, b)


def _ple_kernel(hb_ref, wg_ref, p_ref, wp_ref, h_ref, o_ref, ob_ref):
    gate = jax.nn.sigmoid(_dot(hb_ref[...], wg_ref[...]))
    out = h_ref[...] + gate * _dot(p_ref[...], wp_ref[...])
    o_ref[...] = out
    ob_ref[...] = out.astype(BF16)


def _ple(h2, h2b, pb, wg, wp, layer):
    rows = h2.shape[0]
    tr = min(rows, 1024)
    tn = 512
    return pl.pallas_call(
        _ple_kernel,
        out_shape=(jax.ShapeDtypeStruct((rows, D_MODEL), F32), jax.ShapeDtypeStruct((rows, D_MODEL), BF16)),
        grid=(rows // tr, D_MODEL // tn),
        in_specs=[pl.BlockSpec((tr, D_MODEL), lambda r, n: (r, 0)),
                  pl.BlockSpec((None, D_MODEL, tn), lambda r, n: (layer, 0, n)),
                  pl.BlockSpec((None, tr, PLE_DIM), lambda r, n: (layer, r, 0)),
                  pl.BlockSpec((None, PLE_DIM, tn), lambda r, n: (layer, 0, n)),
                  pl.BlockSpec((tr, tn), lambda r, n: (r, n))],
        out_specs=(pl.BlockSpec((tr, tn), lambda r, n: (r, n)),
                   pl.BlockSpec((tr, tn), lambda r, n: (r, n))),
        compiler_params=_params(("parallel", "arbitrary")),
        name="ple_gate",
    )(h2b, wg, pb, wp, h2)


IN_WIDTH = COL_KIW + IDX_DIM + IDX_HEADS + 2 * D_MODEL
PACK_ROWS = 128


SMALL_COLS = IDX_DIM + IDX_HEADS
KIW_TILE = COL_KIW // TN


def _pack_w_in_kernel(prev_ref, cur_ref, o_ref):
    n = pl.program_id(1)

    @pl.when(n < KIW_TILE)
    def _():
        o_ref[...] = cur_ref[...].T.astype(BF16)

    @pl.when(n == KIW_TILE)
    def _():
        t = jnp.concatenate([cur_ref[:SMALL_COLS, :], jnp.zeros((TN - SMALL_COLS, D_MODEL), F32)], axis=0)
        o_ref[...] = t.T.astype(BF16)

    @pl.when(n > KIW_TILE)
    def _():
        t = jnp.concatenate([prev_ref[SMALL_COLS:, :], cur_ref[:SMALL_COLS, :]], axis=0)
        o_ref[...] = t.T.astype(BF16)


def _pack_w_in(w_in):
    depth, rows, width = w_in.shape
    assert width == IN_WIDTH and rows == D_MODEL and SMALL_COLS % 16 == 0
    w_t = jnp.swapaxes(w_in, 1, 2)
    return pl.pallas_call(
        _pack_w_in_kernel,
        out_shape=jax.ShapeDtypeStruct((depth, rows, IN_COLS), BF16),
        grid=(depth, N_IN_TILES),
        in_specs=[pl.BlockSpec((None, TN, D_MODEL), lambda l, n: (l, jnp.maximum(n - 1, KIW_TILE), 0)),
                  pl.BlockSpec((None, TN, D_MODEL), lambda l, n: (l, n, 0))],
        out_specs=pl.BlockSpec((None, D_MODEL, TN), lambda l, n: (l, 0, n)),
        compiler_params=_params(("parallel", "arbitrary")),
        name="pack_w_in",
    )(w_t, w_t)


def _pack_ffn_in_kernel(w_ref, a_ref, u_ref):
    x = w_ref[...]
    zeros = jnp.zeros((x.shape[0], D_FF_PAD - D_FF), BF16)
    a_ref[:, :D_FF] = x[:, :D_FF].astype(BF16)
    a_ref[:, D_FF:] = zeros
    u_ref[:, :D_FF] = x[:, D_FF:].astype(BF16)
    u_ref[:, D_FF:] = zeros


def _pack_ffn_in(w_ffn_in):
    depth, rows, width = w_ffn_in.shape
    out = jax.ShapeDtypeStruct((depth, rows, D_FF_PAD), BF16)
    spec = pl.BlockSpec((None, PACK_ROWS, D_FF_PAD), lambda l, r: (l, r, 0))
    return pl.pallas_call(
        _pack_ffn_in_kernel,
        out_shape=(out, out),
        grid=(depth, rows // PACK_ROWS),
        in_specs=[pl.BlockSpec((None, PACK_ROWS, width), lambda l, r: (l, r, 0))],
        out_specs=(spec, spec),
        compiler_params=_params(("parallel", "parallel")),
        name="pack_ffn_in",
    )(w_ffn_in)


def _pack_ffn_out_kernel(w_ref, o_ref):
    row = pl.program_id(1) * TF + lax.broadcasted_iota(I32, w_ref.shape, 0)
    o_ref[...] = jnp.where(row < D_FF, w_ref[...], 0.0).astype(BF16)


def _pack_ffn_out(w_ffn_out):
    depth, _, cols = w_ffn_out.shape
    return pl.pallas_call(
        _pack_ffn_out_kernel,
        out_shape=jax.ShapeDtypeStruct((depth, D_FF_PAD, cols), BF16),
        grid=(depth, D_FF_PAD // TF),
        in_specs=[pl.BlockSpec((None, TF, cols), lambda l, r: (l, r, 0))],
        out_specs=pl.BlockSpec((None, TF, cols), lambda l, r: (l, r, 0)),
        compiler_params=_params(("parallel", "parallel")),
        name="pack_ffn_out",
    )(w_ffn_out)


def _rope_tables(pos):
    pos = pos.astype(F32)[:, None]

    def table(head_dim):
        rot = head_dim // 4
        half = rot // 2
        inv = ROPE_THETA ** (-(2.0 * jnp.arange(half, dtype=F32)) / rot)
        ang = pos * inv[None, :]
        cos, sin = jnp.cos(ang), jnp.sin(ang)
        ones = jnp.ones((pos.shape[0], head_dim - rot), F32)
        c = jnp.concatenate([cos, cos, ones], axis=1)
        s = jnp.concatenate([-sin, sin, 0.0 * ones], axis=1)
        reps = LANES // head_dim
        return jnp.tile(c, (1, reps)), jnp.tile(s, (1, reps))

    c128, s128 = table(HEAD_DIM)
    c64, s64 = table(IDX_DIM)
    return c128, s128, c64, s64


def kernel(x_prompt, x_sample, cache_sb_kv, cache_dsa_kv, cache_idx_k, state_ffn_conv, page_table,
           p_prompt, p_sample, w_in, w_branch_sb, w_branch_dsa, w_out, ln1_g, ln1_b, w_ffn_in,
           ffn_conv_w, ffn_conv_b, w_ffn_out, ln2_g, ln2_b, w_ple_gate, w_ple_proj):
    batch, seq = x_prompt.shape[:2]
    nseq, n_tok = x_sample.shape[:2]
    depth = w_in.shape[0]
    n_pool, page = cache_sb_kv.shape[1:3]
    n_pages = page_table.shape[1]
    past_len = n_pages * page
    alpha = (2 * depth) ** 0.25
    kv_w = 2 * N_KV * HEAD_DIM
    top_s = max(1, min(TOPK_MAX, (past_len + n_tok) // 4))
    pps = min(32, n_pages)
    pps_idx = min(64, n_pages)

    w_in_p = _pack_w_in(w_in)
    wa = w_branch_sb.astype(BF16)
    wb = w_branch_dsa.astype(BF16)
    wo = w_out.astype(BF16)
    ff_pad = D_FF_PAD - D_FF
    w1a, w1u = _pack_ffn_in(w_ffn_in)
    w2 = _pack_ffn_out(w_ffn_out)
    cw = jnp.pad(ffn_conv_w, ((0, 0), (0, SUBLANES - CONV_W), (0, ff_pad)))
    cb = jnp.pad(ffn_conv_b, ((0, 0), (0, ff_pad)))[:, None, :]
    wg = w_ple_gate.astype(BF16)
    wp = w_ple_proj.astype(BF16)
    g1, b1 = ln1_g[:, None, :], ln1_b[:, None, :]
    g2, b2 = ln2_g[:, None, :], ln2_b[:, None, :]

    tabs_p = _rope_tables(jnp.tile(jnp.arange(seq, dtype=jnp.int32), batch))
    tabs_s = _rope_tables(jnp.tile(past_len + jnp.arange(TOK_PAD, dtype=jnp.int32), nseq))

    sb_pages = cache_sb_kv.reshape(depth, n_pool, page * KV_SLOTS, HEAD_DIM)
    dsa_pages = cache_dsa_kv.reshape(depth, n_pool, page * KV_SLOTS, HEAD_DIM)
    idx_pages_t = jnp.swapaxes(cache_idx_k, 2, 3)

    assert n_tok <= TOK_PAD
    tok_pad = ((0, 0), (0, TOK_PAD - n_tok), (0, 0))
    xp = x_prompt.reshape(batch * seq, D_MODEL)
    xs = jnp.pad(x_sample, tok_pad).reshape(nseq * TOK_PAD, D_MODEL)
    xpb, xsb = xp.astype(BF16), xs.astype(BF16)
    ppb = p_prompt.reshape(depth, batch * seq, PLE_DIM).astype(BF16)
    psb = jnp.pad(p_sample, ((0, 0),) + tok_pad).reshape(depth, nseq * TOK_PAD, PLE_DIM).astype(BF16)
    st = jnp.pad(state_ffn_conv, ((0, 0), (0, 0), (0, 0), (0, ff_pad)))
    conv_s1 = jnp.repeat(st[:, :, 1], TOK_PAD, axis=1)
    conv_s2 = jnp.pad(st, ((0, 0), (0, 0), (0, TOK_PAD - (CONV_W - 1)), (0, 0))).reshape(
        depth, nseq * TOK_PAD, D_FF_PAD)

    outs = {k: [] for k in ("sb_p", "dsa_p", "idx_p", "conv_p", "sb_s", "dsa_s", "idx_s", "conv_s")}
    for l in range(depth):
        y = _project(xpb, w_in_p, l, tabs_p)
        oa = _sb_prompt(y, batch, seq)
        ob = _dsa_prompt(y, batch, seq)
        h, hb = _merge(oa, ob, y, xp, wa, wb, wo, g1, b1, l, alpha)
        h2, h2b, a_tail = _ffn(h, hb, w1a, w1u, cw, cb, w2, g2, b2, l, alpha, seq)
        xp, xpb = _ple(h2, h2b, ppb, wg, wp, l)
        outs["sb_p"].append(y[:, COL_KVA:COL_KVA + kv_w].reshape(batch, seq, 2, N_KV, HEAD_DIM))
        outs["dsa_p"].append(y[:, COL_KVB:COL_KVB + kv_w].reshape(batch, seq, 2, N_KV, HEAD_DIM))
        outs["idx_p"].append(y[:, COL_KIW:COL_KIW + IDX_DIM].reshape(batch, seq, IDX_DIM))
        tails = a_tail.reshape(batch, -1, SUBLANES, D_FF_PAD)[:, -1, SUBLANES - (CONV_W - 1):, :D_FF]
        outs["conv_p"].append(tails)

        ys = _project(xsb, w_in_p, l, tabs_s)
        oa_s = _sb_sample(page_table, ys, sb_pages, l, pps)
        bias = _idx_sample(page_table, ys, idx_pages_t, l, pps_idx, top_s, n_tok)
        ob_s = _dsa_sample(page_table, ys, bias, dsa_pages, l, pps)
        hs, hsb = _merge(oa_s, ob_s, ys, xs, wa, wb, wo, g1, b1, l, alpha)
        h2s, h2sb, a_s = _ffn(hs, hsb, w1a, w1u, cw, cb, w2, g2, b2, l, alpha,
                              TOK_PAD, state=(conv_s1[l], conv_s2[l]))
        xs, xsb = _ple(h2s, h2sb, psb, wg, wp, l)
        ys_tok = ys.reshape(nseq, TOK_PAD, IN_COLS)[:, :n_tok]
        outs["sb_s"].append(ys_tok[..., COL_KVA:COL_KVA + kv_w].reshape(nseq, n_tok, 2, N_KV, HEAD_DIM))
        outs["dsa_s"].append(ys_tok[..., COL_KVB:COL_KVB + kv_w].reshape(nseq, n_tok, 2, N_KV, HEAD_DIM))
        outs["idx_s"].append(ys_tok[..., COL_KIW:COL_KIW + IDX_DIM])
        outs["conv_s"].append(a_s.reshape(nseq, TOK_PAD, D_FF_PAD)[:, n_tok - (CONV_W - 1):n_tok, :D_FF])

    return (xp.reshape(batch, seq, D_MODEL), xs.reshape(nseq, TOK_PAD, D_MODEL)[:, :n_tok],
            jnp.stack(outs["sb_p"]), jnp.stack(outs["dsa_p"]), jnp.stack(outs["idx_p"]),
            jnp.stack(outs["conv_p"]), jnp.stack(outs["sb_s"]), jnp.stack(outs["dsa_s"]),
            jnp.stack(outs["idx_s"]), jnp.stack(outs["conv_s"]))
```

```python
import functools

import jax
import jax.numpy as jnp
import numpy as np
from jax import lax
from jax.experimental import pallas as pl
from jax.experimental.pallas import tpu as pltpu

F32 = jnp.float32
BF16 = jnp.bfloat16
I32 = jnp.int32

D_MODEL = 2048
HEAD_DIM = 128
N_HEADS = 8
N_KV = 2
GROUP = N_HEADS // N_KV
IDX_HEADS = 16
IDX_DIM = 64
TOPK_MAX = 256
ROPE_THETA = 500000.0
D_FF = 5504
CONV_W = 3
PLE_DIM = 256
LN_EPS = 1e-5

LANES = 128
SUBLANES = 8
VMEM_LIMIT = 56 * 1024 * 1024

TN = 512
COL_QA = 0
COL_KVA = 1024
COL_QB = 1536
COL_KVB = 2560
COL_QI = 3072
COL_KIW = 4096
COL_GA = 4608
COL_GB = 6656
IN_COLS = 8704
N_IN_TILES = IN_COLS // TN
D_FF_PAD = 5632
TF = 512

SB_Q_SCALE = (HEAD_DIM ** -0.5) * float(np.log2(np.e))
SB_STACK_ROWS = 64
NEG_BIG = -1e30
KEY_NEG_INF = np.int32(np.array(0xFF800000, dtype=np.uint32).view(np.int32) ^ 0x7FFFFFFF)
INT_MIN = np.int32(-2 ** 31)


def _params(sem):
    return pltpu.CompilerParams(dimension_semantics=sem, vmem_limit_bytes=VMEM_LIMIT)


def _dot_t(a, b):
    return lax.dot_general(a, b, (((1,), (1,)), ((), ())), preferred_element_type=F32)


def _dot(a, b):
    return jnp.dot(a, b, preferred_element_type=F32)


def _rope(y, c, s, half):
    w = y.shape[1]
    reps = w // LANES
    if reps > 1:
        c = jnp.concatenate([c] * reps, axis=1)
        s = jnp.concatenate([s] * reps, axis=1)
    lane = lax.broadcasted_iota(I32, y.shape, 1)
    first = (lane & (2 * half - 1)) < half
    partner = jnp.where(first, pltpu.roll(y, w - half, 1), pltpu.roll(y, half, 1))
    return y * c + partner * s


def _proj_kernel(x_ref, w_ref, c128_ref, s128_ref, c64_ref, s64_ref, sb_in, dsa_in, kiw_in,
                 y_ref, sb_ref, dsa_ref, kiw_ref):
    del sb_in, dsa_in, kiw_in
    n = pl.program_id(1)

    def product():
        return _dot(x_ref[...], w_ref[...])

    @pl.when(jnp.logical_and(n < COL_QB // TN, n != COL_KVA // TN))
    def _():
        y_ref[...] = product()

    @pl.when(n == COL_KVA // TN)
    def _():
        y = product()
        y_ref[...] = y
        sb_ref[...] = y

    @pl.when(jnp.logical_and(n >= COL_QB // TN, n < COL_KVB // TN))
    def _():
        y_ref[...] = _rope(product(), c128_ref[...], s128_ref[...], 16)

    @pl.when(n == COL_KVB // TN)
    def _():
        y = product()
        k_rot = _rope(y[:, :256], c128_ref[...], s128_ref[...], 16)
        y_ref[:, :256] = k_rot
        y_ref[:, 256:] = y[:, 256:]
        dsa_ref[:, :256] = k_rot
        dsa_ref[:, 256:] = y[:, 256:]

    @pl.when(jnp.logical_and(n >= COL_QI // TN, n < COL_KIW // TN))
    def _():
        y_ref[...] = _rope(product(), c64_ref[...], s64_ref[...], 8)

    @pl.when(n == COL_KIW // TN)
    def _():
        y = product()
        lane = lax.broadcasted_iota(I32, c64_ref.shape, 1)
        c = jnp.where(lane < IDX_DIM, c64_ref[...], 1.0)
        s = jnp.where(lane < IDX_DIM, s64_ref[...], 0.0)
        kiw = _rope(y[:, :LANES], c, s, 8)
        y_ref[:, :LANES] = kiw
        y_ref[:, LANES:] = y[:, LANES:]
        kiw_ref[...] = kiw

    @pl.when(n >= COL_GA // TN)
    def _():
        y_ref[...] = jax.nn.sigmoid(product())


def _project(xb, w, layer, tabs, new_rows):
    rows = xb.shape[0]
    tr = min(rows, 1024)
    tab_spec = pl.BlockSpec((tr, LANES), lambda r, n: (r, 0))
    any_spec = pl.BlockSpec(memory_space=pl.ANY)
    kv_spec = pl.BlockSpec((None, tr, TN), lambda r, n: (layer, r, 0))
    return pl.pallas_call(
        _proj_kernel,
        out_shape=(jax.ShapeDtypeStruct((rows, IN_COLS), F32),)
                  + tuple(jax.ShapeDtypeStruct(b.shape, b.dtype) for b in new_rows),
        grid=(rows // tr, N_IN_TILES),
        in_specs=[pl.BlockSpec((tr, D_MODEL), lambda r, n: (r, 0)),
                  pl.BlockSpec((None, D_MODEL, TN), lambda r, n: (layer, 0, n)),
                  tab_spec, tab_spec, tab_spec, tab_spec, any_spec, any_spec, any_spec],
        out_specs=(pl.BlockSpec((tr, TN), lambda r, n: (r, n)), kv_spec, kv_spec,
                   pl.BlockSpec((None, tr, LANES), lambda r, n: (layer, r, 0))),
        input_output_aliases={6: 1, 7: 2, 8: 3},
        compiler_params=_params(("parallel", "arbitrary")),
        name="in_proj",
    )(xb, w, *tabs, *new_rows)


def _suffix_matrix():
    j = lax.broadcasted_iota(I32, (2 * LANES, 2 * LANES), 0) & (LANES - 1)
    s = lax.broadcasted_iota(I32, (2 * LANES, 2 * LANES), 1)
    return jnp.where(jnp.logical_or(j > s, s >= LANES), -1.0, 0.0).astype(BF16)


def _sb_scores(q, k, mask):
    n_sub = k.shape[0] // LANES
    z = _dot_t(q, k)
    neg_abs = pltpu.bitcast(pltpu.bitcast(z, I32) | INT_MIN, F32)
    sp = jnp.maximum(z, 0.0) + jnp.log2(1.0 + jnp.exp2(neg_abs))
    spm = sp if mask is None else jnp.where(mask, sp, 0.0)
    hi = spm.astype(BF16)
    lo = (spm - hi.astype(F32)).astype(BF16)
    subs = [jnp.concatenate([hi[:, i * LANES:(i + 1) * LANES], lo[:, i * LANES:(i + 1) * LANES]], axis=1)
            for i in range(n_sub)]
    return z - sp, subs


def _sb_weights(d, subs, v, u, carry, mask):
    n_sub = len(subs)
    m_rows = d.shape[0]
    if m_rows <= SB_STACK_ROWS:
        r_all = _dot(subs[0] if n_sub == 1 else jnp.concatenate(subs, axis=0), u)
        rs = [r_all[i * m_rows:(i + 1) * m_rows] for i in range(n_sub)]
    else:
        rs = [_dot(s, u) for s in subs]
    afters = [None] * n_sub
    for i in reversed(range(n_sub)):
        afters[i] = rs[i][:, :LANES] + carry
        carry = carry + rs[i][:, LANES:]
    after = afters[0] if n_sub == 1 else jnp.concatenate(afters, axis=1)
    w = jnp.exp2(d + after)
    if mask is not None:
        w = jnp.where(mask, w, 0.0)
    return _dot(w.astype(BF16), v), carry


def _sb_block(q, k, v, u, carry, mask):
    d, subs = _sb_scores(q, k, mask)
    return _sb_weights(d, subs, v, u, carry, mask)


def _softmax_block(q, k, v, bias, m_old, l_old, acc_old):
    logit = _dot_t(q, k) + bias
    m_new = jnp.maximum(m_old, jnp.max(logit, axis=1, keepdims=True))
    p = jnp.exp(logit - m_new[:, 0:1])
    alpha = jnp.exp(m_old - m_new)
    l_new = alpha * l_old + jnp.sum(p, axis=1, keepdims=True)
    acc_new = alpha[:, 0:1] * acc_old + _dot(p.astype(BF16), v)
    return m_new, l_new, acc_new


def _sortable(x):
    b = pltpu.bitcast(x, I32)
    return jnp.where(b < 0, b ^ jnp.int32(0x7FFFFFFF), b)


def _kth_largest(count_ge, shape, k, n_total):
    bits_per_check = 4

    def cond(state):
        i, _, cnt = state
        return jnp.logical_and(i < 32, jnp.max(cnt) > k)

    def body(state):
        i, t, cnt = state
        for _ in range(bits_per_check):
            cand = t + jnp.left_shift(jnp.int32(1), jnp.int32(31) - i)
            c = count_ge(cand)
            ok = c >= k
            i, t, cnt = i + 1, jnp.where(ok, cand, t), jnp.where(ok, c, cnt)
        return i, t, cnt

    state = (jnp.int32(0), jnp.full(shape, INT_MIN, I32), jnp.full(shape, n_total, I32))
    _, t, cnt = lax.while_loop(cond, body, state)
    return t, cnt


def _tie_cutoff(count_tie_below, shape, need, n_bits):
    def body(i, j):
        cand = j + jnp.left_shift(jnp.int32(1), jnp.int32(n_bits - 1) - i)
        return jnp.where(count_tie_below(cand) < need, cand, j)

    return lax.fori_loop(0, n_bits, body, jnp.zeros(shape, I32))


def _sb_prompt_kernel(q_ref, k_ref, v_ref, o_ref, kb_sc, vb_sc, carry_sc, acc_sc, *, tq):
    iq = pl.program_id(1)

    @pl.when(iq == 0)
    def _():
        kb_sc[...] = k_ref[...].astype(BF16)
        vb_sc[...] = v_ref[...].astype(BF16)

    scale = SB_Q_SCALE
    u = _suffix_matrix()
    n_chain = N_HEADS // 2
    c_rows = 2 * tq
    qs = []
    for c in range(n_chain):
        q = jnp.concatenate([q_ref[:, h * HEAD_DIM:(h + 1) * HEAD_DIM] for h in (2 * c, 2 * c + 1)], axis=0)
        qs.append((q * scale).astype(BF16))

    def kv_block(o2, c):
        g = (2 * c) // GROUP
        return (kb_sc[pl.ds(o2, tq), g * HEAD_DIM:(g + 1) * HEAD_DIM],
                vb_sc[pl.ds(o2, tq), g * HEAD_DIM:(g + 1) * HEAD_DIM])

    row = lax.broadcasted_iota(I32, (c_rows, tq), 0) & (tq - 1)
    col = lax.broadcasted_iota(I32, (c_rows, tq), 1)
    off = pl.multiple_of(iq * tq, tq)
    for c in range(n_chain):
        k, v = kv_block(off, c)
        contrib, carry = _sb_block(qs[c], k, v, u, jnp.zeros((c_rows, LANES), F32), col < row)
        acc_sc[c] = contrib
        carry_sc[c] = carry

    def body(i, _):
        o2 = pl.multiple_of((iq - 1 - i) * tq, tq)
        for c in range(n_chain):
            k, v = kv_block(o2, c)
            contrib, carry = _sb_block(qs[c], k, v, u, carry_sc[c], None)
            acc_sc[c] += contrib
            carry_sc[c] = carry
        return 0

    lax.fori_loop(0, iq, body, 0)
    for h in range(N_HEADS):
        r0 = (h % 2) * tq
        o_ref[:, h * HEAD_DIM:(h + 1) * HEAD_DIM] = acc_sc[h // 2, r0:r0 + tq, :].astype(o_ref.dtype)


def _sb_prompt(y, batch, seq):
    tq = min(2 * LANES, seq)
    nq = seq // tq
    qw = N_HEADS * HEAD_DIM
    kvw = N_KV * HEAD_DIM
    return pl.pallas_call(
        functools.partial(_sb_prompt_kernel, tq=tq),
        out_shape=jax.ShapeDtypeStruct((batch * seq, qw), BF16),
        grid=(batch, nq),
        in_specs=[pl.BlockSpec((tq, qw), lambda b, i: (b * nq + i, COL_QA // qw)),
                  pl.BlockSpec((seq, kvw), lambda b, i: (b, COL_KVA // kvw)),
                  pl.BlockSpec((seq, kvw), lambda b, i: (b, COL_KVA // kvw + 1))],
        out_specs=pl.BlockSpec((tq, qw), lambda b, i: (b * nq + i, 0)),
        scratch_shapes=[pltpu.VMEM((seq, kvw), BF16), pltpu.VMEM((seq, kvw), BF16),
                        pltpu.VMEM((N_HEADS // 2, 2 * tq, LANES), F32),
                        pltpu.VMEM((N_HEADS // 2, 2 * tq, LANES), F32)],
        compiler_params=_params(("parallel", "arbitrary")),
        name="sb_prompt",
    )(y, y, y)


TKI = 256
LOGIT_SAFE = 60.0
NORM_SLACK = 1.02


def _fold_rows(x, op):
    return op(x.reshape(x.shape[0] // SUBLANES, SUBLANES, x.shape[1]), axis=0)


def _dsa_prompt_kernel(qb0_ref, qb1_ref, k_ref, v_ref, qi_ref, kiw_all_ref, kiw_q_ref, o_ref,
                       kb_sc, vb_sc, ki2_sc, key_sc, bias_sc, cut_sc, kn_sc, m_sc, l_sc, acc_sc,
                       *, tq, n_top, idx_bits):
    iq = pl.program_id(1)
    seq = k_ref.shape[0]

    @pl.when(iq == 0)
    def _():
        kb_sc[...] = k_ref[...].astype(BF16)
        vb_sc[...] = v_ref[...].astype(BF16)
        k_sq = k_ref[...] * k_ref[...]
        k_sq_max = jnp.float32(0.0)
        for g in range(N_KV):
            norms = jnp.sum(k_sq[:, g * HEAD_DIM:(g + 1) * HEAD_DIM], axis=1, keepdims=True)
            k_sq_max = jnp.maximum(k_sq_max, jnp.max(norms))
        kn_sc[...] = jnp.zeros(kn_sc.shape, F32) + k_sq_max
        lane = lax.broadcasted_iota(I32, (seq, LANES), 1)
        kia = jnp.where(lane < IDX_DIM, kiw_all_ref[...], 0.0)
        ki2_sc[0] = kia.astype(BF16)
        ki2_sc[1] = pltpu.roll(kia, IDX_DIM, 1).astype(BF16)

    w_t = (kiw_q_ref[...] * ((IDX_DIM ** -0.5) * (IDX_HEADS ** -0.5))).T
    w_rows = [w_t[IDX_DIM + h:IDX_DIM + h + 1, :] for h in range(IDX_HEADS)]

    q_pairs = [qi_ref[:, p * LANES:(p + 1) * LANES].astype(BF16) for p in range(IDX_HEADS // 2)]
    q_pos = iq * tq + lax.broadcasted_iota(I32, (1, tq), 1)
    n_blk = (iq * tq + tq + TKI - 1) // TKI

    def idx_body(j, _):
        off = pl.multiple_of(j * TKI, TKI)
        ki_e = ki2_sc[0, pl.ds(off, TKI), :]
        ki_o = ki2_sc[1, pl.ds(off, TKI), :]
        acc = jnp.zeros((TKI, tq), F32)
        for p in range(IDX_HEADS // 2):
            acc = acc + jnp.maximum(_dot_t(ki_e, q_pairs[p]), 0.0) * w_rows[2 * p]
            acc = acc + jnp.maximum(_dot_t(ki_o, q_pairs[p]), 0.0) * w_rows[2 * p + 1]
        k_pos = off + lax.broadcasted_iota(I32, (TKI, 1), 0)
        key_sc[pl.ds(off, TKI), :] = jnp.where(k_pos <= q_pos, _sortable(acc), KEY_NEG_INF)
        return 0

    lax.fori_loop(0, n_blk, idx_body, 0)

    def query_counts(hit_fn):
        def body(j, c):
            off = pl.multiple_of(j * TKI, TKI)
            return c + _fold_rows(hit_fn(key_sc[pl.ds(off, TKI), :], off), jnp.sum)

        c = lax.fori_loop(0, n_blk, body, jnp.zeros((SUBLANES, tq), I32))
        return jnp.sum(c, axis=0, keepdims=True)

    def count_ge(t):
        return query_counts(lambda kb, off: jnp.where(kb >= t, 1, 0))

    thr, n_ge = _kth_largest(count_ge, (1, tq), n_top, n_blk * TKI)

    cut_sc[...] = jnp.full(cut_sc.shape, seq, I32)

    @pl.when(jnp.max(n_ge) > n_top)
    def _():
        need = n_top - count_ge(thr + 1)

        def count_tie_below(jc):
            def hit(kb, off):
                idx = off + lax.broadcasted_iota(I32, kb.shape, 0)
                return jnp.where(kb == thr, jnp.where(idx < jc, 1, 0), 0)
            return query_counts(hit)

        cut = _tie_cutoff(count_tie_below, (1, tq), need, idx_bits)
        cut_sc[...] = jnp.broadcast_to(cut, cut_sc.shape)

    cut = cut_sc[0:1, :]

    def bias_body(j, _):
        off = pl.multiple_of(j * TKI, TKI)
        kb = key_sc[pl.ds(off, TKI), :]
        idx = off + lax.broadcasted_iota(I32, (TKI, tq), 0)
        tie = jnp.where(kb == thr, jnp.where(idx <= cut, 0.0, -jnp.inf), -jnp.inf)
        sel = jnp.where(kb > thr, 0.0, tie)
        sel = jnp.where(kb > KEY_NEG_INF, sel, -jnp.inf)
        bias_sc[:, pl.ds(off, TKI)] = sel.T
        return 0

    lax.fori_loop(0, n_blk, bias_body, 0)

    scale = (HEAD_DIM ** -0.5) * np.log2(np.e)
    n_chain = N_HEADS // 2
    n_sub = TKI // LANES
    qs = []
    ones = jnp.ones((HEAD_DIM, LANES), BF16)
    q_sq = jnp.zeros((SUBLANES, LANES), F32)
    for c in range(n_chain):
        q_ref = (qb0_ref, qb1_ref)[c // 2]
        hs = (2 * (c % 2), 2 * (c % 2) + 1)
        q = jnp.concatenate([q_ref[:, h * HEAD_DIM:(h + 1) * HEAD_DIM] for h in hs], axis=0) * scale
        q_sq = jnp.maximum(q_sq, _fold_rows(_dot((q * q).astype(BF16), ones), jnp.max))
        qs.append(q.astype(BF16))
    q_sq_max = jnp.max(q_sq) * NORM_SLACK

    def bias_rows(off):
        b = bias_sc[:, pl.ds(off, TKI)]
        return jnp.concatenate([b, b], axis=0)

    small_logits = q_sq_max * jnp.max(kn_sc[...]) <= LOGIT_SAFE * LOGIT_SAFE

    @pl.when(small_logits)
    def _():
        m_sc[...] = jnp.zeros(m_sc.shape, F32)

    @pl.when(jnp.logical_not(small_logits))
    def _():
        m_sc[...] = jnp.full(m_sc.shape, -jnp.inf, F32)

        def max_body(j, _):
            off = pl.multiple_of(j * TKI, TKI)
            bias = bias_rows(off)
            for c in range(n_chain):
                g = c // 2
                lg = _dot_t(qs[c], kb_sc[pl.ds(off, TKI), g * HEAD_DIM:(g + 1) * HEAD_DIM]) + bias
                m = m_sc[c]
                for i in range(n_sub):
                    m = jnp.maximum(m, lg[:, i * LANES:(i + 1) * LANES])
                m_sc[c] = m
            return 0

        lax.fori_loop(0, n_blk, max_body, 0)
        for c in range(n_chain):
            m_sc[c] = jnp.broadcast_to(jnp.max(m_sc[c], axis=1, keepdims=True), (2 * tq, LANES))

    l_sc[...] = jnp.zeros(l_sc.shape, F32)
    acc_sc[...] = jnp.zeros(acc_sc.shape, F32)

    def sum_body(j, _):
        off = pl.multiple_of(j * TKI, TKI)
        bias = bias_rows(off)
        for c in range(n_chain):
            g = c // 2
            lg = _dot_t(qs[c], kb_sc[pl.ds(off, TKI), g * HEAD_DIM:(g + 1) * HEAD_DIM]) + bias
            p = jnp.exp2(lg - jnp.concatenate([m_sc[c]] * n_sub, axis=1))
            l = l_sc[c]
            for i in range(n_sub):
                l = l + p[:, i * LANES:(i + 1) * LANES]
            l_sc[c] = l
            acc_sc[c] += _dot(p.astype(BF16), vb_sc[pl.ds(off, TKI), g * HEAD_DIM:(g + 1) * HEAD_DIM])
        return 0

    lax.fori_loop(0, n_blk, sum_body, 0)
    for c in range(n_chain):
        out = acc_sc[c] / jnp.sum(l_sc[c], axis=1, keepdims=True)
        for hh in range(2):
            c0 = (2 * c + hh) * HEAD_DIM
            o_ref[:, c0:c0 + HEAD_DIM] = out[hh * tq:(hh + 1) * tq, :].astype(o_ref.dtype)


def _dsa_prompt(y, batch, seq):
    tq = min(2 * LANES, seq)
    assert tq == TKI or seq == tq
    nq = seq // tq
    gw = GROUP * HEAD_DIM
    kvw = N_KV * HEAD_DIM
    n_top = max(1, min(TOPK_MAX, seq // 4))
    idx_bits = int(seq).bit_length()
    q_rows = 2 * tq
    return pl.pallas_call(
        functools.partial(_dsa_prompt_kernel, tq=tq, n_top=n_top, idx_bits=idx_bits),
        out_shape=jax.ShapeDtypeStruct((batch * seq, N_HEADS * HEAD_DIM), BF16),
        grid=(batch, nq),
        in_specs=[pl.BlockSpec((tq, gw), lambda b, i: (b * nq + i, COL_QB // gw)),
                  pl.BlockSpec((tq, gw), lambda b, i: (b * nq + i, COL_QB // gw + 1)),
                  pl.BlockSpec((seq, kvw), lambda b, i: (b, COL_KVB // kvw)),
                  pl.BlockSpec((seq, kvw), lambda b, i: (b, COL_KVB // kvw + 1)),
                  pl.BlockSpec((tq, IDX_HEADS * IDX_DIM), lambda b, i: (b * nq + i, COL_QI // (IDX_HEADS * IDX_DIM))),
                  pl.BlockSpec((seq, LANES), lambda b, i: (b, COL_KIW // LANES)),
                  pl.BlockSpec((tq, LANES), lambda b, i: (b * nq + i, COL_KIW // LANES))],
        out_specs=pl.BlockSpec((tq, N_HEADS * HEAD_DIM), lambda b, i: (b * nq + i, 0)),
        scratch_shapes=[pltpu.VMEM((seq, kvw), BF16),
                        pltpu.VMEM((seq, kvw), BF16),
                        pltpu.VMEM((2, seq, LANES), BF16),
                        pltpu.VMEM((seq, tq), I32),
                        pltpu.VMEM((tq, seq), F32),
                        pltpu.VMEM((SUBLANES, tq), I32),
                        pltpu.VMEM((SUBLANES, LANES), F32),
                        pltpu.VMEM((N_HEADS // 2, q_rows, LANES), F32),
                        pltpu.VMEM((N_HEADS // 2, q_rows, LANES), F32),
                        pltpu.VMEM((N_HEADS // 2, q_rows, HEAD_DIM), F32)],
        compiler_params=_params(("parallel", "arbitrary")),
        name="dsa_prompt",
    )(y, y, y, y, y, y, y)


TOK_PAD = SUBLANES
S_ROWS = GROUP * TOK_PAD


KV_SLOTS = 2 * N_KV


def _page_rows(ref, slot, page):
    return ref[pl.ds(slot, page, stride=KV_SLOTS), :].astype(BF16)


J_ROWS = N_KV * S_ROWS
J_LANES = N_KV * HEAD_DIM


def _gather_kv(page_refs, page):
    k = [jnp.concatenate([_page_rows(r, g, page) for g in range(N_KV)], axis=1) for r in page_refs]
    v = [jnp.concatenate([_page_rows(r, N_KV + g, page) for g in range(N_KV)], axis=1) for r in page_refs]
    if len(page_refs) == 1:
        return k[0], v[0]
    return jnp.concatenate(k, axis=0), jnp.concatenate(v, axis=0)


def _joint_queries(q_blks, scale):
    rows = []
    for g, q_blk in enumerate(q_blks):
        q = jnp.concatenate([q_blk[:, h * HEAD_DIM:(h + 1) * HEAD_DIM] for h in range(GROUP)], axis=0) * scale
        zero = jnp.zeros_like(q)
        rows.append(jnp.concatenate([q if gg == g else zero for gg in range(N_KV)], axis=1))
    return jnp.concatenate(rows, axis=0).astype(BF16)


def _new_token_kv(new_ref, page):
    pad = jnp.zeros((page - TOK_PAD, J_LANES), F32)
    k = jnp.concatenate([new_ref[:, :J_LANES], pad], axis=0)
    v = jnp.concatenate([new_ref[:, J_LANES:], pad], axis=0)
    return k.astype(BF16), v.astype(BF16)


def _unstack_heads(o_ref, acc):
    for g in range(N_KV):
        for h in range(GROUP):
            r0 = g * S_ROWS + h * TOK_PAD
            c0 = (g * GROUP + h) * HEAD_DIM
            o_ref[:, c0:c0 + HEAD_DIM] = acc[r0:r0 + TOK_PAD, g * HEAD_DIM:(g + 1) * HEAD_DIM]


def _sb_sample_kernel(pt_ref, q_ref, new_ref, *rest, pps, page):
    page_refs = rest[:pps]
    o_ref = rest[pps]
    carry_sc, acc_sc = rest[pps + 1:]
    j = pl.program_id(1)
    u = _suffix_matrix()
    gw = GROUP * HEAD_DIM
    q = _joint_queries([q_ref[:, g * gw:(g + 1) * gw] for g in range(N_KV)], SB_Q_SCALE)

    def visit(k, v, mask):
        c, cr = _sb_block(q, k, v, u, carry_sc[...], mask)
        acc_sc[...] += c
        carry_sc[...] = cr

    @pl.when(j == 0)
    def _():
        carry_sc[...] = jnp.zeros(carry_sc.shape, F32)
        acc_sc[...] = jnp.zeros(acc_sc.shape, F32)
        tok = lax.broadcasted_iota(I32, (J_ROWS, page), 0) & (TOK_PAD - 1)
        col = lax.broadcasted_iota(I32, (J_ROWS, page), 1)
        visit(*_new_token_kv(new_ref, page), col < tok)

    visit(*_gather_kv(page_refs, page), None)

    @pl.when(j == pl.num_programs(1) - 1)
    def _():
        _unstack_heads(o_ref, acc_sc[...])


def _sb_sample(page_table, ys, cache, layer, pps):
    nseq, n_pages = page_table.shape
    rows = cache.shape[2]
    page = rows // KV_SLOTS
    n_steps = n_pages // pps
    qw = N_HEADS * HEAD_DIM
    kvw = KV_SLOTS * HEAD_DIM

    def page_spec(i):
        return pl.BlockSpec((None, None, rows, HEAD_DIM),
                            lambda b, j, pt, i=i: (layer, pt[b, (n_steps - 1 - j) * pps + i], 0, 0))

    grid_spec = pltpu.PrefetchScalarGridSpec(
        num_scalar_prefetch=1,
        grid=(nseq, n_steps),
        in_specs=[pl.BlockSpec((TOK_PAD, qw), lambda b, j, pt: (b, COL_QA // qw)),
                  pl.BlockSpec((TOK_PAD, kvw), lambda b, j, pt: (b, COL_KVA // kvw))]
                 + [page_spec(i) for i in range(pps)],
        out_specs=pl.BlockSpec((TOK_PAD, qw), lambda b, j, pt: (b, 0)),
        scratch_shapes=[pltpu.VMEM((J_ROWS, LANES), F32), pltpu.VMEM((J_ROWS, J_LANES), F32)],
    )
    return pl.pallas_call(
        functools.partial(_sb_sample_kernel, pps=pps, page=page),
        out_shape=jax.ShapeDtypeStruct((nseq * TOK_PAD, qw), F32),
        grid_spec=grid_spec,
        compiler_params=_params(("parallel", "arbitrary")),
        name="sb_sample",
    )(page_table, ys, ys, *([cache] * pps))


def _idx_sample_kernel(pt_ref, qi_ref, kiw_ref, *rest, pps, page, n_pages, n_top, idx_bits, n_tok):
    page_refs = rest[:pps]
    bias_ref = rest[pps]
    score_sc = rest[pps + 1]
    j = pl.program_id(1)

    row = lax.broadcasted_iota(I32, (TOK_PAD, 1), 0)

    def real_rows(x):
        return jnp.where(row < n_tok, x, pltpu.roll(x, n_tok, 0))

    qi_rows = real_rows(qi_ref[...])
    kiw = real_rows(kiw_ref[...])
    qi = jnp.concatenate([qi_rows[:, h * IDX_DIM:(h + 1) * IDX_DIM] for h in range(IDX_HEADS)],
                         axis=0).astype(BF16)
    w_scale = (IDX_DIM ** -0.5) * (IDX_HEADS ** -0.5)
    wm = jnp.concatenate([jnp.broadcast_to(kiw[:, IDX_DIM + h:IDX_DIM + h + 1] * w_scale, (TOK_PAD, LANES))
                          for h in range(IDX_HEADS)], axis=0)

    def head_sum(s):
        n = s.shape[1]
        w = wm if n == LANES else jnp.concatenate([wm] * (n // LANES), axis=1)
        return jnp.sum((jnp.maximum(s, 0.0) * w).reshape(IDX_HEADS, TOK_PAD, n), axis=0)

    @pl.when(j == 0)
    def _():
        ki_new = jnp.concatenate([kiw_ref[:, :IDX_DIM], jnp.zeros((page - TOK_PAD, IDX_DIM), F32)], axis=0)
        tok = lax.broadcasted_iota(I32, (TOK_PAD, page), 0) & (n_tok - 1)
        col = lax.broadcasted_iota(I32, (TOK_PAD, page), 1)
        s_new = head_sum(_dot_t(qi, ki_new.astype(BF16)))
        score_sc[:, n_pages * page:] = jnp.where(col <= tok, s_new, -jnp.inf)

    off = pl.multiple_of(j * (pps * page), pps * page)
    ki_t = jnp.concatenate([r[...] for r in page_refs], axis=1)
    score_sc[:, pl.ds(off, pps * page)] = head_sum(_dot(qi, ki_t.astype(BF16)))

    @pl.when(j == pl.num_programs(1) - 1)
    def _():
        key = _sortable(score_sc[...])
        idx = lax.broadcasted_iota(I32, key.shape, 1)

        def count_ge(t):
            return jnp.sum(jnp.where(key >= t, 1, 0), axis=1, keepdims=True)

        thr, n_ge = _kth_largest(count_ge, (TOK_PAD, 1), n_top, key.shape[1])

        def search_cut():
            need = n_top - count_ge(thr + 1)

            def count_tie_below(jc):
                return jnp.sum(jnp.where(key == thr, jnp.where(idx < jc, 1, 0), 0), axis=1, keepdims=True)

            return _tie_cutoff(count_tie_below, (TOK_PAD, 1), need, idx_bits)

        cut = lax.cond(jnp.max(n_ge) > n_top, search_cut,
                       lambda: jnp.full((TOK_PAD, 1), key.shape[1], I32))
        tie = jnp.where(key == thr, jnp.where(idx <= cut, 0.0, -jnp.inf), -jnp.inf)
        sel = jnp.where(key > thr, 0.0, tie)
        bias_ref[0] = jnp.where(key > KEY_NEG_INF, sel, -jnp.inf)


def _idx_sample(page_table, ys, cache_t, layer, pps, n_top, n_tok):
    assert n_tok & (n_tok - 1) == 0 and TOK_PAD % n_tok == 0
    nseq, n_pages = page_table.shape
    page = cache_t.shape[3]
    n_cols = (n_pages + 1) * page
    cache = cache_t
    qiw = IDX_HEADS * IDX_DIM

    def page_spec(i):
        return pl.BlockSpec((None, None, IDX_DIM, page),
                            lambda b, j, pt, i=i: (layer, pt[b, j * pps + i], 0, 0))

    grid_spec = pltpu.PrefetchScalarGridSpec(
        num_scalar_prefetch=1,
        grid=(nseq, n_pages // pps),
        in_specs=[pl.BlockSpec((TOK_PAD, qiw), lambda b, j, pt: (b, COL_QI // qiw)),
                  pl.BlockSpec((TOK_PAD, LANES), lambda b, j, pt: (b, COL_KIW // LANES))]
                 + [page_spec(i) for i in range(pps)],
        out_specs=pl.BlockSpec((1, TOK_PAD, n_cols), lambda b, j, pt: (b, 0, 0)),
        scratch_shapes=[pltpu.VMEM((TOK_PAD, n_cols), F32)],
    )
    return pl.pallas_call(
        functools.partial(_idx_sample_kernel, pps=pps, page=page, n_pages=n_pages, n_top=n_top,
                          idx_bits=int(n_cols).bit_length(), n_tok=n_tok),
        out_shape=jax.ShapeDtypeStruct((nseq, TOK_PAD, n_cols), F32),
        grid_spec=grid_spec,
        compiler_params=_params(("parallel", "arbitrary")),
        name="idx_sample",
    )(page_table, ys, ys, *([cache] * pps))


def _dsa_sample_kernel(pt_ref, q0_ref, q1_ref, new_ref, bias_ref, *rest, pps, page, n_pages):
    page_refs = rest[:pps]
    o_ref = rest[pps]
    m_sc, l_sc, acc_sc = rest[pps + 1:]
    j = pl.program_id(1)
    q = _joint_queries([q0_ref[...], q1_ref[...]], HEAD_DIM ** -0.5)

    def visit(k, v, b8):
        bias = jnp.concatenate([b8] * (J_ROWS // TOK_PAD), axis=0)
        m, l, a = _softmax_block(q, k, v, bias, m_sc[...], l_sc[...], acc_sc[...])
        m_sc[...] = m
        l_sc[...] = l
        acc_sc[...] = a

    @pl.when(j == 0)
    def _():
        m_sc[...] = jnp.full(m_sc.shape, NEG_BIG, F32)
        l_sc[...] = jnp.zeros(l_sc.shape, F32)
        acc_sc[...] = jnp.zeros(acc_sc.shape, F32)
        visit(*_new_token_kv(new_ref, page), bias_ref[0, :, n_pages * page:])

    off = pl.multiple_of(j * (pps * page), pps * page)
    visit(*_gather_kv(page_refs, page), bias_ref[0, :, pl.ds(off, pps * page)])

    @pl.when(j == pl.num_programs(1) - 1)
    def _():
        _unstack_heads(o_ref, acc_sc[...] / l_sc[:, 0:1])


def _dsa_sample(page_table, ys, bias, cache, layer, pps):
    nseq, n_pages = page_table.shape
    rows = cache.shape[2]
    page = rows // KV_SLOTS
    n_cols = bias.shape[2]
    gw = GROUP * HEAD_DIM
    kvw = KV_SLOTS * HEAD_DIM

    def page_spec(i):
        return pl.BlockSpec((None, None, rows, HEAD_DIM),
                            lambda b, j, pt, i=i: (layer, pt[b, j * pps + i], 0, 0))

    grid_spec = pltpu.PrefetchScalarGridSpec(
        num_scalar_prefetch=1,
        grid=(nseq, n_pages // pps),
        in_specs=[pl.BlockSpec((TOK_PAD, gw), lambda b, j, pt: (b, COL_QB // gw)),
                  pl.BlockSpec((TOK_PAD, gw), lambda b, j, pt: (b, COL_QB // gw + 1)),
                  pl.BlockSpec((TOK_PAD, kvw), lambda b, j, pt: (b, COL_KVB // kvw)),
                  pl.BlockSpec((1, TOK_PAD, n_cols), lambda b, j, pt: (b, 0, 0))]
                 + [page_spec(i) for i in range(pps)],
        out_specs=pl.BlockSpec((TOK_PAD, N_HEADS * HEAD_DIM), lambda b, j, pt: (b, 0)),
        scratch_shapes=[pltpu.VMEM((J_ROWS, LANES), F32), pltpu.VMEM((J_ROWS, LANES), F32),
                        pltpu.VMEM((J_ROWS, J_LANES), F32)],
    )
    return pl.pallas_call(
        functools.partial(_dsa_sample_kernel, pps=pps, page=page, n_pages=n_pages),
        out_shape=jax.ShapeDtypeStruct((nseq * TOK_PAD, N_HEADS * HEAD_DIM), F32),
        grid_spec=grid_spec,
        compiler_params=_params(("parallel", "arbitrary")),
        name="dsa_sample",
    )(page_table, ys, ys, ys, bias, *([cache] * pps))


def _layer_norm(x, g, b):
    mu = jnp.mean(x, axis=-1, keepdims=True)
    xc = x - mu
    var = jnp.mean(xc * xc, axis=-1, keepdims=True)
    return xc * lax.rsqrt(var + LN_EPS) * g + b


def _merge_kernel(oa_ref, ob_ref, wa_ref, wb_ref, ga_ref, gb_ref, wo_ref, x_ref, g_ref, b_ref,
                  h_ref, hb_ref, acc_sc, *, alpha):
    kt = pl.program_id(1)

    @pl.when(kt == 0)
    def _():
        acc_sc[...] = jnp.zeros(acc_sc.shape, F32)

    oa = oa_ref[...].astype(BF16)
    ob = ob_ref[...].astype(BF16)
    mix = ga_ref[...] * _dot(oa, wa_ref[...]) + gb_ref[...] * _dot(ob, wb_ref[...])
    acc_sc[...] += _dot(mix.astype(BF16), wo_ref[...])

    @pl.when(kt == pl.num_programs(1) - 1)
    def _():
        h = _layer_norm(alpha * x_ref[...] + acc_sc[...], g_ref[...], b_ref[...])
        h_ref[...] = h
        hb_ref[...] = h.astype(BF16)


def _merge(oa, ob, y, x, wa, wb, wo, g, b, layer, alpha):
    rows = x.shape[0]
    tr = min(rows, 512)
    tk = 512
    kw = N_HEADS * HEAD_DIM
    return pl.pallas_call(
        functools.partial(_merge_kernel, alpha=alpha),
        out_shape=(jax.ShapeDtypeStruct((rows, D_MODEL), F32), jax.ShapeDtypeStruct((rows, D_MODEL), BF16)),
        grid=(rows // tr, D_MODEL // tk),
        in_specs=[pl.BlockSpec((tr, kw), lambda r, k: (r, 0)),
                  pl.BlockSpec((tr, kw), lambda r, k: (r, 0)),
                  pl.BlockSpec((None, kw, tk), lambda r, k: (layer, 0, k)),
                  pl.BlockSpec((None, kw, tk), lambda r, k: (layer, 0, k)),
                  pl.BlockSpec((tr, tk), lambda r, k: (r, COL_GA // tk + k)),
                  pl.BlockSpec((tr, tk), lambda r, k: (r, COL_GB // tk + k)),
                  pl.BlockSpec((None, tk, D_MODEL), lambda r, k: (layer, k, 0)),
                  pl.BlockSpec((tr, D_MODEL), lambda r, k: (r, 0)),
                  pl.BlockSpec((None, 1, D_MODEL), lambda r, k: (layer, 0, 0)),
                  pl.BlockSpec((None, 1, D_MODEL), lambda r, k: (layer, 0, 0))],
        out_specs=(pl.BlockSpec((tr, D_MODEL), lambda r, k: (r, 0)),
                   pl.BlockSpec((tr, D_MODEL), lambda r, k: (r, 0))),
        scratch_shapes=[pltpu.VMEM((tr, D_MODEL), F32)],
        compiler_params=_params(("parallel", "arbitrary")),
        name="merge_ln",
    )(oa, ob, wa, wb, y, y, wo, x, g, b)


def _gelu_tanh(x):
    return 0.5 * x * (1.0 + jnp.tanh(np.sqrt(2.0 / np.pi) * (x + 0.044715 * (x * x * x))))


HALO = 16


def _ffn_kernel(*refs, alpha, tr, seq_len, blocks_per_seq, prompt_mode):
    if prompt_mode:
        (hb_ref, halo_ref, w1a_ref, w1u_ref, cw_ref, cb_ref, w2_ref, h_ref, g_ref, b_ref,
         h2_ref, h2b_ref, a_ref, acc_sc) = refs
    else:
        (hb_ref, s1_ref, s2_ref, w1a_ref, w1u_ref, cw_ref, cb_ref, w2_ref, h_ref, g_ref, b_ref,
         h2_ref, h2b_ref, a_ref, acc_sc) = refs
    r = pl.program_id(0)
    ft = pl.program_id(1)

    @pl.when(ft == 0)
    def _():
        acc_sc[...] = jnp.zeros(acc_sc.shape, F32)

    hb = hb_ref[...]
    a = _dot(hb, w1a_ref[...])
    up = _dot(hb, w1u_ref[...])
    row = lax.broadcasted_iota(I32, a.shape, 0)
    p1 = pltpu.roll(a, 1, 0)
    p2 = pltpu.roll(a, 2, 0)
    if prompt_mode:
        a_halo = _dot(halo_ref[...], w1a_ref[...])
        keep = jnp.where(r % blocks_per_seq == 0, 0.0, 1.0)
        h6 = a_halo[HALO - 2:HALO - 1, :] * keep
        h7 = a_halo[HALO - 1:HALO, :] * keep
        p1 = jnp.where(row == 0, h7, p1)
        p2 = jnp.where(row == 0, h6, jnp.where(row == 1, h7, p2))
        a_ref[...] = a[tr - SUBLANES:, :]
    else:
        t = row & (seq_len - 1)
        p1 = jnp.where(t == 0, s1_ref[...], p1)
        p2 = jnp.where(t < 2, s2_ref[...], p2)
        a_ref[...] = a
    c = cb_ref[...] + cw_ref[0:1, :] * p2 + cw_ref[1:2, :] * p1 + cw_ref[2:3, :] * a
    hmid = (_gelu_tanh(c) * up).astype(BF16)
    acc_sc[...] += _dot(hmid, w2_ref[...])

    @pl.when(ft == pl.num_programs(1) - 1)
    def _():
        h2 = _layer_norm(alpha * h_ref[...] + acc_sc[...], g_ref[...], b_ref[...])
        h2_ref[...] = h2
        h2b_ref[...] = h2.astype(BF16)


def _ffn(h, hb, w1a, w1u, cw, cb, w2, g, b, layer, alpha, seq_len, state=None):
    rows = h.shape[0]
    prompt_mode = state is None
    tr = min(seq_len, 512) if prompt_mode else rows
    n_r = rows // tr
    n_f = D_FF_PAD // TF
    common_w = [pl.BlockSpec((None, D_MODEL, TF), lambda r, f: (layer, 0, f)),
                pl.BlockSpec((None, D_MODEL, TF), lambda r, f: (layer, 0, f)),
                pl.BlockSpec((None, SUBLANES, TF), lambda r, f: (layer, 0, f)),
                pl.BlockSpec((None, 1, TF), lambda r, f: (layer, 0, f)),
                pl.BlockSpec((None, TF, D_MODEL), lambda r, f: (layer, f, 0)),
                pl.BlockSpec((tr, D_MODEL), lambda r, f: (r, 0)),
                pl.BlockSpec((None, 1, D_MODEL), lambda r, f: (layer, 0, 0)),
                pl.BlockSpec((None, 1, D_MODEL), lambda r, f: (layer, 0, 0))]
    if prompt_mode:
        assert seq_len % tr == 0 and tr % HALO == 0
        per = tr // HALO
        extra_specs = [pl.BlockSpec((HALO, D_MODEL), lambda r, f: (jnp.maximum(r * per - 1, 0), 0))]
        extra = [hb]
        a_rows, a_blk = n_r * SUBLANES, SUBLANES
    else:
        assert seq_len & (seq_len - 1) == 0 and seq_len >= CONV_W - 1
        extra_specs = [pl.BlockSpec((tr, TF), lambda r, f: (0, f)),
                       pl.BlockSpec((tr, TF), lambda r, f: (0, f))]
        extra = list(state)
        a_rows, a_blk = rows, tr
    return pl.pallas_call(
        functools.partial(_ffn_kernel, alpha=alpha, tr=tr, seq_len=seq_len,
                          blocks_per_seq=max(seq_len // tr, 1), prompt_mode=prompt_mode),
        out_shape=(jax.ShapeDtypeStruct((rows, D_MODEL), F32), jax.ShapeDtypeStruct((rows, D_MODEL), BF16),
                   jax.ShapeDtypeStruct((a_rows, D_FF_PAD), F32)),
        grid=(n_r, n_f),
        in_specs=[pl.BlockSpec((tr, D_MODEL), lambda r, f: (r, 0))] + extra_specs + common_w,
        out_specs=(pl.BlockSpec((tr, D_MODEL), lambda r, f: (r, 0)),
                   pl.BlockSpec((tr, D_MODEL), lambda r, f: (r, 0)),
                   pl.BlockSpec((a_blk, TF), lambda r, f: (r, f))),
        scratch_shapes=[pltpu.VMEM((tr, D_MODEL), F32)],
        compiler_params=_params(("parallel", "arbitrary")),
        name="conv_ffn_ln",
    )(hb, *extra, w1a, w1u, cw, cb, w2, h, g, b)


def _ple_kernel(hb_ref, wg_ref, p_ref, wp_ref, h_ref, o_ref, ob_ref):
    gate = jax.nn.sigmoid(_dot(hb_ref[...], wg_ref[...]))
    out = h_ref[...] + gate * _dot(p_ref[...], wp_ref[...])
    o_ref[...] = out
    ob_ref[...] = out.astype(BF16)


def _ple(h2, h2b, pb, wg, wp, layer):
    rows = h2.shape[0]
    tr = min(rows, 1024)
    tn = 512
    return pl.pallas_call(
        _ple_kernel,
        out_shape=(jax.ShapeDtypeStruct((rows, D_MODEL), F32), jax.ShapeDtypeStruct((rows, D_MODEL), BF16)),
        grid=(rows // tr, D_MODEL // tn),
        in_specs=[pl.BlockSpec((tr, D_MODEL), lambda r, n: (r, 0)),
                  pl.BlockSpec((None, D_MODEL, tn), lambda r, n: (layer, 0, n)),
                  pl.BlockSpec((None, tr, PLE_DIM), lambda r, n: (layer, r, 0)),
                  pl.BlockSpec((None, PLE_DIM, tn), lambda r, n: (layer, 0, n)),
                  pl.BlockSpec((tr, tn), lambda r, n: (r, n))],
        out_specs=(pl.BlockSpec((tr, tn), lambda r, n: (r, n)),
                   pl.BlockSpec((tr, tn), lambda r, n: (r, n))),
        compiler_params=_params(("parallel", "arbitrary")),
        name="ple_gate",
    )(h2b, wg, pb, wp, h2)


IN_WIDTH = COL_KIW + IDX_DIM + IDX_HEADS + 2 * D_MODEL
PACK_ROWS = 128


SMALL_COLS = IDX_DIM + IDX_HEADS
KIW_TILE = COL_KIW // TN


def _pack_w_in_kernel(prev_ref, cur_ref, o_ref):
    n = pl.program_id(1)

    @pl.when(n < KIW_TILE)
    def _():
        o_ref[...] = cur_ref[...].T.astype(BF16)

    @pl.when(n == KIW_TILE)
    def _():
        t = jnp.concatenate([cur_ref[:SMALL_COLS, :], jnp.zeros((TN - SMALL_COLS, D_MODEL), F32)], axis=0)
        o_ref[...] = t.T.astype(BF16)

    @pl.when(n > KIW_TILE)
    def _():
        t = jnp.concatenate([prev_ref[SMALL_COLS:, :], cur_ref[:SMALL_COLS, :]], axis=0)
        o_ref[...] = t.T.astype(BF16)


def _pack_w_in(w_in):
    depth, rows, width = w_in.shape
    assert width == IN_WIDTH and rows == D_MODEL and SMALL_COLS % 16 == 0
    w_t = jnp.swapaxes(w_in, 1, 2)
    return pl.pallas_call(
        _pack_w_in_kernel,
        out_shape=jax.ShapeDtypeStruct((depth, rows, IN_COLS), BF16),
        grid=(depth, N_IN_TILES),
        in_specs=[pl.BlockSpec((None, TN, D_MODEL), lambda l, n: (l, jnp.maximum(n - 1, KIW_TILE), 0)),
                  pl.BlockSpec((None, TN, D_MODEL), lambda l, n: (l, n, 0))],
        out_specs=pl.BlockSpec((None, D_MODEL, TN), lambda l, n: (l, 0, n)),
        compiler_params=_params(("parallel", "arbitrary")),
        name="pack_w_in",
    )(w_t, w_t)


def _pack_ffn_in_kernel(w_ref, a_ref, u_ref):
    x = w_ref[...]
    zeros = jnp.zeros((x.shape[0], D_FF_PAD - D_FF), BF16)
    a_ref[:, :D_FF] = x[:, :D_FF].astype(BF16)
    a_ref[:, D_FF:] = zeros
    u_ref[:, :D_FF] = x[:, D_FF:].astype(BF16)
    u_ref[:, D_FF:] = zeros


def _pack_ffn_in(w_ffn_in):
    depth, rows, width = w_ffn_in.shape
    out = jax.ShapeDtypeStruct((depth, rows, D_FF_PAD), BF16)
    spec = pl.BlockSpec((None, PACK_ROWS, D_FF_PAD), lambda l, r: (l, r, 0))
    return pl.pallas_call(
        _pack_ffn_in_kernel,
        out_shape=(out, out),
        grid=(depth, rows // PACK_ROWS),
        in_specs=[pl.BlockSpec((None, PACK_ROWS, width), lambda l, r: (l, r, 0))],
        out_specs=(spec, spec),
        compiler_params=_params(("parallel", "parallel")),
        name="pack_ffn_in",
    )(w_ffn_in)


def _pack_ffn_out_kernel(w_ref, o_ref):
    row = pl.program_id(1) * TF + lax.broadcasted_iota(I32, w_ref.shape, 0)
    o_ref[...] = jnp.where(row < D_FF, w_ref[...], 0.0).astype(BF16)


def _pack_ffn_out(w_ffn_out):
    depth, _, cols = w_ffn_out.shape
    return pl.pallas_call(
        _pack_ffn_out_kernel,
        out_shape=jax.ShapeDtypeStruct((depth, D_FF_PAD, cols), BF16),
        grid=(depth, D_FF_PAD // TF),
        in_specs=[pl.BlockSpec((None, TF, cols), lambda l, r: (l, r, 0))],
        out_specs=pl.BlockSpec((None, TF, cols), lambda l, r: (l, r, 0)),
        compiler_params=_params(("parallel", "parallel")),
        name="pack_ffn_out",
    )(w_ffn_out)


def _rope_tables(pos):
    pos = pos.astype(F32)[:, None]

    def table(head_dim):
        rot = head_dim // 4
        half = rot // 2
        inv = ROPE_THETA ** (-(2.0 * jnp.arange(half, dtype=F32)) / rot)
        ang = pos * inv[None, :]
        cos, sin = jnp.cos(ang), jnp.sin(ang)
        ones = jnp.ones((pos.shape[0], head_dim - rot), F32)
        c = jnp.concatenate([cos, cos, ones], axis=1)
        s = jnp.concatenate([-sin, sin, 0.0 * ones], axis=1)
        reps = LANES // head_dim
        return jnp.tile(c, (1, reps)), jnp.tile(s, (1, reps))

    c128, s128 = table(HEAD_DIM)
    c64, s64 = table(IDX_DIM)
    return c128, s128, c64, s64


def kernel(x_prompt, x_sample, cache_sb_kv, cache_dsa_kv, cache_idx_k, state_ffn_conv, page_table,
           p_prompt, p_sample, w_in, w_branch_sb, w_branch_dsa, w_out, ln1_g, ln1_b, w_ffn_in,
           ffn_conv_w, ffn_conv_b, w_ffn_out, ln2_g, ln2_b, w_ple_gate, w_ple_proj):
    batch, seq = x_prompt.shape[:2]
    nseq, n_tok = x_sample.shape[:2]
    depth = w_in.shape[0]
    n_pool, page = cache_sb_kv.shape[1:3]
    n_pages = page_table.shape[1]
    past_len = n_pages * page
    alpha = (2 * depth) ** 0.25
    kv_w = 2 * N_KV * HEAD_DIM
    top_s = max(1, min(TOPK_MAX, (past_len + n_tok) // 4))
    pps = min(32, n_pages)
    pps_idx = min(64, n_pages)

    w_in_p = _pack_w_in(w_in)
    wa = w_branch_sb.astype(BF16)
    wb = w_branch_dsa.astype(BF16)
    wo = w_out.astype(BF16)
    ff_pad = D_FF_PAD - D_FF
    w1a, w1u = _pack_ffn_in(w_ffn_in)
    w2 = _pack_ffn_out(w_ffn_out)
    cw = jnp.pad(ffn_conv_w, ((0, 0), (0, SUBLANES - CONV_W), (0, ff_pad)))
    cb = jnp.pad(ffn_conv_b, ((0, 0), (0, ff_pad)))[:, None, :]
    wg = w_ple_gate.astype(BF16)
    wp = w_ple_proj.astype(BF16)
    g1, b1 = ln1_g[:, None, :], ln1_b[:, None, :]
    g2, b2 = ln2_g[:, None, :], ln2_b[:, None, :]

    tabs_p = _rope_tables(jnp.tile(jnp.arange(seq, dtype=jnp.int32), batch))
    tabs_s = _rope_tables(jnp.tile(past_len + jnp.arange(TOK_PAD, dtype=jnp.int32), nseq))

    sb_pages = cache_sb_kv.reshape(depth, n_pool, page * KV_SLOTS, HEAD_DIM)
    dsa_pages = cache_dsa_kv.reshape(depth, n_pool, page * KV_SLOTS, HEAD_DIM)
    idx_pages_t = jnp.swapaxes(cache_idx_k, 2, 3)

    assert n_tok <= TOK_PAD
    tok_pad = ((0, 0), (0, TOK_PAD - n_tok), (0, 0))
    xp = x_prompt.reshape(batch * seq, D_MODEL)
    xs = jnp.pad(x_sample, tok_pad).reshape(nseq * TOK_PAD, D_MODEL)
    xpb, xsb = xp.astype(BF16), xs.astype(BF16)
    ppb = p_prompt.reshape(depth, batch * seq, PLE_DIM).astype(BF16)
    psb = jnp.pad(p_sample, ((0, 0),) + tok_pad).reshape(depth, nseq * TOK_PAD, PLE_DIM).astype(BF16)
    st = jnp.pad(state_ffn_conv, ((0, 0), (0, 0), (0, 0), (0, ff_pad)))
    conv_s1 = jnp.repeat(st[:, :, 1], TOK_PAD, axis=1)
    conv_s2 = jnp.pad(st, ((0, 0), (0, 0), (0, TOK_PAD - (CONV_W - 1)), (0, 0))).reshape(
        depth, nseq * TOK_PAD, D_FF_PAD)

    def new_row_buffers(rows):
        return (jnp.zeros((depth, rows, kv_w), F32), jnp.zeros((depth, rows, kv_w), F32),
                jnp.zeros((depth, rows, LANES), F32))

    new_p = new_row_buffers(batch * seq)
    new_s = new_row_buffers(nseq * TOK_PAD)
    outs = {k: [] for k in ("conv_p", "conv_s")}
    for l in range(depth):
        y, *new_p = _project(xpb, w_in_p, l, tabs_p, new_p)
        oa = _sb_prompt(y, batch, seq)
        ob = _dsa_prompt(y, batch, seq)
        h, hb = _merge(oa, ob, y, xp, wa, wb, wo, g1, b1, l, alpha)
        h2, h2b, a_tail = _ffn(h, hb, w1a, w1u, cw, cb, w2, g2, b2, l, alpha, seq)
        xp, xpb = _ple(h2, h2b, ppb, wg, wp, l)
        tails = a_tail.reshape(batch, -1, SUBLANES, D_FF_PAD)[:, -1, SUBLANES - (CONV_W - 1):, :D_FF]
        outs["conv_p"].append(tails)

        ys, *new_s = _project(xsb, w_in_p, l, tabs_s, new_s)
        oa_s = _sb_sample(page_table, ys, sb_pages, l, pps)
        bias = _idx_sample(page_table, ys, idx_pages_t, l, pps_idx, top_s, n_tok)
        ob_s = _dsa_sample(page_table, ys, bias, dsa_pages, l, pps)
        hs, hsb = _merge(oa_s, ob_s, ys, xs, wa, wb, wo, g1, b1, l, alpha)
        h2s, h2sb, a_s = _ffn(hs, hsb, w1a, w1u, cw, cb, w2, g2, b2, l, alpha,
                              TOK_PAD, state=(conv_s1[l], conv_s2[l]))
        xs, xsb = _ple(h2s, h2sb, psb, wg, wp, l)
        outs["conv_s"].append(a_s.reshape(nseq, TOK_PAD, D_FF_PAD)[:, n_tok - (CONV_W - 1):n_tok, :D_FF])

    sb_p, dsa_p, kiw_p = new_p
    sb_s, dsa_s, kiw_s = [b.reshape(depth, nseq, TOK_PAD, -1)[:, :, :n_tok] for b in new_s]
    kv_shape = (2, N_KV, HEAD_DIM)
    return (xp.reshape(batch, seq, D_MODEL), xs.reshape(nseq, TOK_PAD, D_MODEL)[:, :n_tok],
            sb_p.reshape(depth, batch, seq, *kv_shape), dsa_p.reshape(depth, batch, seq, *kv_shape),
            kiw_p[..., :IDX_DIM].reshape(depth, batch, seq, IDX_DIM), jnp.stack(outs["conv_p"]),
            sb_s.reshape(depth, nseq, n_tok, *kv_shape), dsa_s.reshape(depth, nseq, n_tok, *kv_shape),
            kiw_s[..., :IDX_DIM], jnp.stack(outs["conv_s"]))
```

```python
import functools

import jax
import jax.numpy as jnp
import numpy as np
from jax import lax
from jax.experimental import pallas as pl
from jax.experimental.pallas import tpu as pltpu

F32 = jnp.float32
BF16 = jnp.bfloat16
I32 = jnp.int32

D_MODEL = 2048
HEAD_DIM = 128
N_HEADS = 8
N_KV = 2
GROUP = N_HEADS // N_KV
IDX_HEADS = 16
IDX_DIM = 64
TOPK_MAX = 256
ROPE_THETA = 500000.0
D_FF = 5504
CONV_W = 3
PLE_DIM = 256
LN_EPS = 1e-5

LANES = 128
SUBLANES = 8
VMEM_LIMIT = 56 * 1024 * 1024

TN = 512
COL_QA = 0
COL_KVA = 1024
COL_QB = 1536
COL_KVB = 2560
COL_QI = 3072
COL_KIW = 4096
COL_GA = 4608
COL_GB = 6656
IN_COLS = 8704
N_IN_TILES = IN_COLS // TN
D_FF_PAD = 5632
TF = 512

SB_Q_SCALE = (HEAD_DIM ** -0.5) * float(np.log2(np.e))
SB_STACK_ROWS = 64
NEG_BIG = -1e30
KEY_NEG_INF = np.int32(np.array(0xFF800000, dtype=np.uint32).view(np.int32) ^ 0x7FFFFFFF)
INT_MIN = np.int32(-2 ** 31)


def _params(sem):
    return pltpu.CompilerParams(dimension_semantics=sem, vmem_limit_bytes=VMEM_LIMIT)


def _dot_t(a, b):
    return lax.dot_general(a, b, (((1,), (1,)), ((), ())), preferred_element_type=F32)


def _dot(a, b):
    return jnp.dot(a, b, preferred_element_type=F32)


def _rope(y, c, s, half):
    w = y.shape[1]
    reps = w // LANES
    if reps > 1:
        c = jnp.concatenate([c] * reps, axis=1)
        s = jnp.concatenate([s] * reps, axis=1)
    lane = lax.broadcasted_iota(I32, y.shape, 1)
    first = (lane & (2 * half - 1)) < half
    partner = jnp.where(first, pltpu.roll(y, w - half, 1), pltpu.roll(y, half, 1))
    return y * c + partner * s


def _proj_kernel(x_ref, w_ref, c128_ref, s128_ref, c64_ref, s64_ref, sb_in, dsa_in, kiw_in,
                 y_ref, sb_ref, dsa_ref, kiw_ref):
    del sb_in, dsa_in, kiw_in
    n = pl.program_id(1)

    def product():
        return _dot(x_ref[...], w_ref[...])

    @pl.when(jnp.logical_and(n < COL_QB // TN, n != COL_KVA // TN))
    def _():
        y_ref[...] = product()

    def store_kv_rows(ref, tile):
        for slot in range(TN // HEAD_DIM):
            ref[pl.ds(slot, tile.shape[0], stride=TN // HEAD_DIM), :] = tile[:, slot * HEAD_DIM:(slot + 1) * HEAD_DIM]

    @pl.when(n == COL_KVA // TN)
    def _():
        y = product()
        y_ref[...] = y
        store_kv_rows(sb_ref, y)

    @pl.when(jnp.logical_and(n >= COL_QB // TN, n < COL_KVB // TN))
    def _():
        y_ref[...] = _rope(product(), c128_ref[...], s128_ref[...], 16)

    @pl.when(n == COL_KVB // TN)
    def _():
        y = product()
        k_rot = _rope(y[:, :256], c128_ref[...], s128_ref[...], 16)
        y_ref[:, :256] = k_rot
        y_ref[:, 256:] = y[:, 256:]
        store_kv_rows(dsa_ref, jnp.concatenate([k_rot, y[:, 256:]], axis=1))

    @pl.when(jnp.logical_and(n >= COL_QI // TN, n < COL_KIW // TN))
    def _():
        y_ref[...] = _rope(product(), c64_ref[...], s64_ref[...], 8)

    @pl.when(n == COL_KIW // TN)
    def _():
        y = product()
        lane = lax.broadcasted_iota(I32, c64_ref.shape, 1)
        c = jnp.where(lane < IDX_DIM, c64_ref[...], 1.0)
        s = jnp.where(lane < IDX_DIM, s64_ref[...], 0.0)
        kiw = _rope(y[:, :LANES], c, s, 8)
        y_ref[:, :LANES] = kiw
        y_ref[:, LANES:] = y[:, LANES:]
        kiw_ref[...] = kiw

    @pl.when(n >= COL_GA // TN)
    def _():
        y_ref[...] = jax.nn.sigmoid(product())


def _project(xb, w, layer, tabs, new_rows):
    rows = xb.shape[0]
    tr = min(rows, 1024)
    tab_spec = pl.BlockSpec((tr, LANES), lambda r, n: (r, 0))
    any_spec = pl.BlockSpec(memory_space=pl.ANY)
    kv_spec = pl.BlockSpec((None, tr * (TN // HEAD_DIM), HEAD_DIM), lambda r, n: (layer, r, 0))
    return pl.pallas_call(
        _proj_kernel,
        out_shape=(jax.ShapeDtypeStruct((rows, IN_COLS), F32),)
                  + tuple(jax.ShapeDtypeStruct(b.shape, b.dtype) for b in new_rows),
        grid=(rows // tr, N_IN_TILES),
        in_specs=[pl.BlockSpec((tr, D_MODEL), lambda r, n: (r, 0)),
                  pl.BlockSpec((None, D_MODEL, TN), lambda r, n: (layer, 0, n)),
                  tab_spec, tab_spec, tab_spec, tab_spec, any_spec, any_spec, any_spec],
        out_specs=(pl.BlockSpec((tr, TN), lambda r, n: (r, n)), kv_spec, kv_spec,
                   pl.BlockSpec((None, tr, LANES), lambda r, n: (layer, r, 0))),
        input_output_aliases={6: 1, 7: 2, 8: 3},
        compiler_params=_params(("parallel", "arbitrary")),
        name="in_proj",
    )(xb, w, *tabs, *new_rows)


def _suffix_matrix():
    j = lax.broadcasted_iota(I32, (2 * LANES, 2 * LANES), 0) & (LANES - 1)
    s = lax.broadcasted_iota(I32, (2 * LANES, 2 * LANES), 1)
    return jnp.where(jnp.logical_or(j > s, s >= LANES), -1.0, 0.0).astype(BF16)


def _sb_scores(q, k, mask):
    n_sub = k.shape[0] // LANES
    z = _dot_t(q, k)
    neg_abs = pltpu.bitcast(pltpu.bitcast(z, I32) | INT_MIN, F32)
    sp = jnp.maximum(z, 0.0) + jnp.log2(1.0 + jnp.exp2(neg_abs))
    spm = sp if mask is None else jnp.where(mask, sp, 0.0)
    hi = spm.astype(BF16)
    lo = (spm - hi.astype(F32)).astype(BF16)
    subs = [jnp.concatenate([hi[:, i * LANES:(i + 1) * LANES], lo[:, i * LANES:(i + 1) * LANES]], axis=1)
            for i in range(n_sub)]
    return z - sp, subs


def _sb_weights(d, subs, v, u, carry, mask):
    n_sub = len(subs)
    m_rows = d.shape[0]
    if m_rows <= SB_STACK_ROWS:
        r_all = _dot(subs[0] if n_sub == 1 else jnp.concatenate(subs, axis=0), u)
        rs = [r_all[i * m_rows:(i + 1) * m_rows] for i in range(n_sub)]
    else:
        rs = [_dot(s, u) for s in subs]
    afters = [None] * n_sub
    for i in reversed(range(n_sub)):
        afters[i] = rs[i][:, :LANES] + carry
        carry = carry + rs[i][:, LANES:]
    after = afters[0] if n_sub == 1 else jnp.concatenate(afters, axis=1)
    w = jnp.exp2(d + after)
    if mask is not None:
        w = jnp.where(mask, w, 0.0)
    return _dot(w.astype(BF16), v), carry


def _sb_block(q, k, v, u, carry, mask):
    d, subs = _sb_scores(q, k, mask)
    return _sb_weights(d, subs, v, u, carry, mask)


def _softmax_block(q, k, v, bias, m_old, l_old, acc_old):
    logit = _dot_t(q, k) + bias
    m_new = jnp.maximum(m_old, jnp.max(logit, axis=1, keepdims=True))
    p = jnp.exp(logit - m_new[:, 0:1])
    alpha = jnp.exp(m_old - m_new)
    l_new = alpha * l_old + jnp.sum(p, axis=1, keepdims=True)
    acc_new = alpha[:, 0:1] * acc_old + _dot(p.astype(BF16), v)
    return m_new, l_new, acc_new


def _sortable(x):
    b = pltpu.bitcast(x, I32)
    return jnp.where(b < 0, b ^ jnp.int32(0x7FFFFFFF), b)


def _kth_largest(count_ge, shape, k, n_total):
    bits_per_check = 4

    def cond(state):
        i, _, cnt = state
        return jnp.logical_and(i < 32, jnp.max(cnt) > k)

    def body(state):
        i, t, cnt = state
        for _ in range(bits_per_check):
            cand = t + jnp.left_shift(jnp.int32(1), jnp.int32(31) - i)
            c = count_ge(cand)
            ok = c >= k
            i, t, cnt = i + 1, jnp.where(ok, cand, t), jnp.where(ok, c, cnt)
        return i, t, cnt

    state = (jnp.int32(0), jnp.full(shape, INT_MIN, I32), jnp.full(shape, n_total, I32))
    _, t, cnt = lax.while_loop(cond, body, state)
    return t, cnt


def _tie_cutoff(count_tie_below, shape, need, n_bits):
    def body(i, j):
        cand = j + jnp.left_shift(jnp.int32(1), jnp.int32(n_bits - 1) - i)
        return jnp.where(count_tie_below(cand) < need, cand, j)

    return lax.fori_loop(0, n_bits, body, jnp.zeros(shape, I32))


def _sb_prompt_kernel(q_ref, k_ref, v_ref, o_ref, kb_sc, vb_sc, carry_sc, acc_sc, *, tq):
    iq = pl.program_id(1)

    @pl.when(iq == 0)
    def _():
        kb_sc[...] = k_ref[...].astype(BF16)
        vb_sc[...] = v_ref[...].astype(BF16)

    scale = SB_Q_SCALE
    u = _suffix_matrix()
    n_chain = N_HEADS // 2
    c_rows = 2 * tq
    qs = []
    for c in range(n_chain):
        q = jnp.concatenate([q_ref[:, h * HEAD_DIM:(h + 1) * HEAD_DIM] for h in (2 * c, 2 * c + 1)], axis=0)
        qs.append((q * scale).astype(BF16))

    def kv_block(o2, c):
        g = (2 * c) // GROUP
        return (kb_sc[pl.ds(o2, tq), g * HEAD_DIM:(g + 1) * HEAD_DIM],
                vb_sc[pl.ds(o2, tq), g * HEAD_DIM:(g + 1) * HEAD_DIM])

    row = lax.broadcasted_iota(I32, (c_rows, tq), 0) & (tq - 1)
    col = lax.broadcasted_iota(I32, (c_rows, tq), 1)
    off = pl.multiple_of(iq * tq, tq)
    for c in range(n_chain):
        k, v = kv_block(off, c)
        contrib, carry = _sb_block(qs[c], k, v, u, jnp.zeros((c_rows, LANES), F32), col < row)
        acc_sc[c] = contrib
        carry_sc[c] = carry

    def body(i, _):
        o2 = pl.multiple_of((iq - 1 - i) * tq, tq)
        for c in range(n_chain):
            k, v = kv_block(o2, c)
            contrib, carry = _sb_block(qs[c], k, v, u, carry_sc[c], None)
            acc_sc[c] += contrib
            carry_sc[c] = carry
        return 0

    lax.fori_loop(0, iq, body, 0)
    for h in range(N_HEADS):
        r0 = (h % 2) * tq
        o_ref[:, h * HEAD_DIM:(h + 1) * HEAD_DIM] = acc_sc[h // 2, r0:r0 + tq, :].astype(o_ref.dtype)


def _sb_prompt(y, batch, seq):
    tq = min(2 * LANES, seq)
    nq = seq // tq
    qw = N_HEADS * HEAD_DIM
    kvw = N_KV * HEAD_DIM
    return pl.pallas_call(
        functools.partial(_sb_prompt_kernel, tq=tq),
        out_shape=jax.ShapeDtypeStruct((batch * seq, qw), BF16),
        grid=(batch, nq),
        in_specs=[pl.BlockSpec((tq, qw), lambda b, i: (b * nq + i, COL_QA // qw)),
                  pl.BlockSpec((seq, kvw), lambda b, i: (b, COL_KVA // kvw)),
                  pl.BlockSpec((seq, kvw), lambda b, i: (b, COL_KVA // kvw + 1))],
        out_specs=pl.BlockSpec((tq, qw), lambda b, i: (b * nq + i, 0)),
        scratch_shapes=[pltpu.VMEM((seq, kvw), BF16), pltpu.VMEM((seq, kvw), BF16),
                        pltpu.VMEM((N_HEADS // 2, 2 * tq, LANES), F32),
                        pltpu.VMEM((N_HEADS // 2, 2 * tq, LANES), F32)],
        compiler_params=_params(("parallel", "arbitrary")),
        name="sb_prompt",
    )(y, y, y)


TKI = 256
LOGIT_SAFE = 60.0
NORM_SLACK = 1.02


def _fold_rows(x, op):
    return op(x.reshape(x.shape[0] // SUBLANES, SUBLANES, x.shape[1]), axis=0)


def _dsa_prompt_kernel(qb0_ref, qb1_ref, k_ref, v_ref, qi_ref, kiw_all_ref, kiw_q_ref, o_ref,
                       kb_sc, vb_sc, ki2_sc, key_sc, bias_sc, cut_sc, kn_sc, m_sc, l_sc, acc_sc,
                       *, tq, n_top, idx_bits):
    iq = pl.program_id(1)
    seq = k_ref.shape[0]

    @pl.when(iq == 0)
    def _():
        kb_sc[...] = k_ref[...].astype(BF16)
        vb_sc[...] = v_ref[...].astype(BF16)
        k_sq = k_ref[...] * k_ref[...]
        k_sq_max = jnp.float32(0.0)
        for g in range(N_KV):
            norms = jnp.sum(k_sq[:, g * HEAD_DIM:(g + 1) * HEAD_DIM], axis=1, keepdims=True)
            k_sq_max = jnp.maximum(k_sq_max, jnp.max(norms))
        kn_sc[...] = jnp.zeros(kn_sc.shape, F32) + k_sq_max
        lane = lax.broadcasted_iota(I32, (seq, LANES), 1)
        kia = jnp.where(lane < IDX_DIM, kiw_all_ref[...], 0.0)
        ki2_sc[0] = kia.astype(BF16)
        ki2_sc[1] = pltpu.roll(kia, IDX_DIM, 1).astype(BF16)

    w_t = (kiw_q_ref[...] * ((IDX_DIM ** -0.5) * (IDX_HEADS ** -0.5))).T
    w_rows = [w_t[IDX_DIM + h:IDX_DIM + h + 1, :] for h in range(IDX_HEADS)]

    q_pairs = [qi_ref[:, p * LANES:(p + 1) * LANES].astype(BF16) for p in range(IDX_HEADS // 2)]
    q_pos = iq * tq + lax.broadcasted_iota(I32, (1, tq), 1)
    n_blk = (iq * tq + tq + TKI - 1) // TKI

    def idx_body(j, _):
        off = pl.multiple_of(j * TKI, TKI)
        ki_e = ki2_sc[0, pl.ds(off, TKI), :]
        ki_o = ki2_sc[1, pl.ds(off, TKI), :]
        acc = jnp.zeros((TKI, tq), F32)
        for p in range(IDX_HEADS // 2):
            acc = acc + jnp.maximum(_dot_t(ki_e, q_pairs[p]), 0.0) * w_rows[2 * p]
            acc = acc + jnp.maximum(_dot_t(ki_o, q_pairs[p]), 0.0) * w_rows[2 * p + 1]
        k_pos = off + lax.broadcasted_iota(I32, (TKI, 1), 0)
        key_sc[pl.ds(off, TKI), :] = jnp.where(k_pos <= q_pos, _sortable(acc), KEY_NEG_INF)
        return 0

    lax.fori_loop(0, n_blk, idx_body, 0)

    def query_counts(hit_fn):
        def body(j, c):
            off = pl.multiple_of(j * TKI, TKI)
            return c + _fold_rows(hit_fn(key_sc[pl.ds(off, TKI), :], off), jnp.sum)

        c = lax.fori_loop(0, n_blk, body, jnp.zeros((SUBLANES, tq), I32))
        return jnp.sum(c, axis=0, keepdims=True)

    def count_ge(t):
        return query_counts(lambda kb, off: jnp.where(kb >= t, 1, 0))

    thr, n_ge = _kth_largest(count_ge, (1, tq), n_top, n_blk * TKI)

    cut_sc[...] = jnp.full(cut_sc.shape, seq, I32)

    @pl.when(jnp.max(n_ge) > n_top)
    def _():
        need = n_top - count_ge(thr + 1)

        def count_tie_below(jc):
            def hit(kb, off):
                idx = off + lax.broadcasted_iota(I32, kb.shape, 0)
                return jnp.where(kb == thr, jnp.where(idx < jc, 1, 0), 0)
            return query_counts(hit)

        cut = _tie_cutoff(count_tie_below, (1, tq), need, idx_bits)
        cut_sc[...] = jnp.broadcast_to(cut, cut_sc.shape)

    cut = cut_sc[0:1, :]

    def bias_body(j, _):
        off = pl.multiple_of(j * TKI, TKI)
        kb = key_sc[pl.ds(off, TKI), :]
        idx = off + lax.broadcasted_iota(I32, (TKI, tq), 0)
        tie = jnp.where(kb == thr, jnp.where(idx <= cut, 0.0, -jnp.inf), -jnp.inf)
        sel = jnp.where(kb > thr, 0.0, tie)
        sel = jnp.where(kb > KEY_NEG_INF, sel, -jnp.inf)
        bias_sc[:, pl.ds(off, TKI)] = sel.T
        return 0

    lax.fori_loop(0, n_blk, bias_body, 0)

    scale = (HEAD_DIM ** -0.5) * np.log2(np.e)
    n_chain = N_HEADS // 2
    n_sub = TKI // LANES
    qs = []
    ones = jnp.ones((HEAD_DIM, LANES), BF16)
    q_sq = jnp.zeros((SUBLANES, LANES), F32)
    for c in range(n_chain):
        q_ref = (qb0_ref, qb1_ref)[c // 2]
        hs = (2 * (c % 2), 2 * (c % 2) + 1)
        q = jnp.concatenate([q_ref[:, h * HEAD_DIM:(h + 1) * HEAD_DIM] for h in hs], axis=0) * scale
        q_sq = jnp.maximum(q_sq, _fold_rows(_dot((q * q).astype(BF16), ones), jnp.max))
        qs.append(q.astype(BF16))
    q_sq_max = jnp.max(q_sq) * NORM_SLACK

    def bias_rows(off):
        b = bias_sc[:, pl.ds(off, TKI)]
        return jnp.concatenate([b, b], axis=0)

    small_logits = q_sq_max * jnp.max(kn_sc[...]) <= LOGIT_SAFE * LOGIT_SAFE

    @pl.when(small_logits)
    def _():
        m_sc[...] = jnp.zeros(m_sc.shape, F32)

    @pl.when(jnp.logical_not(small_logits))
    def _():
        m_sc[...] = jnp.full(m_sc.shape, -jnp.inf, F32)

        def max_body(j, _):
            off = pl.multiple_of(j * TKI, TKI)
            bias = bias_rows(off)
            for c in range(n_chain):
                g = c // 2
                lg = _dot_t(qs[c], kb_sc[pl.ds(off, TKI), g * HEAD_DIM:(g + 1) * HEAD_DIM]) + bias
                m = m_sc[c]
                for i in range(n_sub):
                    m = jnp.maximum(m, lg[:, i * LANES:(i + 1) * LANES])
                m_sc[c] = m
            return 0

        lax.fori_loop(0, n_blk, max_body, 0)
        for c in range(n_chain):
            m_sc[c] = jnp.broadcast_to(jnp.max(m_sc[c], axis=1, keepdims=True), (2 * tq, LANES))

    l_sc[...] = jnp.zeros(l_sc.shape, F32)
    acc_sc[...] = jnp.zeros(acc_sc.shape, F32)

    def sum_body(j, _):
        off = pl.multiple_of(j * TKI, TKI)
        bias = bias_rows(off)
        for c in range(n_chain):
            g = c // 2
            lg = _dot_t(qs[c], kb_sc[pl.ds(off, TKI), g * HEAD_DIM:(g + 1) * HEAD_DIM]) + bias
            p = jnp.exp2(lg - jnp.concatenate([m_sc[c]] * n_sub, axis=1))
            l = l_sc[c]
            for i in range(n_sub):
                l = l + p[:, i * LANES:(i + 1) * LANES]
            l_sc[c] = l
            acc_sc[c] += _dot(p.astype(BF16), vb_sc[pl.ds(off, TKI), g * HEAD_DIM:(g + 1) * HEAD_DIM])
        return 0

    lax.fori_loop(0, n_blk, sum_body, 0)
    for c in range(n_chain):
        out = acc_sc[c] / jnp.sum(l_sc[c], axis=1, keepdims=True)
        for hh in range(2):
            c0 = (2 * c + hh) * HEAD_DIM
            o_ref[:, c0:c0 + HEAD_DIM] = out[hh * tq:(hh + 1) * tq, :].astype(o_ref.dtype)


def _dsa_prompt(y, batch, seq):
    tq = min(2 * LANES, seq)
    assert tq == TKI or seq == tq
    nq = seq // tq
    gw = GROUP * HEAD_DIM
    kvw = N_KV * HEAD_DIM
    n_top = max(1, min(TOPK_MAX, seq // 4))
    idx_bits = int(seq).bit_length()
    q_rows = 2 * tq
    return pl.pallas_call(
        functools.partial(_dsa_prompt_kernel, tq=tq, n_top=n_top, idx_bits=idx_bits),
        out_shape=jax.ShapeDtypeStruct((batch * seq, N_HEADS * HEAD_DIM), BF16),
        grid=(batch, nq),
        in_specs=[pl.BlockSpec((tq, gw), lambda b, i: (b * nq + i, COL_QB // gw)),
                  pl.BlockSpec((tq, gw), lambda b, i: (b * nq + i, COL_QB // gw + 1)),
                  pl.BlockSpec((seq, kvw), lambda b, i: (b, COL_KVB // kvw)),
                  pl.BlockSpec((seq, kvw), lambda b, i: (b, COL_KVB // kvw + 1)),
                  pl.BlockSpec((tq, IDX_HEADS * IDX_DIM), lambda b, i: (b * nq + i, COL_QI // (IDX_HEADS * IDX_DIM))),
                  pl.BlockSpec((seq, LANES), lambda b, i: (b, COL_KIW // LANES)),
                  pl.BlockSpec((tq, LANES), lambda b, i: (b * nq + i, COL_KIW // LANES))],
        out_specs=pl.BlockSpec((tq, N_HEADS * HEAD_DIM), lambda b, i: (b * nq + i, 0)),
        scratch_shapes=[pltpu.VMEM((seq, kvw), BF16),
                        pltpu.VMEM((seq, kvw), BF16),
                        pltpu.VMEM((2, seq, LANES), BF16),
                        pltpu.VMEM((seq, tq), I32),
                        pltpu.VMEM((tq, seq), F32),
                        pltpu.VMEM((SUBLANES, tq), I32),
                        pltpu.VMEM((SUBLANES, LANES), F32),
                        pltpu.VMEM((N_HEADS // 2, q_rows, LANES), F32),
                        pltpu.VMEM((N_HEADS // 2, q_rows, LANES), F32),
                        pltpu.VMEM((N_HEADS // 2, q_rows, HEAD_DIM), F32)],
        compiler_params=_params(("parallel", "arbitrary")),
        name="dsa_prompt",
    )(y, y, y, y, y, y, y)


TOK_PAD = SUBLANES
S_ROWS = GROUP * TOK_PAD


KV_SLOTS = 2 * N_KV


def _page_rows(ref, slot, page):
    return ref[pl.ds(slot, page, stride=KV_SLOTS), :].astype(BF16)


J_ROWS = N_KV * S_ROWS
J_LANES = N_KV * HEAD_DIM


def _gather_kv(page_refs, page):
    k = [jnp.concatenate([_page_rows(r, g, page) for g in range(N_KV)], axis=1) for r in page_refs]
    v = [jnp.concatenate([_page_rows(r, N_KV + g, page) for g in range(N_KV)], axis=1) for r in page_refs]
    if len(page_refs) == 1:
        return k[0], v[0]
    return jnp.concatenate(k, axis=0), jnp.concatenate(v, axis=0)


def _joint_queries(q_blks, scale):
    rows = []
    for g, q_blk in enumerate(q_blks):
        q = jnp.concatenate([q_blk[:, h * HEAD_DIM:(h + 1) * HEAD_DIM] for h in range(GROUP)], axis=0) * scale
        zero = jnp.zeros_like(q)
        rows.append(jnp.concatenate([q if gg == g else zero for gg in range(N_KV)], axis=1))
    return jnp.concatenate(rows, axis=0).astype(BF16)


def _new_token_kv(new_ref, page):
    pad = jnp.zeros((page - TOK_PAD, J_LANES), F32)
    k = jnp.concatenate([new_ref[:, :J_LANES], pad], axis=0)
    v = jnp.concatenate([new_ref[:, J_LANES:], pad], axis=0)
    return k.astype(BF16), v.astype(BF16)


def _unstack_heads(o_ref, acc):
    for g in range(N_KV):
        for h in range(GROUP):
            r0 = g * S_ROWS + h * TOK_PAD
            c0 = (g * GROUP + h) * HEAD_DIM
            o_ref[:, c0:c0 + HEAD_DIM] = acc[r0:r0 + TOK_PAD, g * HEAD_DIM:(g + 1) * HEAD_DIM]


def _sb_sample_kernel(pt_ref, q_ref, new_ref, *rest, pps, page):
    page_refs = rest[:pps]
    o_ref = rest[pps]
    carry_sc, acc_sc = rest[pps + 1:]
    j = pl.program_id(1)
    u = _suffix_matrix()
    gw = GROUP * HEAD_DIM
    q = _joint_queries([q_ref[:, g * gw:(g + 1) * gw] for g in range(N_KV)], SB_Q_SCALE)

    def visit(k, v, mask):
        c, cr = _sb_block(q, k, v, u, carry_sc[...], mask)
        acc_sc[...] += c
        carry_sc[...] = cr

    @pl.when(j == 0)
    def _():
        carry_sc[...] = jnp.zeros(carry_sc.shape, F32)
        acc_sc[...] = jnp.zeros(acc_sc.shape, F32)
        tok = lax.broadcasted_iota(I32, (J_ROWS, page), 0) & (TOK_PAD - 1)
        col = lax.broadcasted_iota(I32, (J_ROWS, page), 1)
        visit(*_new_token_kv(new_ref, page), col < tok)

    visit(*_gather_kv(page_refs, page), None)

    @pl.when(j == pl.num_programs(1) - 1)
    def _():
        _unstack_heads(o_ref, acc_sc[...])


def _sb_sample(page_table, ys, cache, layer, pps):
    nseq, n_pages = page_table.shape
    rows = cache.shape[2]
    page = rows // KV_SLOTS
    n_steps = n_pages // pps
    qw = N_HEADS * HEAD_DIM
    kvw = KV_SLOTS * HEAD_DIM

    def page_spec(i):
        return pl.BlockSpec((None, None, rows, HEAD_DIM),
                            lambda b, j, pt, i=i: (layer, pt[b, (n_steps - 1 - j) * pps + i], 0, 0))

    grid_spec = pltpu.PrefetchScalarGridSpec(
        num_scalar_prefetch=1,
        grid=(nseq, n_steps),
        in_specs=[pl.BlockSpec((TOK_PAD, qw), lambda b, j, pt: (b, COL_QA // qw)),
                  pl.BlockSpec((TOK_PAD, kvw), lambda b, j, pt: (b, COL_KVA // kvw))]
                 + [page_spec(i) for i in range(pps)],
        out_specs=pl.BlockSpec((TOK_PAD, qw), lambda b, j, pt: (b, 0)),
        scratch_shapes=[pltpu.VMEM((J_ROWS, LANES), F32), pltpu.VMEM((J_ROWS, J_LANES), F32)],
    )
    return pl.pallas_call(
        functools.partial(_sb_sample_kernel, pps=pps, page=page),
        out_shape=jax.ShapeDtypeStruct((nseq * TOK_PAD, qw), F32),
        grid_spec=grid_spec,
        compiler_params=_params(("parallel", "arbitrary")),
        name="sb_sample",
    )(page_table, ys, ys, *([cache] * pps))


def _idx_sample_kernel(pt_ref, qi_ref, kiw_ref, *rest, pps, page, n_pages, n_top, idx_bits, n_tok):
    page_refs = rest[:pps]
    bias_ref = rest[pps]
    score_sc = rest[pps + 1]
    j = pl.program_id(1)

    row = lax.broadcasted_iota(I32, (TOK_PAD, 1), 0)

    def real_rows(x):
        return jnp.where(row < n_tok, x, pltpu.roll(x, n_tok, 0))

    qi_rows = real_rows(qi_ref[...])
    kiw = real_rows(kiw_ref[...])
    qi = jnp.concatenate([qi_rows[:, h * IDX_DIM:(h + 1) * IDX_DIM] for h in range(IDX_HEADS)],
                         axis=0).astype(BF16)
    w_scale = (IDX_DIM ** -0.5) * (IDX_HEADS ** -0.5)
    wm = jnp.concatenate([jnp.broadcast_to(kiw[:, IDX_DIM + h:IDX_DIM + h + 1] * w_scale, (TOK_PAD, LANES))
                          for h in range(IDX_HEADS)], axis=0)

    def head_sum(s):
        n = s.shape[1]
        w = wm if n == LANES else jnp.concatenate([wm] * (n // LANES), axis=1)
        return jnp.sum((jnp.maximum(s, 0.0) * w).reshape(IDX_HEADS, TOK_PAD, n), axis=0)

    @pl.when(j == 0)
    def _():
        ki_new = jnp.concatenate([kiw_ref[:, :IDX_DIM], jnp.zeros((page - TOK_PAD, IDX_DIM), F32)], axis=0)
        tok = lax.broadcasted_iota(I32, (TOK_PAD, page), 0) & (n_tok - 1)
        col = lax.broadcasted_iota(I32, (TOK_PAD, page), 1)
        s_new = head_sum(_dot_t(qi, ki_new.astype(BF16)))
        score_sc[:, n_pages * page:] = jnp.where(col <= tok, s_new, -jnp.inf)

    off = pl.multiple_of(j * (pps * page), pps * page)
    ki_t = jnp.concatenate([r[...] for r in page_refs], axis=1)
    score_sc[:, pl.ds(off, pps * page)] = head_sum(_dot(qi, ki_t.astype(BF16)))

    @pl.when(j == pl.num_programs(1) - 1)
    def _():
        key = _sortable(score_sc[...])
        idx = lax.broadcasted_iota(I32, key.shape, 1)

        def count_ge(t):
            return jnp.sum(jnp.where(key >= t, 1, 0), axis=1, keepdims=True)

        thr, n_ge = _kth_largest(count_ge, (TOK_PAD, 1), n_top, key.shape[1])

        def search_cut():
            need = n_top - count_ge(thr + 1)

            def count_tie_below(jc):
                return jnp.sum(jnp.where(key == thr, jnp.where(idx < jc, 1, 0), 0), axis=1, keepdims=True)

            return _tie_cutoff(count_tie_below, (TOK_PAD, 1), need, idx_bits)

        cut = lax.cond(jnp.max(n_ge) > n_top, search_cut,
                       lambda: jnp.full((TOK_PAD, 1), key.shape[1], I32))
        tie = jnp.where(key == thr, jnp.where(idx <= cut, 0.0, -jnp.inf), -jnp.inf)
        sel = jnp.where(key > thr, 0.0, tie)
        bias_ref[0] = jnp.where(key > KEY_NEG_INF, sel, -jnp.inf)


def _idx_sample(page_table, ys, cache_t, layer, pps, n_top, n_tok):
    assert n_tok & (n_tok - 1) == 0 and TOK_PAD % n_tok == 0
    nseq, n_pages = page_table.shape
    page = cache_t.shape[3]
    n_cols = (n_pages + 1) * page
    cache = cache_t
    qiw = IDX_HEADS * IDX_DIM

    def page_spec(i):
        return pl.BlockSpec((None, None, IDX_DIM, page),
                            lambda b, j, pt, i=i: (layer, pt[b, j * pps + i], 0, 0))

    grid_spec = pltpu.PrefetchScalarGridSpec(
        num_scalar_prefetch=1,
        grid=(nseq, n_pages // pps),
        in_specs=[pl.BlockSpec((TOK_PAD, qiw), lambda b, j, pt: (b, COL_QI // qiw)),
                  pl.BlockSpec((TOK_PAD, LANES), lambda b, j, pt: (b, COL_KIW // LANES))]
                 + [page_spec(i) for i in range(pps)],
        out_specs=pl.BlockSpec((1, TOK_PAD, n_cols), lambda b, j, pt: (b, 0, 0)),
        scratch_shapes=[pltpu.VMEM((TOK_PAD, n_cols), F32)],
    )
    return pl.pallas_call(
        functools.partial(_idx_sample_kernel, pps=pps, page=page, n_pages=n_pages, n_top=n_top,
                          idx_bits=int(n_cols).bit_length(), n_tok=n_tok),
        out_shape=jax.ShapeDtypeStruct((nseq, TOK_PAD, n_cols), F32),
        grid_spec=grid_spec,
        compiler_params=_params(("parallel", "arbitrary")),
        name="idx_sample",
    )(page_table, ys, ys, *([cache] * pps))


def _dsa_sample_kernel(pt_ref, q0_ref, q1_ref, new_ref, bias_ref, *rest, pps, page, n_pages):
    page_refs = rest[:pps]
    o_ref = rest[pps]
    m_sc, l_sc, acc_sc = rest[pps + 1:]
    j = pl.program_id(1)
    q = _joint_queries([q0_ref[...], q1_ref[...]], HEAD_DIM ** -0.5)

    def visit(k, v, b8):
        bias = jnp.concatenate([b8] * (J_ROWS // TOK_PAD), axis=0)
        m, l, a = _softmax_block(q, k, v, bias, m_sc[...], l_sc[...], acc_sc[...])
        m_sc[...] = m
        l_sc[...] = l
        acc_sc[...] = a

    @pl.when(j == 0)
    def _():
        m_sc[...] = jnp.full(m_sc.shape, NEG_BIG, F32)
        l_sc[...] = jnp.zeros(l_sc.shape, F32)
        acc_sc[...] = jnp.zeros(acc_sc.shape, F32)
        visit(*_new_token_kv(new_ref, page), bias_ref[0, :, n_pages * page:])

    off = pl.multiple_of(j * (pps * page), pps * page)
    visit(*_gather_kv(page_refs, page), bias_ref[0, :, pl.ds(off, pps * page)])

    @pl.when(j == pl.num_programs(1) - 1)
    def _():
        _unstack_heads(o_ref, acc_sc[...] / l_sc[:, 0:1])


def _dsa_sample(page_table, ys, bias, cache, layer, pps):
    nseq, n_pages = page_table.shape
    rows = cache.shape[2]
    page = rows // KV_SLOTS
    n_cols = bias.shape[2]
    gw = GROUP * HEAD_DIM
    kvw = KV_SLOTS * HEAD_DIM

    def page_spec(i):
        return pl.BlockSpec((None, None, rows, HEAD_DIM),
                            lambda b, j, pt, i=i: (layer, pt[b, j * pps + i], 0, 0))

    grid_spec = pltpu.PrefetchScalarGridSpec(
        num_scalar_prefetch=1,
        grid=(nseq, n_pages // pps),
        in_specs=[pl.BlockSpec((TOK_PAD, gw), lambda b, j, pt: (b, COL_QB // gw)),
                  pl.BlockSpec((TOK_PAD, gw), lambda b, j, pt: (b, COL_QB // gw + 1)),
                  pl.BlockSpec((TOK_PAD, kvw), lambda b, j, pt: (b, COL_KVB // kvw)),
                  pl.BlockSpec((1, TOK_PAD, n_cols), lambda b, j, pt: (b, 0, 0))]
                 + [page_spec(i) for i in range(pps)],
        out_specs=pl.BlockSpec((TOK_PAD, N_HEADS * HEAD_DIM), lambda b, j, pt: (b, 0)),
        scratch_shapes=[pltpu.VMEM((J_ROWS, LANES), F32), pltpu.VMEM((J_ROWS, LANES), F32),
                        pltpu.VMEM((J_ROWS, J_LANES), F32)],
    )
    return pl.pallas_call(
        functools.partial(_dsa_sample_kernel, pps=pps, page=page, n_pages=n_pages),
        out_shape=jax.ShapeDtypeStruct((nseq * TOK_PAD, N_HEADS * HEAD_DIM), F32),
        grid_spec=grid_spec,
        compiler_params=_params(("parallel", "arbitrary")),
        name="dsa_sample",
    )(page_table, ys, ys, ys, bias, *([cache] * pps))


def _layer_norm(x, g, b):
    mu = jnp.mean(x, axis=-1, keepdims=True)
    xc = x - mu
    var = jnp.mean(xc * xc, axis=-1, keepdims=True)
    return xc * lax.rsqrt(var + LN_EPS) * g + b


def _merge_kernel(oa_ref, ob_ref, wa_ref, wb_ref, ga_ref, gb_ref, wo_ref, x_ref, g_ref, b_ref,
                  h_ref, hb_ref, acc_sc, *, alpha):
    kt = pl.program_id(1)

    @pl.when(kt == 0)
    def _():
        acc_sc[...] = jnp.zeros(acc_sc.shape, F32)

    oa = oa_ref[...].astype(BF16)
    ob = ob_ref[...].astype(BF16)
    mix = ga_ref[...] * _dot(oa, wa_ref[...]) + gb_ref[...] * _dot(ob, wb_ref[...])
    acc_sc[...] += _dot(mix.astype(BF16), wo_ref[...])

    @pl.when(kt == pl.num_programs(1) - 1)
    def _():
        h = _layer_norm(alpha * x_ref[...] + acc_sc[...], g_ref[...], b_ref[...])
        h_ref[...] = h
        hb_ref[...] = h.astype(BF16)


def _merge(oa, ob, y, x, wa, wb, wo, g, b, layer, alpha):
    rows = x.shape[0]
    tr = min(rows, 512)
    tk = 512
    kw = N_HEADS * HEAD_DIM
    return pl.pallas_call(
        functools.partial(_merge_kernel, alpha=alpha),
        out_shape=(jax.ShapeDtypeStruct((rows, D_MODEL), F32), jax.ShapeDtypeStruct((rows, D_MODEL), BF16)),
        grid=(rows // tr, D_MODEL // tk),
        in_specs=[pl.BlockSpec((tr, kw), lambda r, k: (r, 0)),
                  pl.BlockSpec((tr, kw), lambda r, k: (r, 0)),
                  pl.BlockSpec((None, kw, tk), lambda r, k: (layer, 0, k)),
                  pl.BlockSpec((None, kw, tk), lambda r, k: (layer, 0, k)),
                  pl.BlockSpec((tr, tk), lambda r, k: (r, COL_GA // tk + k)),
                  pl.BlockSpec((tr, tk), lambda r, k: (r, COL_GB // tk + k)),
                  pl.BlockSpec((None, tk, D_MODEL), lambda r, k: (layer, k, 0)),
                  pl.BlockSpec((tr, D_MODEL), lambda r, k: (r, 0)),
                  pl.BlockSpec((None, 1, D_MODEL), lambda r, k: (layer, 0, 0)),
                  pl.BlockSpec((None, 1, D_MODEL), lambda r, k: (layer, 0, 0))],
        out_specs=(pl.BlockSpec((tr, D_MODEL), lambda r, k: (r, 0)),
                   pl.BlockSpec((tr, D_MODEL), lambda r, k: (r, 0))),
        scratch_shapes=[pltpu.VMEM((tr, D_MODEL), F32)],
        compiler_params=_params(("parallel", "arbitrary")),
        name="merge_ln",
    )(oa, ob, wa, wb, y, y, wo, x, g, b)


def _gelu_tanh(x):
    return 0.5 * x * (1.0 + jnp.tanh(np.sqrt(2.0 / np.pi) * (x + 0.044715 * (x * x * x))))


HALO = 16


def _ffn_kernel(*refs, alpha, tr, seq_len, blocks_per_seq, prompt_mode):
    if prompt_mode:
        (hb_ref, halo_ref, w1a_ref, w1u_ref, cw_ref, cb_ref, w2_ref, h_ref, g_ref, b_ref,
         h2_ref, h2b_ref, a_ref, acc_sc) = refs
    else:
        (hb_ref, s1_ref, s2_ref, w1a_ref, w1u_ref, cw_ref, cb_ref, w2_ref, h_ref, g_ref, b_ref,
         h2_ref, h2b_ref, a_ref, acc_sc) = refs
    r = pl.program_id(0)
    ft = pl.program_id(1)

    @pl.when(ft == 0)
    def _():
        acc_sc[...] = jnp.zeros(acc_sc.shape, F32)

    hb = hb_ref[...]
    a = _dot(hb, w1a_ref[...])
    up = _dot(hb, w1u_ref[...])
    row = lax.broadcasted_iota(I32, a.shape, 0)
    p1 = pltpu.roll(a, 1, 0)
    p2 = pltpu.roll(a, 2, 0)
    if prompt_mode:
        a_halo = _dot(halo_ref[...], w1a_ref[...])
        keep = jnp.where(r % blocks_per_seq == 0, 0.0, 1.0)
        h6 = a_halo[HALO - 2:HALO - 1, :] * keep
        h7 = a_halo[HALO - 1:HALO, :] * keep
        p1 = jnp.where(row == 0, h7, p1)
        p2 = jnp.where(row == 0, h6, jnp.where(row == 1, h7, p2))
        a_ref[...] = a[tr - SUBLANES:, :]
    else:
        t = row & (seq_len - 1)
        p1 = jnp.where(t == 0, s1_ref[...], p1)
        p2 = jnp.where(t < 2, s2_ref[...], p2)
        a_ref[...] = a
    c = cb_ref[...] + cw_ref[0:1, :] * p2 + cw_ref[1:2, :] * p1 + cw_ref[2:3, :] * a
    hmid = (_gelu_tanh(c) * up).astype(BF16)
    acc_sc[...] += _dot(hmid, w2_ref[...])

    @pl.when(ft == pl.num_programs(1) - 1)
    def _():
        h2 = _layer_norm(alpha * h_ref[...] + acc_sc[...], g_ref[...], b_ref[...])
        h2_ref[...] = h2
        h2b_ref[...] = h2.astype(BF16)


def _ffn(h, hb, w1a, w1u, cw, cb, w2, g, b, layer, alpha, seq_len, state=None):
    rows = h.shape[0]
    prompt_mode = state is None
    tr = min(seq_len, 512) if prompt_mode else rows
    n_r = rows // tr
    n_f = D_FF_PAD // TF
    common_w = [pl.BlockSpec((None, D_MODEL, TF), lambda r, f: (layer, 0, f)),
                pl.BlockSpec((None, D_MODEL, TF), lambda r, f: (layer, 0, f)),
                pl.BlockSpec((None, SUBLANES, TF), lambda r, f: (layer, 0, f)),
                pl.BlockSpec((None, 1, TF), lambda r, f: (layer, 0, f)),
                pl.BlockSpec((None, TF, D_MODEL), lambda r, f: (layer, f, 0)),
                pl.BlockSpec((tr, D_MODEL), lambda r, f: (r, 0)),
                pl.BlockSpec((None, 1, D_MODEL), lambda r, f: (layer, 0, 0)),
                pl.BlockSpec((None, 1, D_MODEL), lambda r, f: (layer, 0, 0))]
    if prompt_mode:
        assert seq_len % tr == 0 and tr % HALO == 0
        per = tr // HALO
        extra_specs = [pl.BlockSpec((HALO, D_MODEL), lambda r, f: (jnp.maximum(r * per - 1, 0), 0))]
        extra = [hb]
        a_rows, a_blk = n_r * SUBLANES, SUBLANES
    else:
        assert seq_len & (seq_len - 1) == 0 and seq_len >= CONV_W - 1
        extra_specs = [pl.BlockSpec((tr, TF), lambda r, f: (0, f)),
                       pl.BlockSpec((tr, TF), lambda r, f: (0, f))]
        extra = list(state)
        a_rows, a_blk = rows, tr
    return pl.pallas_call(
        functools.partial(_ffn_kernel, alpha=alpha, tr=tr, seq_len=seq_len,
                          blocks_per_seq=max(seq_len // tr, 1), prompt_mode=prompt_mode),
        out_shape=(jax.ShapeDtypeStruct((rows, D_MODEL), F32), jax.ShapeDtypeStruct((rows, D_MODEL), BF16),
                   jax.ShapeDtypeStruct((a_rows, D_FF_PAD), F32)),
        grid=(n_r, n_f),
        in_specs=[pl.BlockSpec((tr, D_MODEL), lambda r, f: (r, 0))] + extra_specs + common_w,
        out_specs=(pl.BlockSpec((tr, D_MODEL), lambda r, f: (r, 0)),
                   pl.BlockSpec((tr, D_MODEL), lambda r, f: (r, 0)),
                   pl.BlockSpec((a_blk, TF), lambda r, f: (r, f))),
        scratch_shapes=[pltpu.VMEM((tr, D_MODEL), F32)],
        compiler_params=_params(("parallel", "arbitrary")),
        name="conv_ffn_ln",
    )(hb, *extra, w1a, w1u, cw, cb, w2, h, g, b)


def _ple_kernel(hb_ref, wg_ref, p_ref, wp_ref, h_ref, o_ref, ob_ref):
    gate = jax.nn.sigmoid(_dot(hb_ref[...], wg_ref[...]))
    out = h_ref[...] + gate * _dot(p_ref[...], wp_ref[...])
    o_ref[...] = out
    ob_ref[...] = out.astype(BF16)


def _ple(h2, h2b, pb, wg, wp, layer):
    rows = h2.shape[0]
    tr = min(rows, 1024)
    tn = 512
    return pl.pallas_call(
        _ple_kernel,
        out_shape=(jax.ShapeDtypeStruct((rows, D_MODEL), F32), jax.ShapeDtypeStruct((rows, D_MODEL), BF16)),
        grid=(rows // tr, D_MODEL // tn),
        in_specs=[pl.BlockSpec((tr, D_MODEL), lambda r, n: (r, 0)),
                  pl.BlockSpec((None, D_MODEL, tn), lambda r, n: (layer, 0, n)),
                  pl.BlockSpec((None, tr, PLE_DIM), lambda r, n: (layer, r, 0)),
                  pl.BlockSpec((None, PLE_DIM, tn), lambda r, n: (layer, 0, n)),
                  pl.BlockSpec((tr, tn), lambda r, n: (r, n))],
        out_specs=(pl.BlockSpec((tr, tn), lambda r, n: (r, n)),
                   pl.BlockSpec((tr, tn), lambda r, n: (r, n))),
        compiler_params=_params(("parallel", "arbitrary")),
        name="ple_gate",
    )(h2b, wg, pb, wp, h2)


IN_WIDTH = COL_KIW + IDX_DIM + IDX_HEADS + 2 * D_MODEL
PACK_ROWS = 128


SMALL_COLS = IDX_DIM + IDX_HEADS
KIW_TILE = COL_KIW // TN


def _pack_w_in_kernel(prev_ref, cur_ref, o_ref):
    n = pl.program_id(1)

    @pl.when(n < KIW_TILE)
    def _():
        o_ref[...] = cur_ref[...].T.astype(BF16)

    @pl.when(n == KIW_TILE)
    def _():
        t = jnp.concatenate([cur_ref[:SMALL_COLS, :], jnp.zeros((TN - SMALL_COLS, D_MODEL), F32)], axis=0)
        o_ref[...] = t.T.astype(BF16)

    @pl.when(n > KIW_TILE)
    def _():
        t = jnp.concatenate([prev_ref[SMALL_COLS:, :], cur_ref[:SMALL_COLS, :]], axis=0)
        o_ref[...] = t.T.astype(BF16)


def _pack_w_in(w_in):
    depth, rows, width = w_in.shape
    assert width == IN_WIDTH and rows == D_MODEL and SMALL_COLS % 16 == 0
    w_t = jnp.swapaxes(w_in, 1, 2)
    return pl.pallas_call(
        _pack_w_in_kernel,
        out_shape=jax.ShapeDtypeStruct((depth, rows, IN_COLS), BF16),
        grid=(depth, N_IN_TILES),
        in_specs=[pl.BlockSpec((None, TN, D_MODEL), lambda l, n: (l, jnp.maximum(n - 1, KIW_TILE), 0)),
                  pl.BlockSpec((None, TN, D_MODEL), lambda l, n: (l, n, 0))],
        out_specs=pl.BlockSpec((None, D_MODEL, TN), lambda l, n: (l, 0, n)),
        compiler_params=_params(("parallel", "arbitrary")),
        name="pack_w_in",
    )(w_t, w_t)


def _pack_ffn_in_kernel(w_ref, a_ref, u_ref):
    x = w_ref[...]
    zeros = jnp.zeros((x.shape[0], D_FF_PAD - D_FF), BF16)
    a_ref[:, :D_FF] = x[:, :D_FF].astype(BF16)
    a_ref[:, D_FF:] = zeros
    u_ref[:, :D_FF] = x[:, D_FF:].astype(BF16)
    u_ref[:, D_FF:] = zeros


def _pack_ffn_in(w_ffn_in):
    depth, rows, width = w_ffn_in.shape
    out = jax.ShapeDtypeStruct((depth, rows, D_FF_PAD), BF16)
    spec = pl.BlockSpec((None, PACK_ROWS, D_FF_PAD), lambda l, r: (l, r, 0))
    return pl.pallas_call(
        _pack_ffn_in_kernel,
        out_shape=(out, out),
        grid=(depth, rows // PACK_ROWS),
        in_specs=[pl.BlockSpec((None, PACK_ROWS, width), lambda l, r: (l, r, 0))],
        out_specs=(spec, spec),
        compiler_params=_params(("parallel", "parallel")),
        name="pack_ffn_in",
    )(w_ffn_in)


def _pack_ffn_out_kernel(w_ref, o_ref):
    row = pl.program_id(1) * TF + lax.broadcasted_iota(I32, w_ref.shape, 0)
    o_ref[...] = jnp.where(row < D_FF, w_ref[...], 0.0).astype(BF16)


def _pack_ffn_out(w_ffn_out):
    depth, _, cols = w_ffn_out.shape
    return pl.pallas_call(
        _pack_ffn_out_kernel,
        out_shape=jax.ShapeDtypeStruct((depth, D_FF_PAD, cols), BF16),
        grid=(depth, D_FF_PAD // TF),
        in_specs=[pl.BlockSpec((None, TF, cols), lambda l, r: (l, r, 0))],
        out_specs=pl.BlockSpec((None, TF, cols), lambda l, r: (l, r, 0)),
        compiler_params=_params(("parallel", "parallel")),
        name="pack_ffn_out",
    )(w_ffn_out)


def _rope_tables(pos):
    pos = pos.astype(F32)[:, None]

    def table(head_dim):
        rot = head_dim // 4
        half = rot // 2
        inv = ROPE_THETA ** (-(2.0 * jnp.arange(half, dtype=F32)) / rot)
        ang = pos * inv[None, :]
        cos, sin = jnp.cos(ang), jnp.sin(ang)
        ones = jnp.ones((pos.shape[0], head_dim - rot), F32)
        c = jnp.concatenate([cos, cos, ones], axis=1)
        s = jnp.concatenate([-sin, sin, 0.0 * ones], axis=1)
        reps = LANES // head_dim
        return jnp.tile(c, (1, reps)), jnp.tile(s, (1, reps))

    c128, s128 = table(HEAD_DIM)
    c64, s64 = table(IDX_DIM)
    return c128, s128, c64, s64


def kernel(x_prompt, x_sample, cache_sb_kv, cache_dsa_kv, cache_idx_k, state_ffn_conv, page_table,
           p_prompt, p_sample, w_in, w_branch_sb, w_branch_dsa, w_out, ln1_g, ln1_b, w_ffn_in,
           ffn_conv_w, ffn_conv_b, w_ffn_out, ln2_g, ln2_b, w_ple_gate, w_ple_proj):
    batch, seq = x_prompt.shape[:2]
    nseq, n_tok = x_sample.shape[:2]
    depth = w_in.shape[0]
    n_pool, page = cache_sb_kv.shape[1:3]
    n_pages = page_table.shape[1]
    past_len = n_pages * page
    alpha = (2 * depth) ** 0.25
    kv_w = 2 * N_KV * HEAD_DIM
    top_s = max(1, min(TOPK_MAX, (past_len + n_tok) // 4))
    pps = min(32, n_pages)
    pps_idx = min(64, n_pages)

    w_in_p = _pack_w_in(w_in)
    wa = w_branch_sb.astype(BF16)
    wb = w_branch_dsa.astype(BF16)
    wo = w_out.astype(BF16)
    ff_pad = D_FF_PAD - D_FF
    w1a, w1u = _pack_ffn_in(w_ffn_in)
    w2 = _pack_ffn_out(w_ffn_out)
    cw = jnp.pad(ffn_conv_w, ((0, 0), (0, SUBLANES - CONV_W), (0, ff_pad)))
    cb = jnp.pad(ffn_conv_b, ((0, 0), (0, ff_pad)))[:, None, :]
    wg = w_ple_gate.astype(BF16)
    wp = w_ple_proj.astype(BF16)
    g1, b1 = ln1_g[:, None, :], ln1_b[:, None, :]
    g2, b2 = ln2_g[:, None, :], ln2_b[:, None, :]

    tabs_p = _rope_tables(jnp.tile(jnp.arange(seq, dtype=jnp.int32), batch))
    tabs_s = _rope_tables(jnp.tile(past_len + jnp.arange(TOK_PAD, dtype=jnp.int32), nseq))

    sb_pages = cache_sb_kv.reshape(depth, n_pool, page * KV_SLOTS, HEAD_DIM)
    dsa_pages = cache_dsa_kv.reshape(depth, n_pool, page * KV_SLOTS, HEAD_DIM)
    idx_pages_t = jnp.swapaxes(cache_idx_k, 2, 3)

    assert n_tok <= TOK_PAD
    tok_pad = ((0, 0), (0, TOK_PAD - n_tok), (0, 0))
    xp = x_prompt.reshape(batch * seq, D_MODEL)
    xs = jnp.pad(x_sample, tok_pad).reshape(nseq * TOK_PAD, D_MODEL)
    xpb, xsb = xp.astype(BF16), xs.astype(BF16)
    ppb = p_prompt.reshape(depth, batch * seq, PLE_DIM).astype(BF16)
    psb = jnp.pad(p_sample, ((0, 0),) + tok_pad).reshape(depth, nseq * TOK_PAD, PLE_DIM).astype(BF16)
    st = jnp.pad(state_ffn_conv, ((0, 0), (0, 0), (0, 0), (0, ff_pad)))
    conv_s1 = jnp.repeat(st[:, :, 1], TOK_PAD, axis=1)
    conv_s2 = jnp.pad(st, ((0, 0), (0, 0), (0, TOK_PAD - (CONV_W - 1)), (0, 0))).reshape(
        depth, nseq * TOK_PAD, D_FF_PAD)

    def new_row_buffers(rows):
        kv_shape = (depth, rows * KV_SLOTS, HEAD_DIM)
        return jnp.zeros(kv_shape, F32), jnp.zeros(kv_shape, F32), jnp.zeros((depth, rows, LANES), F32)

    new_p = new_row_buffers(batch * seq)
    new_s = new_row_buffers(nseq * TOK_PAD)
    outs = {k: [] for k in ("conv_p", "conv_s")}
    for l in range(depth):
        y, *new_p = _project(xpb, w_in_p, l, tabs_p, new_p)
        oa = _sb_prompt(y, batch, seq)
        ob = _dsa_prompt(y, batch, seq)
        h, hb = _merge(oa, ob, y, xp, wa, wb, wo, g1, b1, l, alpha)
        h2, h2b, a_tail = _ffn(h, hb, w1a, w1u, cw, cb, w2, g2, b2, l, alpha, seq)
        xp, xpb = _ple(h2, h2b, ppb, wg, wp, l)
        tails = a_tail.reshape(batch, -1, SUBLANES, D_FF_PAD)[:, -1, SUBLANES - (CONV_W - 1):, :D_FF]
        outs["conv_p"].append(tails)

        ys, *new_s = _project(xsb, w_in_p, l, tabs_s, new_s)
        oa_s = _sb_sample(page_table, ys, sb_pages, l, pps)
        bias = _idx_sample(page_table, ys, idx_pages_t, l, pps_idx, top_s, n_tok)
        ob_s = _dsa_sample(page_table, ys, bias, dsa_pages, l, pps)
        hs, hsb = _merge(oa_s, ob_s, ys, xs, wa, wb, wo, g1, b1, l, alpha)
        h2s, h2sb, a_s = _ffn(hs, hsb, w1a, w1u, cw, cb, w2, g2, b2, l, alpha,
                              TOK_PAD, state=(conv_s1[l], conv_s2[l]))
        xs, xsb = _ple(h2s, h2sb, psb, wg, wp, l)
        outs["conv_s"].append(a_s.reshape(nseq, TOK_PAD, D_FF_PAD)[:, n_tok - (CONV_W - 1):n_tok, :D_FF])

    sb_p, dsa_p, kiw_p = new_p
    sb_s, dsa_s, kiw_s = [b.reshape(depth, nseq, TOK_PAD, -1)[:, :, :n_tok] for b in new_s]
    kv_shape = (2, N_KV, HEAD_DIM)
    return (xp.reshape(batch, seq, D_MODEL), xs.reshape(nseq, TOK_PAD, D_MODEL)[:, :n_tok],
            sb_p.reshape(depth, batch, seq, *kv_shape), dsa_p.reshape(depth, batch, seq, *kv_shape),
            kiw_p[..., :IDX_DIM].reshape(depth, batch, seq, IDX_DIM), jnp.stack(outs["conv_p"]),
            sb_s.reshape(depth, nseq, n_tok, *kv_shape), dsa_s.reshape(depth, nseq, n_tok, *kv_shape),
            kiw_s[..., :IDX_DIM], jnp.stack(outs["conv_s"]))
```

```python
import functools

import jax
import jax.numpy as jnp
import numpy as np
from jax import lax
from jax.experimental import pallas as pl
from jax.experimental.pallas import tpu as pltpu

F32 = jnp.float32
BF16 = jnp.bfloat16
I32 = jnp.int32

D_MODEL = 2048
HEAD_DIM = 128
N_HEADS = 8
N_KV = 2
GROUP = N_HEADS // N_KV
IDX_HEADS = 16
IDX_DIM = 64
TOPK_MAX = 256
ROPE_THETA = 500000.0
D_FF = 5504
CONV_W = 3
PLE_DIM = 256
LN_EPS = 1e-5

LANES = 128
SUBLANES = 8
VMEM_LIMIT = 56 * 1024 * 1024

TN = 512
COL_QA = 0
COL_KVA = 1024
COL_QB = 1536
COL_KVB = 2560
COL_QI = 3072
COL_KIW = 4096
COL_GA = 4608
COL_GB = 6656
IN_COLS = 8704
N_IN_TILES = IN_COLS // TN
D_FF_PAD = 5632
TF = 512

SB_Q_SCALE = (HEAD_DIM ** -0.5) * float(np.log2(np.e))
SB_STACK_ROWS = 64
NEG_BIG = -1e30
KEY_NEG_INF = np.int32(np.array(0xFF800000, dtype=np.uint32).view(np.int32) ^ 0x7FFFFFFF)
INT_MIN = np.int32(-2 ** 31)


def _params(sem, vmem_limit=VMEM_LIMIT):
    return pltpu.CompilerParams(dimension_semantics=sem, vmem_limit_bytes=vmem_limit)


def _dot_t(a, b):
    return lax.dot_general(a, b, (((1,), (1,)), ((), ())), preferred_element_type=F32)


def _dot(a, b):
    return jnp.dot(a, b, preferred_element_type=F32)


def _rope(y, c, s, half):
    w = y.shape[1]
    reps = w // LANES
    if reps > 1:
        c = jnp.concatenate([c] * reps, axis=1)
        s = jnp.concatenate([s] * reps, axis=1)
    lane = lax.broadcasted_iota(I32, y.shape, 1)
    first = (lane & (2 * half - 1)) < half
    partner = jnp.where(first, pltpu.roll(y, w - half, 1), pltpu.roll(y, half, 1))
    return y * c + partner * s


def _proj_kernel(x_ref, w_ref, c128_ref, s128_ref, c64_ref, s64_ref, sb_in, dsa_in, kiw_in,
                 y_ref, sb_ref, dsa_ref, kiw_ref):
    del sb_in, dsa_in, kiw_in
    n = pl.program_id(1)

    def product():
        return _dot(x_ref[...], w_ref[...])

    @pl.when(jnp.logical_and(n < COL_QB // TN, n != COL_KVA // TN))
    def _():
        y_ref[...] = product()

    def store_kv_rows(ref, tile):
        for slot in range(TN // HEAD_DIM):
            ref[pl.ds(slot, tile.shape[0], stride=TN // HEAD_DIM), :] = tile[:, slot * HEAD_DIM:(slot + 1) * HEAD_DIM]

    @pl.when(n == COL_KVA // TN)
    def _():
        y = product()
        y_ref[...] = y
        store_kv_rows(sb_ref, y)

    @pl.when(jnp.logical_and(n >= COL_QB // TN, n < COL_KVB // TN))
    def _():
        y_ref[...] = _rope(product(), c128_ref[...], s128_ref[...], 16)

    @pl.when(n == COL_KVB // TN)
    def _():
        y = product()
        k_rot = _rope(y[:, :256], c128_ref[...], s128_ref[...], 16)
        y_ref[:, :256] = k_rot
        y_ref[:, 256:] = y[:, 256:]
        store_kv_rows(dsa_ref, jnp.concatenate([k_rot, y[:, 256:]], axis=1))

    @pl.when(jnp.logical_and(n >= COL_QI // TN, n < COL_KIW // TN))
    def _():
        y_ref[...] = _rope(product(), c64_ref[...], s64_ref[...], 8)

    @pl.when(n == COL_KIW // TN)
    def _():
        y = product()
        lane = lax.broadcasted_iota(I32, c64_ref.shape, 1)
        c = jnp.where(lane < IDX_DIM, c64_ref[...], 1.0)
        s = jnp.where(lane < IDX_DIM, s64_ref[...], 0.0)
        kiw = _rope(y[:, :LANES], c, s, 8)
        y_ref[:, :LANES] = kiw
        y_ref[:, LANES:] = y[:, LANES:]
        kiw_ref[...] = kiw

    @pl.when(n >= COL_GA // TN)
    def _():
        y_ref[...] = jax.nn.sigmoid(product())


def _project(xb, w, layer, tabs, new_rows):
    rows = xb.shape[0]
    tr = min(rows, 1024)
    tab_spec = pl.BlockSpec((tr, LANES), lambda r, n: (r, 0))
    any_spec = pl.BlockSpec(memory_space=pl.ANY)
    kv_spec = pl.BlockSpec((None, tr * (TN // HEAD_DIM), HEAD_DIM), lambda r, n: (layer, r, 0))
    return pl.pallas_call(
        _proj_kernel,
        out_shape=(jax.ShapeDtypeStruct((rows, IN_COLS), F32),)
                  + tuple(jax.ShapeDtypeStruct(b.shape, b.dtype) for b in new_rows),
        grid=(rows // tr, N_IN_TILES),
        in_specs=[pl.BlockSpec((tr, D_MODEL), lambda r, n: (r, 0)),
                  pl.BlockSpec((None, D_MODEL, TN), lambda r, n: (layer, 0, n)),
                  tab_spec, tab_spec, tab_spec, tab_spec, any_spec, any_spec, any_spec],
        out_specs=(pl.BlockSpec((tr, TN), lambda r, n: (r, n)), kv_spec, kv_spec,
                   pl.BlockSpec((None, tr, LANES), lambda r, n: (layer, r, 0))),
        input_output_aliases={6: 1, 7: 2, 8: 3},
        compiler_params=_params(("parallel", "arbitrary")),
        name="in_proj",
    )(xb, w, *tabs, *new_rows)


def _suffix_matrix():
    j = lax.broadcasted_iota(I32, (2 * LANES, 2 * LANES), 0) & (LANES - 1)
    s = lax.broadcasted_iota(I32, (2 * LANES, 2 * LANES), 1)
    return jnp.where(jnp.logical_or(j > s, s >= LANES), -1.0, 0.0).astype(BF16)


def _sb_scores(q, k, mask):
    n_sub = k.shape[0] // LANES
    z = _dot_t(q, k)
    neg_abs = pltpu.bitcast(pltpu.bitcast(z, I32) | INT_MIN, F32)
    sp = jnp.maximum(z, 0.0) + jnp.log2(1.0 + jnp.exp2(neg_abs))
    spm = sp if mask is None else jnp.where(mask, sp, 0.0)
    hi = spm.astype(BF16)
    lo = (spm - hi.astype(F32)).astype(BF16)
    subs = [jnp.concatenate([hi[:, i * LANES:(i + 1) * LANES], lo[:, i * LANES:(i + 1) * LANES]], axis=1)
            for i in range(n_sub)]
    return z - sp, subs


def _sb_weights(d, subs, v, u, carry, mask):
    n_sub = len(subs)
    m_rows = d.shape[0]
    if m_rows <= SB_STACK_ROWS:
        r_all = _dot(subs[0] if n_sub == 1 else jnp.concatenate(subs, axis=0), u)
        rs = [r_all[i * m_rows:(i + 1) * m_rows] for i in range(n_sub)]
    else:
        rs = [_dot(s, u) for s in subs]
    afters = [None] * n_sub
    for i in reversed(range(n_sub)):
        afters[i] = rs[i][:, :LANES] + carry
        carry = carry + rs[i][:, LANES:]
    after = afters[0] if n_sub == 1 else jnp.concatenate(afters, axis=1)
    w = jnp.exp2(d + after)
    if mask is not None:
        w = jnp.where(mask, w, 0.0)
    return _dot(w.astype(BF16), v), carry


def _sb_block(q, k, v, u, carry, mask):
    d, subs = _sb_scores(q, k, mask)
    return _sb_weights(d, subs, v, u, carry, mask)


def _softmax_block(q, k, v, bias, m_old, l_old, acc_old):
    logit = _dot_t(q, k) + bias
    m_new = jnp.maximum(m_old, jnp.max(logit, axis=1, keepdims=True))
    p = jnp.exp(logit - m_new[:, 0:1])
    alpha = jnp.exp(m_old - m_new)
    l_new = alpha * l_old + jnp.sum(p, axis=1, keepdims=True)
    acc_new = alpha[:, 0:1] * acc_old + _dot(p.astype(BF16), v)
    return m_new, l_new, acc_new


def _sortable(x):
    b = pltpu.bitcast(x, I32)
    return jnp.where(b < 0, b ^ jnp.int32(0x7FFFFFFF), b)


def _kth_largest(count_ge, shape, k, n_total):
    bits_per_check = 4

    def cond(state):
        i, _, cnt = state
        return jnp.logical_and(i < 32, jnp.max(cnt) > k)

    def body(state):
        i, t, cnt = state
        for _ in range(bits_per_check):
            cand = t + jnp.left_shift(jnp.int32(1), jnp.int32(31) - i)
            c = count_ge(cand)
            ok = c >= k
            i, t, cnt = i + 1, jnp.where(ok, cand, t), jnp.where(ok, c, cnt)
        return i, t, cnt

    state = (jnp.int32(0), jnp.full(shape, INT_MIN, I32), jnp.full(shape, n_total, I32))
    _, t, cnt = lax.while_loop(cond, body, state)
    return t, cnt


def _tie_cutoff(count_tie_below, shape, need, n_bits):
    def body(i, j):
        cand = j + jnp.left_shift(jnp.int32(1), jnp.int32(n_bits - 1) - i)
        return jnp.where(count_tie_below(cand) < need, cand, j)

    return lax.fori_loop(0, n_bits, body, jnp.zeros(shape, I32))


def _sb_prompt_kernel(q_ref, k_ref, v_ref, o_ref, kb_sc, vb_sc, carry_sc, acc_sc, *, tq):
    iq = pl.program_id(1)

    @pl.when(iq == 0)
    def _():
        kb_sc[...] = k_ref[...].astype(BF16)
        vb_sc[...] = v_ref[...].astype(BF16)

    scale = SB_Q_SCALE
    u = _suffix_matrix()
    n_chain = N_HEADS // 2
    c_rows = 2 * tq
    qs = []
    for c in range(n_chain):
        q = jnp.concatenate([q_ref[:, h * HEAD_DIM:(h + 1) * HEAD_DIM] for h in (2 * c, 2 * c + 1)], axis=0)
        qs.append((q * scale).astype(BF16))

    def kv_block(o2, c):
        g = (2 * c) // GROUP
        return (kb_sc[pl.ds(o2, tq), g * HEAD_DIM:(g + 1) * HEAD_DIM],
                vb_sc[pl.ds(o2, tq), g * HEAD_DIM:(g + 1) * HEAD_DIM])

    row = lax.broadcasted_iota(I32, (c_rows, tq), 0) & (tq - 1)
    col = lax.broadcasted_iota(I32, (c_rows, tq), 1)
    off = pl.multiple_of(iq * tq, tq)
    for c in range(n_chain):
        k, v = kv_block(off, c)
        contrib, carry = _sb_block(qs[c], k, v, u, jnp.zeros((c_rows, LANES), F32), col < row)
        acc_sc[c] = contrib
        carry_sc[c] = carry

    def body(i, _):
        o2 = pl.multiple_of((iq - 1 - i) * tq, tq)
        for c in range(n_chain):
            k, v = kv_block(o2, c)
            contrib, carry = _sb_block(qs[c], k, v, u, carry_sc[c], None)
            acc_sc[c] += contrib
            carry_sc[c] = carry
        return 0

    lax.fori_loop(0, iq, body, 0)
    for h in range(N_HEADS):
        r0 = (h % 2) * tq
        o_ref[:, h * HEAD_DIM:(h + 1) * HEAD_DIM] = acc_sc[h // 2, r0:r0 + tq, :].astype(o_ref.dtype)


def _sb_prompt(y, batch, seq):
    tq = min(2 * LANES, seq)
    nq = seq // tq
    qw = N_HEADS * HEAD_DIM
    kvw = N_KV * HEAD_DIM
    return pl.pallas_call(
        functools.partial(_sb_prompt_kernel, tq=tq),
        out_shape=jax.ShapeDtypeStruct((batch * seq, qw), BF16),
        grid=(batch, nq),
        in_specs=[pl.BlockSpec((tq, qw), lambda b, i: (b * nq + i, COL_QA // qw)),
                  pl.BlockSpec((seq, kvw), lambda b, i: (b, COL_KVA // kvw)),
                  pl.BlockSpec((seq, kvw), lambda b, i: (b, COL_KVA // kvw + 1))],
        out_specs=pl.BlockSpec((tq, qw), lambda b, i: (b * nq + i, 0)),
        scratch_shapes=[pltpu.VMEM((seq, kvw), BF16), pltpu.VMEM((seq, kvw), BF16),
                        pltpu.VMEM((N_HEADS // 2, 2 * tq, LANES), F32),
                        pltpu.VMEM((N_HEADS // 2, 2 * tq, LANES), F32)],
        compiler_params=_params(("parallel", "arbitrary")),
        name="sb_prompt",
    )(y, y, y)


TKI = 256
LOGIT_SAFE = 60.0
NORM_SLACK = 1.02


def _fold_rows(x, op):
    return op(x.reshape(x.shape[0] // SUBLANES, SUBLANES, x.shape[1]), axis=0)


def _dsa_prompt_kernel(qb0_ref, qb1_ref, k_ref, v_ref, qi_ref, kiw_all_ref, kiw_q_ref, o_ref,
                       kb_sc, vb_sc, ki2_sc, key_sc, bias_sc, cut_sc, kn_sc, m_sc, l_sc, acc_sc,
                       *, tq, n_top, idx_bits):
    iq = pl.program_id(1)
    seq = k_ref.shape[0]

    @pl.when(iq == 0)
    def _():
        kb_sc[...] = k_ref[...].astype(BF16)
        vb_sc[...] = v_ref[...].astype(BF16)
        k_sq = k_ref[...] * k_ref[...]
        k_sq_max = jnp.float32(0.0)
        for g in range(N_KV):
            norms = jnp.sum(k_sq[:, g * HEAD_DIM:(g + 1) * HEAD_DIM], axis=1, keepdims=True)
            k_sq_max = jnp.maximum(k_sq_max, jnp.max(norms))
        kn_sc[...] = jnp.zeros(kn_sc.shape, F32) + k_sq_max
        lane = lax.broadcasted_iota(I32, (seq, LANES), 1)
        kia = jnp.where(lane < IDX_DIM, kiw_all_ref[...], 0.0)
        ki2_sc[0] = kia.astype(BF16)
        ki2_sc[1] = pltpu.roll(kia, IDX_DIM, 1).astype(BF16)

    w_t = (kiw_q_ref[...] * ((IDX_DIM ** -0.5) * (IDX_HEADS ** -0.5))).T
    w_rows = [w_t[IDX_DIM + h:IDX_DIM + h + 1, :] for h in range(IDX_HEADS)]

    q_pairs = [qi_ref[:, p * LANES:(p + 1) * LANES].astype(BF16) for p in range(IDX_HEADS // 2)]
    q_pos = iq * tq + lax.broadcasted_iota(I32, (1, tq), 1)
    n_blk = (iq * tq + tq + TKI - 1) // TKI

    def idx_body(j, _):
        off = pl.multiple_of(j * TKI, TKI)
        ki_e = ki2_sc[0, pl.ds(off, TKI), :]
        ki_o = ki2_sc[1, pl.ds(off, TKI), :]
        acc = jnp.zeros((TKI, tq), F32)
        for p in range(IDX_HEADS // 2):
            acc = acc + jnp.maximum(_dot_t(ki_e, q_pairs[p]), 0.0) * w_rows[2 * p]
            acc = acc + jnp.maximum(_dot_t(ki_o, q_pairs[p]), 0.0) * w_rows[2 * p + 1]
        k_pos = off + lax.broadcasted_iota(I32, (TKI, 1), 0)
        key_sc[pl.ds(off, TKI), :] = jnp.where(k_pos <= q_pos, _sortable(acc), KEY_NEG_INF)
        return 0

    lax.fori_loop(0, n_blk, idx_body, 0)

    def query_counts(hit_fn):
        def body(j, c):
            off = pl.multiple_of(j * TKI, TKI)
            return c + _fold_rows(hit_fn(key_sc[pl.ds(off, TKI), :], off), jnp.sum)

        c = lax.fori_loop(0, n_blk, body, jnp.zeros((SUBLANES, tq), I32))
        return jnp.sum(c, axis=0, keepdims=True)

    def count_ge(t):
        return query_counts(lambda kb, off: jnp.where(kb >= t, 1, 0))

    thr, n_ge = _kth_largest(count_ge, (1, tq), n_top, n_blk * TKI)

    cut_sc[...] = jnp.full(cut_sc.shape, seq, I32)

    @pl.when(jnp.max(n_ge) > n_top)
    def _():
        need = n_top - count_ge(thr + 1)

        def count_tie_below(jc):
            def hit(kb, off):
                idx = off + lax.broadcasted_iota(I32, kb.shape, 0)
                return jnp.where(kb == thr, jnp.where(idx < jc, 1, 0), 0)
            return query_counts(hit)

        cut = _tie_cutoff(count_tie_below, (1, tq), need, idx_bits)
        cut_sc[...] = jnp.broadcast_to(cut, cut_sc.shape)

    cut = cut_sc[0:1, :]

    def bias_body(j, _):
        off = pl.multiple_of(j * TKI, TKI)
        kb = key_sc[pl.ds(off, TKI), :]
        idx = off + lax.broadcasted_iota(I32, (TKI, tq), 0)
        tie = jnp.where(kb == thr, jnp.where(idx <= cut, 0.0, -jnp.inf), -jnp.inf)
        sel = jnp.where(kb > thr, 0.0, tie)
        sel = jnp.where(kb > KEY_NEG_INF, sel, -jnp.inf)
        bias_sc[:, pl.ds(off, TKI)] = sel.T
        return 0

    lax.fori_loop(0, n_blk, bias_body, 0)

    scale = (HEAD_DIM ** -0.5) * np.log2(np.e)
    n_chain = N_HEADS // 2
    n_sub = TKI // LANES
    qs = []
    ones = jnp.ones((HEAD_DIM, LANES), BF16)
    q_sq = jnp.zeros((SUBLANES, LANES), F32)
    for c in range(n_chain):
        q_ref = (qb0_ref, qb1_ref)[c // 2]
        hs = (2 * (c % 2), 2 * (c % 2) + 1)
        q = jnp.concatenate([q_ref[:, h * HEAD_DIM:(h + 1) * HEAD_DIM] for h in hs], axis=0) * scale
        q_sq = jnp.maximum(q_sq, _fold_rows(_dot((q * q).astype(BF16), ones), jnp.max))
        qs.append(q.astype(BF16))
    q_sq_max = jnp.max(q_sq) * NORM_SLACK

    def bias_rows(off):
        b = bias_sc[:, pl.ds(off, TKI)]
        return jnp.concatenate([b, b], axis=0)

    small_logits = q_sq_max * jnp.max(kn_sc[...]) <= LOGIT_SAFE * LOGIT_SAFE

    @pl.when(small_logits)
    def _():
        m_sc[...] = jnp.zeros(m_sc.shape, F32)

    @pl.when(jnp.logical_not(small_logits))
    def _():
        m_sc[...] = jnp.full(m_sc.shape, -jnp.inf, F32)

        def max_body(j, _):
            off = pl.multiple_of(j * TKI, TKI)
            bias = bias_rows(off)
            for c in range(n_chain):
                g = c // 2
                lg = _dot_t(qs[c], kb_sc[pl.ds(off, TKI), g * HEAD_DIM:(g + 1) * HEAD_DIM]) + bias
                m = m_sc[c]
                for i in range(n_sub):
                    m = jnp.maximum(m, lg[:, i * LANES:(i + 1) * LANES])
                m_sc[c] = m
            return 0

        lax.fori_loop(0, n_blk, max_body, 0)
        for c in range(n_chain):
            m_sc[c] = jnp.broadcast_to(jnp.max(m_sc[c], axis=1, keepdims=True), (2 * tq, LANES))

    l_sc[...] = jnp.zeros(l_sc.shape, F32)
    acc_sc[...] = jnp.zeros(acc_sc.shape, F32)

    def sum_body(j, _):
        off = pl.multiple_of(j * TKI, TKI)
        bias = bias_rows(off)
        for c in range(n_chain):
            g = c // 2
            lg = _dot_t(qs[c], kb_sc[pl.ds(off, TKI), g * HEAD_DIM:(g + 1) * HEAD_DIM]) + bias
            p = jnp.exp2(lg - jnp.concatenate([m_sc[c]] * n_sub, axis=1))
            l = l_sc[c]
            for i in range(n_sub):
                l = l + p[:, i * LANES:(i + 1) * LANES]
            l_sc[c] = l
            acc_sc[c] += _dot(p.astype(BF16), vb_sc[pl.ds(off, TKI), g * HEAD_DIM:(g + 1) * HEAD_DIM])
        return 0

    lax.fori_loop(0, n_blk, sum_body, 0)
    for c in range(n_chain):
        out = acc_sc[c] / jnp.sum(l_sc[c], axis=1, keepdims=True)
        for hh in range(2):
            c0 = (2 * c + hh) * HEAD_DIM
            o_ref[:, c0:c0 + HEAD_DIM] = out[hh * tq:(hh + 1) * tq, :].astype(o_ref.dtype)


def _dsa_prompt(y, batch, seq):
    tq = min(2 * LANES, seq)
    assert tq == TKI or seq == tq
    nq = seq // tq
    gw = GROUP * HEAD_DIM
    kvw = N_KV * HEAD_DIM
    n_top = max(1, min(TOPK_MAX, seq // 4))
    idx_bits = int(seq).bit_length()
    q_rows = 2 * tq
    return pl.pallas_call(
        functools.partial(_dsa_prompt_kernel, tq=tq, n_top=n_top, idx_bits=idx_bits),
        out_shape=jax.ShapeDtypeStruct((batch * seq, N_HEADS * HEAD_DIM), BF16),
        grid=(batch, nq),
        in_specs=[pl.BlockSpec((tq, gw), lambda b, i: (b * nq + i, COL_QB // gw)),
                  pl.BlockSpec((tq, gw), lambda b, i: (b * nq + i, COL_QB // gw + 1)),
                  pl.BlockSpec((seq, kvw), lambda b, i: (b, COL_KVB // kvw)),
                  pl.BlockSpec((seq, kvw), lambda b, i: (b, COL_KVB // kvw + 1)),
                  pl.BlockSpec((tq, IDX_HEADS * IDX_DIM), lambda b, i: (b * nq + i, COL_QI // (IDX_HEADS * IDX_DIM))),
                  pl.BlockSpec((seq, LANES), lambda b, i: (b, COL_KIW // LANES)),
                  pl.BlockSpec((tq, LANES), lambda b, i: (b * nq + i, COL_KIW // LANES))],
        out_specs=pl.BlockSpec((tq, N_HEADS * HEAD_DIM), lambda b, i: (b * nq + i, 0)),
        scratch_shapes=[pltpu.VMEM((seq, kvw), BF16),
                        pltpu.VMEM((seq, kvw), BF16),
                        pltpu.VMEM((2, seq, LANES), BF16),
                        pltpu.VMEM((seq, tq), I32),
                        pltpu.VMEM((tq, seq), F32),
                        pltpu.VMEM((SUBLANES, tq), I32),
                        pltpu.VMEM((SUBLANES, LANES), F32),
                        pltpu.VMEM((N_HEADS // 2, q_rows, LANES), F32),
                        pltpu.VMEM((N_HEADS // 2, q_rows, LANES), F32),
                        pltpu.VMEM((N_HEADS // 2, q_rows, HEAD_DIM), F32)],
        compiler_params=_params(("parallel", "arbitrary")),
        name="dsa_prompt",
    )(y, y, y, y, y, y, y)


TOK_PAD = SUBLANES
S_ROWS = GROUP * TOK_PAD


KV_SLOTS = 2 * N_KV


def _page_rows(ref, slot, page):
    return ref[pl.ds(slot, page, stride=KV_SLOTS), :].astype(BF16)


J_ROWS = N_KV * S_ROWS
J_LANES = N_KV * HEAD_DIM


def _gather_kv(page_refs, page):
    k = [jnp.concatenate([_page_rows(r, g, page) for g in range(N_KV)], axis=1) for r in page_refs]
    v = [jnp.concatenate([_page_rows(r, N_KV + g, page) for g in range(N_KV)], axis=1) for r in page_refs]
    if len(page_refs) == 1:
        return k[0], v[0]
    return jnp.concatenate(k, axis=0), jnp.concatenate(v, axis=0)


def _joint_queries(q_blks, scale):
    rows = []
    for g, q_blk in enumerate(q_blks):
        q = jnp.concatenate([q_blk[:, h * HEAD_DIM:(h + 1) * HEAD_DIM] for h in range(GROUP)], axis=0) * scale
        zero = jnp.zeros_like(q)
        rows.append(jnp.concatenate([q if gg == g else zero for gg in range(N_KV)], axis=1))
    return jnp.concatenate(rows, axis=0).astype(BF16)


def _new_token_kv(new_ref, page):
    pad = jnp.zeros((page - TOK_PAD, J_LANES), F32)
    k = jnp.concatenate([new_ref[:, :J_LANES], pad], axis=0)
    v = jnp.concatenate([new_ref[:, J_LANES:], pad], axis=0)
    return k.astype(BF16), v.astype(BF16)


def _unstack_heads(o_ref, acc):
    for g in range(N_KV):
        for h in range(GROUP):
            r0 = g * S_ROWS + h * TOK_PAD
            c0 = (g * GROUP + h) * HEAD_DIM
            o_ref[:, c0:c0 + HEAD_DIM] = acc[r0:r0 + TOK_PAD, g * HEAD_DIM:(g + 1) * HEAD_DIM]


def _sb_sample_kernel(pt_ref, q_ref, new_ref, *rest, pps, page):
    page_refs = rest[:pps]
    o_ref = rest[pps]
    carry_sc, acc_sc = rest[pps + 1:]
    j = pl.program_id(1)
    u = _suffix_matrix()
    gw = GROUP * HEAD_DIM
    q = _joint_queries([q_ref[:, g * gw:(g + 1) * gw] for g in range(N_KV)], SB_Q_SCALE)

    def visit(k, v, mask):
        c, cr = _sb_block(q, k, v, u, carry_sc[...], mask)
        acc_sc[...] += c
        carry_sc[...] = cr

    @pl.when(j == 0)
    def _():
        carry_sc[...] = jnp.zeros(carry_sc.shape, F32)
        acc_sc[...] = jnp.zeros(acc_sc.shape, F32)
        tok = lax.broadcasted_iota(I32, (J_ROWS, page), 0) & (TOK_PAD - 1)
        col = lax.broadcasted_iota(I32, (J_ROWS, page), 1)
        visit(*_new_token_kv(new_ref, page), col < tok)

    visit(*_gather_kv(page_refs, page), None)

    @pl.when(j == pl.num_programs(1) - 1)
    def _():
        _unstack_heads(o_ref, acc_sc[...])


def _sb_sample(page_table, ys, cache, layer, pps):
    nseq, n_pages = page_table.shape
    rows = cache.shape[2]
    page = rows // KV_SLOTS
    n_steps = n_pages // pps
    qw = N_HEADS * HEAD_DIM
    kvw = KV_SLOTS * HEAD_DIM

    def page_spec(i):
        return pl.BlockSpec((None, None, rows, HEAD_DIM),
                            lambda b, j, pt, i=i: (layer, pt[b, (n_steps - 1 - j) * pps + i], 0, 0))

    grid_spec = pltpu.PrefetchScalarGridSpec(
        num_scalar_prefetch=1,
        grid=(nseq, n_steps),
        in_specs=[pl.BlockSpec((TOK_PAD, qw), lambda b, j, pt: (b, COL_QA // qw)),
                  pl.BlockSpec((TOK_PAD, kvw), lambda b, j, pt: (b, COL_KVA // kvw))]
                 + [page_spec(i) for i in range(pps)],
        out_specs=pl.BlockSpec((TOK_PAD, qw), lambda b, j, pt: (b, 0)),
        scratch_shapes=[pltpu.VMEM((J_ROWS, LANES), F32), pltpu.VMEM((J_ROWS, J_LANES), F32)],
    )
    return pl.pallas_call(
        functools.partial(_sb_sample_kernel, pps=pps, page=page),
        out_shape=jax.ShapeDtypeStruct((nseq * TOK_PAD, qw), F32),
        grid_spec=grid_spec,
        compiler_params=_params(("parallel", "arbitrary")),
        name="sb_sample",
    )(page_table, ys, ys, *([cache] * pps))


def _idx_sample_kernel(pt_ref, qi_ref, kiw_ref, *rest, pps, page, n_pages, n_top, idx_bits, n_tok):
    page_refs = rest[:pps]
    bias_ref = rest[pps]
    score_sc = rest[pps + 1]
    j = pl.program_id(1)

    row = lax.broadcasted_iota(I32, (TOK_PAD, 1), 0)

    def real_rows(x):
        return jnp.where(row < n_tok, x, pltpu.roll(x, n_tok, 0))

    qi_rows = real_rows(qi_ref[...])
    kiw = real_rows(kiw_ref[...])
    qi = jnp.concatenate([qi_rows[:, h * IDX_DIM:(h + 1) * IDX_DIM] for h in range(IDX_HEADS)],
                         axis=0).astype(BF16)
    w_scale = (IDX_DIM ** -0.5) * (IDX_HEADS ** -0.5)
    wm = jnp.concatenate([jnp.broadcast_to(kiw[:, IDX_DIM + h:IDX_DIM + h + 1] * w_scale, (TOK_PAD, LANES))
                          for h in range(IDX_HEADS)], axis=0)

    def head_sum(s):
        n = s.shape[1]
        w = wm if n == LANES else jnp.concatenate([wm] * (n // LANES), axis=1)
        return jnp.sum((jnp.maximum(s, 0.0) * w).reshape(IDX_HEADS, TOK_PAD, n), axis=0)

    @pl.when(j == 0)
    def _():
        ki_new = jnp.concatenate([kiw_ref[:, :IDX_DIM], jnp.zeros((page - TOK_PAD, IDX_DIM), F32)], axis=0)
        tok = lax.broadcasted_iota(I32, (TOK_PAD, page), 0) & (n_tok - 1)
        col = lax.broadcasted_iota(I32, (TOK_PAD, page), 1)
        s_new = head_sum(_dot_t(qi, ki_new.astype(BF16)))
        score_sc[:, n_pages * page:] = jnp.where(col <= tok, s_new, -jnp.inf)

    off = pl.multiple_of(j * (pps * page), pps * page)
    ki_t = jnp.concatenate([r[...] for r in page_refs], axis=1)
    score_sc[:, pl.ds(off, pps * page)] = head_sum(_dot(qi, ki_t.astype(BF16)))

    @pl.when(j == pl.num_programs(1) - 1)
    def _():
        key = _sortable(score_sc[...])
        idx = lax.broadcasted_iota(I32, key.shape, 1)

        def count_ge(t):
            return jnp.sum(jnp.where(key >= t, 1, 0), axis=1, keepdims=True)

        thr, n_ge = _kth_largest(count_ge, (TOK_PAD, 1), n_top, key.shape[1])

        def search_cut():
            need = n_top - count_ge(thr + 1)

            def count_tie_below(jc):
                return jnp.sum(jnp.where(key == thr, jnp.where(idx < jc, 1, 0), 0), axis=1, keepdims=True)

            return _tie_cutoff(count_tie_below, (TOK_PAD, 1), need, idx_bits)

        cut = lax.cond(jnp.max(n_ge) > n_top, search_cut,
                       lambda: jnp.full((TOK_PAD, 1), key.shape[1], I32))
        tie = jnp.where(key == thr, jnp.where(idx <= cut, 0.0, -jnp.inf), -jnp.inf)
        sel = jnp.where(key > thr, 0.0, tie)
        bias_ref[0] = jnp.where(key > KEY_NEG_INF, sel, -jnp.inf)


def _idx_sample(page_table, ys, cache_t, layer, pps, n_top, n_tok):
    assert n_tok & (n_tok - 1) == 0 and TOK_PAD % n_tok == 0
    nseq, n_pages = page_table.shape
    page = cache_t.shape[3]
    n_cols = (n_pages + 1) * page
    cache = cache_t
    qiw = IDX_HEADS * IDX_DIM

    def page_spec(i):
        return pl.BlockSpec((None, None, IDX_DIM, page),
                            lambda b, j, pt, i=i: (layer, pt[b, j * pps + i], 0, 0))

    grid_spec = pltpu.PrefetchScalarGridSpec(
        num_scalar_prefetch=1,
        grid=(nseq, n_pages // pps),
        in_specs=[pl.BlockSpec((TOK_PAD, qiw), lambda b, j, pt: (b, COL_QI // qiw)),
                  pl.BlockSpec((TOK_PAD, LANES), lambda b, j, pt: (b, COL_KIW // LANES))]
                 + [page_spec(i) for i in range(pps)],
        out_specs=pl.BlockSpec((1, TOK_PAD, n_cols), lambda b, j, pt: (b, 0, 0)),
        scratch_shapes=[pltpu.VMEM((TOK_PAD, n_cols), F32)],
    )
    return pl.pallas_call(
        functools.partial(_idx_sample_kernel, pps=pps, page=page, n_pages=n_pages, n_top=n_top,
                          idx_bits=int(n_cols).bit_length(), n_tok=n_tok),
        out_shape=jax.ShapeDtypeStruct((nseq, TOK_PAD, n_cols), F32),
        grid_spec=grid_spec,
        compiler_params=_params(("parallel", "arbitrary")),
        name="idx_sample",
    )(page_table, ys, ys, *([cache] * pps))


def _dsa_sample_kernel(pt_ref, q0_ref, q1_ref, new_ref, bias_ref, *rest, pps, page, n_pages):
    page_refs = rest[:pps]
    o_ref = rest[pps]
    m_sc, l_sc, acc_sc = rest[pps + 1:]
    j = pl.program_id(1)
    q = _joint_queries([q0_ref[...], q1_ref[...]], HEAD_DIM ** -0.5)

    def visit(k, v, b8):
        bias = jnp.concatenate([b8] * (J_ROWS // TOK_PAD), axis=0)
        m, l, a = _softmax_block(q, k, v, bias, m_sc[...], l_sc[...], acc_sc[...])
        m_sc[...] = m
        l_sc[...] = l
        acc_sc[...] = a

    @pl.when(j == 0)
    def _():
        m_sc[...] = jnp.full(m_sc.shape, NEG_BIG, F32)
        l_sc[...] = jnp.zeros(l_sc.shape, F32)
        acc_sc[...] = jnp.zeros(acc_sc.shape, F32)
        visit(*_new_token_kv(new_ref, page), bias_ref[0, :, n_pages * page:])

    off = pl.multiple_of(j * (pps * page), pps * page)
    visit(*_gather_kv(page_refs, page), bias_ref[0, :, pl.ds(off, pps * page)])

    @pl.when(j == pl.num_programs(1) - 1)
    def _():
        _unstack_heads(o_ref, acc_sc[...] / l_sc[:, 0:1])


def _dsa_sample(page_table, ys, bias, cache, layer, pps):
    nseq, n_pages = page_table.shape
    rows = cache.shape[2]
    page = rows // KV_SLOTS
    n_cols = bias.shape[2]
    gw = GROUP * HEAD_DIM
    kvw = KV_SLOTS * HEAD_DIM

    def page_spec(i):
        return pl.BlockSpec((None, None, rows, HEAD_DIM),
                            lambda b, j, pt, i=i: (layer, pt[b, j * pps + i], 0, 0))

    grid_spec = pltpu.PrefetchScalarGridSpec(
        num_scalar_prefetch=1,
        grid=(nseq, n_pages // pps),
        in_specs=[pl.BlockSpec((TOK_PAD, gw), lambda b, j, pt: (b, COL_QB // gw)),
                  pl.BlockSpec((TOK_PAD, gw), lambda b, j, pt: (b, COL_QB // gw + 1)),
                  pl.BlockSpec((TOK_PAD, kvw), lambda b, j, pt: (b, COL_KVB // kvw)),
                  pl.BlockSpec((1, TOK_PAD, n_cols), lambda b, j, pt: (b, 0, 0))]
                 + [page_spec(i) for i in range(pps)],
        out_specs=pl.BlockSpec((TOK_PAD, N_HEADS * HEAD_DIM), lambda b, j, pt: (b, 0)),
        scratch_shapes=[pltpu.VMEM((J_ROWS, LANES), F32), pltpu.VMEM((J_ROWS, LANES), F32),
                        pltpu.VMEM((J_ROWS, J_LANES), F32)],
    )
    return pl.pallas_call(
        functools.partial(_dsa_sample_kernel, pps=pps, page=page, n_pages=n_pages),
        out_shape=jax.ShapeDtypeStruct((nseq * TOK_PAD, N_HEADS * HEAD_DIM), F32),
        grid_spec=grid_spec,
        compiler_params=_params(("parallel", "arbitrary")),
        name="dsa_sample",
    )(page_table, ys, ys, ys, bias, *([cache] * pps))


def _layer_norm(x, g, b):
    mu = jnp.mean(x, axis=-1, keepdims=True)
    xc = x - mu
    var = jnp.mean(xc * xc, axis=-1, keepdims=True)
    return xc * lax.rsqrt(var + LN_EPS) * g + b


def _merge_kernel(oa_ref, ob_ref, wa_ref, wb_ref, ga_ref, gb_ref, wo_ref, x_ref, g_ref, b_ref,
                  h_ref, hb_ref, acc_sc, *, alpha):
    kt = pl.program_id(1)

    @pl.when(kt == 0)
    def _():
        acc_sc[...] = jnp.zeros(acc_sc.shape, F32)

    oa = oa_ref[...].astype(BF16)
    ob = ob_ref[...].astype(BF16)
    mix = ga_ref[...] * _dot(oa, wa_ref[...]) + gb_ref[...] * _dot(ob, wb_ref[...])
    acc_sc[...] += _dot(mix.astype(BF16), wo_ref[...])

    @pl.when(kt == pl.num_programs(1) - 1)
    def _():
        h = _layer_norm(alpha * x_ref[...] + acc_sc[...], g_ref[...], b_ref[...])
        h_ref[...] = h
        hb_ref[...] = h.astype(BF16)


def _merge(oa, ob, y, x, wa, wb, wo, g, b, layer, alpha):
    rows = x.shape[0]
    tr = min(rows, 512)
    tk = 512
    kw = N_HEADS * HEAD_DIM
    return pl.pallas_call(
        functools.partial(_merge_kernel, alpha=alpha),
        out_shape=(jax.ShapeDtypeStruct((rows, D_MODEL), F32), jax.ShapeDtypeStruct((rows, D_MODEL), BF16)),
        grid=(rows // tr, D_MODEL // tk),
        in_specs=[pl.BlockSpec((tr, kw), lambda r, k: (r, 0)),
                  pl.BlockSpec((tr, kw), lambda r, k: (r, 0)),
                  pl.BlockSpec((None, kw, tk), lambda r, k: (layer, 0, k)),
                  pl.BlockSpec((None, kw, tk), lambda r, k: (layer, 0, k)),
                  pl.BlockSpec((tr, tk), lambda r, k: (r, COL_GA // tk + k)),
                  pl.BlockSpec((tr, tk), lambda r, k: (r, COL_GB // tk + k)),
                  pl.BlockSpec((None, tk, D_MODEL), lambda r, k: (layer, k, 0)),
                  pl.BlockSpec((tr, D_MODEL), lambda r, k: (r, 0)),
                  pl.BlockSpec((None, 1, D_MODEL), lambda r, k: (layer, 0, 0)),
                  pl.BlockSpec((None, 1, D_MODEL), lambda r, k: (layer, 0, 0))],
        out_specs=(pl.BlockSpec((tr, D_MODEL), lambda r, k: (r, 0)),
                   pl.BlockSpec((tr, D_MODEL), lambda r, k: (r, 0))),
        scratch_shapes=[pltpu.VMEM((tr, D_MODEL), F32)],
        compiler_params=_params(("parallel", "arbitrary")),
        name="merge_ln",
    )(oa, ob, wa, wb, y, y, wo, x, g, b)


def _gelu_tanh(x):
    return 0.5 * x * (1.0 + jnp.tanh(np.sqrt(2.0 / np.pi) * (x + 0.044715 * (x * x * x))))


HALO = 16
FFN_ROWS = 1024
FFN_VMEM_LIMIT = 61 * 1024 * 1024


def _ffn_kernel(*refs, alpha, tr, seq_len, blocks_per_seq, prompt_mode):
    if prompt_mode:
        (hb_ref, halo_ref, w1a_ref, w1u_ref, cw_ref, cb_ref, w2_ref, h_ref, g_ref, b_ref,
         h2_ref, h2b_ref, a_ref) = refs
    else:
        (hb_ref, s1_ref, s2_ref, w1a_ref, w1u_ref, cw_ref, cb_ref, w2_ref, h_ref, g_ref, b_ref,
         h2_ref, h2b_ref, a_ref) = refs
    r = pl.program_id(0)
    ft = pl.program_id(1)
    acc_sc = h2_ref

    @pl.when(ft == 0)
    def _():
        acc_sc[...] = jnp.zeros(acc_sc.shape, F32)

    hb = hb_ref[...]
    a = _dot(hb, w1a_ref[...])
    up = _dot(hb, w1u_ref[...])
    row = lax.broadcasted_iota(I32, a.shape, 0)
    p1 = pltpu.roll(a, 1, 0)
    p2 = pltpu.roll(a, 2, 0)
    if prompt_mode:
        a_halo = _dot(halo_ref[...], w1a_ref[...])
        keep = jnp.where(r % blocks_per_seq == 0, 0.0, 1.0)
        h6 = a_halo[HALO - 2:HALO - 1, :] * keep
        h7 = a_halo[HALO - 1:HALO, :] * keep
        p1 = jnp.where(row == 0, h7, p1)
        p2 = jnp.where(row == 0, h6, jnp.where(row == 1, h7, p2))
        a_ref[...] = a[tr - SUBLANES:, :]
    else:
        t = row & (seq_len - 1)
        p1 = jnp.where(t == 0, s1_ref[...], p1)
        p2 = jnp.where(t < 2, s2_ref[...], p2)
        a_ref[...] = a
    c = cb_ref[...] + cw_ref[0:1, :] * p2 + cw_ref[1:2, :] * p1 + cw_ref[2:3, :] * a
    hmid = (_gelu_tanh(c) * up).astype(BF16)
    acc_sc[...] += _dot(hmid, w2_ref[...])

    @pl.when(ft == pl.num_programs(1) - 1)
    def _():
        h2 = _layer_norm(alpha * h_ref[...] + acc_sc[...], g_ref[...], b_ref[...])
        h2_ref[...] = h2
        h2b_ref[...] = h2.astype(BF16)


def _ffn(h, hb, w1a, w1u, cw, cb, w2, g, b, layer, alpha, seq_len, state=None):
    rows = h.shape[0]
    prompt_mode = state is None
    tr = min(seq_len, FFN_ROWS) if prompt_mode else rows
    n_r = rows // tr
    n_f = D_FF_PAD // TF
    common_w = [pl.BlockSpec((None, D_MODEL, TF), lambda r, f: (layer, 0, f)),
                pl.BlockSpec((None, D_MODEL, TF), lambda r, f: (layer, 0, f)),
                pl.BlockSpec((None, SUBLANES, TF), lambda r, f: (layer, 0, f)),
                pl.BlockSpec((None, 1, TF), lambda r, f: (layer, 0, f)),
                pl.BlockSpec((None, TF, D_MODEL), lambda r, f: (layer, f, 0)),
                pl.BlockSpec((tr, D_MODEL), lambda r, f: (r, 0), pipeline_mode=pl.Buffered(1)),
                pl.BlockSpec((None, 1, D_MODEL), lambda r, f: (layer, 0, 0)),
                pl.BlockSpec((None, 1, D_MODEL), lambda r, f: (layer, 0, 0))]
    if prompt_mode:
        assert seq_len % tr == 0 and tr % HALO == 0
        per = tr // HALO
        extra_specs = [pl.BlockSpec((HALO, D_MODEL), lambda r, f: (jnp.maximum(r * per - 1, 0), 0))]
        extra = [hb]
        a_rows, a_blk = n_r * SUBLANES, SUBLANES
    else:
        assert seq_len & (seq_len - 1) == 0 and seq_len >= CONV_W - 1
        extra_specs = [pl.BlockSpec((tr, TF), lambda r, f: (0, f)),
                       pl.BlockSpec((tr, TF), lambda r, f: (0, f))]
        extra = list(state)
        a_rows, a_blk = rows, tr
    return pl.pallas_call(
        functools.partial(_ffn_kernel, alpha=alpha, tr=tr, seq_len=seq_len,
                          blocks_per_seq=max(seq_len // tr, 1), prompt_mode=prompt_mode),
        out_shape=(jax.ShapeDtypeStruct((rows, D_MODEL), F32), jax.ShapeDtypeStruct((rows, D_MODEL), BF16),
                   jax.ShapeDtypeStruct((a_rows, D_FF_PAD), F32)),
        grid=(n_r, n_f),
        in_specs=[pl.BlockSpec((tr, D_MODEL), lambda r, f: (r, 0), pipeline_mode=pl.Buffered(1))]
                 + extra_specs + common_w,
        out_specs=(pl.BlockSpec((tr, D_MODEL), lambda r, f: (r, 0)),
                   pl.BlockSpec((tr, D_MODEL), lambda r, f: (r, 0), pipeline_mode=pl.Buffered(1)),
                   pl.BlockSpec((a_blk, TF), lambda r, f: (r, f))),
        compiler_params=_params(("parallel", "arbitrary"), FFN_VMEM_LIMIT),
        name="conv_ffn_ln",
    )(hb, *extra, w1a, w1u, cw, cb, w2, h, g, b)


def _ple_kernel(hb_ref, wg_ref, p_ref, wp_ref, h_ref, o_ref, ob_ref):
    gate = jax.nn.sigmoid(_dot(hb_ref[...], wg_ref[...]))
    out = h_ref[...] + gate * _dot(p_ref[...], wp_ref[...])
    o_ref[...] = out
    ob_ref[...] = out.astype(BF16)


def _ple(h2, h2b, pb, wg, wp, layer):
    rows = h2.shape[0]
    tr = min(rows, 1024)
    tn = 512
    return pl.pallas_call(
        _ple_kernel,
        out_shape=(jax.ShapeDtypeStruct((rows, D_MODEL), F32), jax.ShapeDtypeStruct((rows, D_MODEL), BF16)),
        grid=(rows // tr, D_MODEL // tn),
        in_specs=[pl.BlockSpec((tr, D_MODEL), lambda r, n: (r, 0)),
                  pl.BlockSpec((None, D_MODEL, tn), lambda r, n: (layer, 0, n)),
                  pl.BlockSpec((None, tr, PLE_DIM), lambda r, n: (layer, r, 0)),
                  pl.BlockSpec((None, PLE_DIM, tn), lambda r, n: (layer, 0, n)),
                  pl.BlockSpec((tr, tn), lambda r, n: (r, n))],
        out_specs=(pl.BlockSpec((tr, tn), lambda r, n: (r, n)),
                   pl.BlockSpec((tr, tn), lambda r, n: (r, n))),
        compiler_params=_params(("parallel", "arbitrary")),
        name="ple_gate",
    )(h2b, wg, pb, wp, h2)


IN_WIDTH = COL_KIW + IDX_DIM + IDX_HEADS + 2 * D_MODEL
PACK_ROWS = 128


SMALL_COLS = IDX_DIM + IDX_HEADS
KIW_TILE = COL_KIW // TN


def _pack_w_in_kernel(prev_ref, cur_ref, o_ref):
    n = pl.program_id(1)

    @pl.when(n < KIW_TILE)
    def _():
        o_ref[...] = cur_ref[...].T.astype(BF16)

    @pl.when(n == KIW_TILE)
    def _():
        t = jnp.concatenate([cur_ref[:SMALL_COLS, :], jnp.zeros((TN - SMALL_COLS, D_MODEL), F32)], axis=0)
        o_ref[...] = t.T.astype(BF16)

    @pl.when(n > KIW_TILE)
    def _():
        t = jnp.concatenate([prev_ref[SMALL_COLS:, :], cur_ref[:SMALL_COLS, :]], axis=0)
        o_ref[...] = t.T.astype(BF16)


def _pack_w_in(w_in):
    depth, rows, width = w_in.shape
    assert width == IN_WIDTH and rows == D_MODEL and SMALL_COLS % 16 == 0
    w_t = jnp.swapaxes(w_in, 1, 2)
    return pl.pallas_call(
        _pack_w_in_kernel,
        out_shape=jax.ShapeDtypeStruct((depth, rows, IN_COLS), BF16),
        grid=(depth, N_IN_TILES),
        in_specs=[pl.BlockSpec((None, TN, D_MODEL), lambda l, n: (l, jnp.maximum(n - 1, KIW_TILE), 0)),
                  pl.BlockSpec((None, TN, D_MODEL), lambda l, n: (l, n, 0))],
        out_specs=pl.BlockSpec((None, D_MODEL, TN), lambda l, n: (l, 0, n)),
        compiler_params=_params(("parallel", "arbitrary")),
        name="pack_w_in",
    )(w_t, w_t)


def _pack_ffn_in_kernel(w_ref, a_ref, u_ref):
    x = w_ref[...]
    zeros = jnp.zeros((x.shape[0], D_FF_PAD - D_FF), BF16)
    a_ref[:, :D_FF] = x[:, :D_FF].astype(BF16)
    a_ref[:, D_FF:] = zeros
    u_ref[:, :D_FF] = x[:, D_FF:].astype(BF16)
    u_ref[:, D_FF:] = zeros


def _pack_ffn_in(w_ffn_in):
    depth, rows, width = w_ffn_in.shape
    out = jax.ShapeDtypeStruct((depth, rows, D_FF_PAD), BF16)
    spec = pl.BlockSpec((None, PACK_ROWS, D_FF_PAD), lambda l, r: (l, r, 0))
    return pl.pallas_call(
        _pack_ffn_in_kernel,
        out_shape=(out, out),
        grid=(depth, rows // PACK_ROWS),
        in_specs=[pl.BlockSpec((None, PACK_ROWS, width), lambda l, r: (l, r, 0))],
        out_specs=(spec, spec),
        compiler_params=_params(("parallel", "parallel")),
        name="pack_ffn_in",
    )(w_ffn_in)


def _pack_ffn_out_kernel(w_ref, o_ref):
    row = pl.program_id(1) * TF + lax.broadcasted_iota(I32, w_ref.shape, 0)
    o_ref[...] = jnp.where(row < D_FF, w_ref[...], 0.0).astype(BF16)


def _pack_ffn_out(w_ffn_out):
    depth, _, cols = w_ffn_out.shape
    return pl.pallas_call(
        _pack_ffn_out_kernel,
        out_shape=jax.ShapeDtypeStruct((depth, D_FF_PAD, cols), BF16),
        grid=(depth, D_FF_PAD // TF),
        in_specs=[pl.BlockSpec((None, TF, cols), lambda l, r: (l, r, 0))],
        out_specs=pl.BlockSpec((None, TF, cols), lambda l, r: (l, r, 0)),
        compiler_params=_params(("parallel", "parallel")),
        name="pack_ffn_out",
    )(w_ffn_out)


def _rope_tables(pos):
    pos = pos.astype(F32)[:, None]

    def table(head_dim):
        rot = head_dim // 4
        half = rot // 2
        inv = ROPE_THETA ** (-(2.0 * jnp.arange(half, dtype=F32)) / rot)
        ang = pos * inv[None, :]
        cos, sin = jnp.cos(ang), jnp.sin(ang)
        ones = jnp.ones((pos.shape[0], head_dim - rot), F32)
        c = jnp.concatenate([cos, cos, ones], axis=1)
        s = jnp.concatenate([-sin, sin, 0.0 * ones], axis=1)
        reps = LANES // head_dim
        return jnp.tile(c, (1, reps)), jnp.tile(s, (1, reps))

    c128, s128 = table(HEAD_DIM)
    c64, s64 = table(IDX_DIM)
    return c128, s128, c64, s64


def kernel(x_prompt, x_sample, cache_sb_kv, cache_dsa_kv, cache_idx_k, state_ffn_conv, page_table,
           p_prompt, p_sample, w_in, w_branch_sb, w_branch_dsa, w_out, ln1_g, ln1_b, w_ffn_in,
           ffn_conv_w, ffn_conv_b, w_ffn_out, ln2_g, ln2_b, w_ple_gate, w_ple_proj):
    batch, seq = x_prompt.shape[:2]
    nseq, n_tok = x_sample.shape[:2]
    depth = w_in.shape[0]
    n_pool, page = cache_sb_kv.shape[1:3]
    n_pages = page_table.shape[1]
    past_len = n_pages * page
    alpha = (2 * depth) ** 0.25
    kv_w = 2 * N_KV * HEAD_DIM
    top_s = max(1, min(TOPK_MAX, (past_len + n_tok) // 4))
    pps = min(32, n_pages)
    pps_idx = min(64, n_pages)

    w_in_p = _pack_w_in(w_in)
    wa = w_branch_sb.astype(BF16)
    wb = w_branch_dsa.astype(BF16)
    wo = w_out.astype(BF16)
    ff_pad = D_FF_PAD - D_FF
    w1a, w1u = _pack_ffn_in(w_ffn_in)
    w2 = _pack_ffn_out(w_ffn_out)
    cw = jnp.pad(ffn_conv_w, ((0, 0), (0, SUBLANES - CONV_W), (0, ff_pad)))
    cb = jnp.pad(ffn_conv_b, ((0, 0), (0, ff_pad)))[:, None, :]
    wg = w_ple_gate.astype(BF16)
    wp = w_ple_proj.astype(BF16)
    g1, b1 = ln1_g[:, None, :], ln1_b[:, None, :]
    g2, b2 = ln2_g[:, None, :], ln2_b[:, None, :]

    tabs_p = _rope_tables(jnp.tile(jnp.arange(seq, dtype=jnp.int32), batch))
    tabs_s = _rope_tables(jnp.tile(past_len + jnp.arange(TOK_PAD, dtype=jnp.int32), nseq))

    sb_pages = cache_sb_kv.reshape(depth, n_pool, page * KV_SLOTS, HEAD_DIM)
    dsa_pages = cache_dsa_kv.reshape(depth, n_pool, page * KV_SLOTS, HEAD_DIM)
    idx_pages_t = jnp.swapaxes(cache_idx_k, 2, 3)

    assert n_tok <= TOK_PAD
    tok_pad = ((0, 0), (0, TOK_PAD - n_tok), (0, 0))
    xp = x_prompt.reshape(batch * seq, D_MODEL)
    xs = jnp.pad(x_sample, tok_pad).reshape(nseq * TOK_PAD, D_MODEL)
    xpb, xsb = xp.astype(BF16), xs.astype(BF16)
    ppb = p_prompt.reshape(depth, batch * seq, PLE_DIM).astype(BF16)
    psb = jnp.pad(p_sample, ((0, 0),) + tok_pad).reshape(depth, nseq * TOK_PAD, PLE_DIM).astype(BF16)
    st = jnp.pad(state_ffn_conv, ((0, 0), (0, 0), (0, 0), (0, ff_pad)))
    conv_s1 = jnp.repeat(st[:, :, 1], TOK_PAD, axis=1)
    conv_s2 = jnp.pad(st, ((0, 0), (0, 0), (0, TOK_PAD - (CONV_W - 1)), (0, 0))).reshape(
        depth, nseq * TOK_PAD, D_FF_PAD)

    def new_row_buffers(rows):
        kv_shape = (depth, rows * KV_SLOTS, HEAD_DIM)
        return jnp.zeros(kv_shape, F32), jnp.zeros(kv_shape, F32), jnp.zeros((depth, rows, LANES), F32)

    new_p = new_row_buffers(batch * seq)
    new_s = new_row_buffers(nseq * TOK_PAD)
    outs = {k: [] for k in ("conv_p", "conv_s")}
    for l in range(depth):
        y, *new_p = _project(xpb, w_in_p, l, tabs_p, new_p)
        oa = _sb_prompt(y, batch, seq)
        ob = _dsa_prompt(y, batch, seq)
        h, hb = _merge(oa, ob, y, xp, wa, wb, wo, g1, b1, l, alpha)
        h2, h2b, a_tail = _ffn(h, hb, w1a, w1u, cw, cb, w2, g2, b2, l, alpha, seq)
        xp, xpb = _ple(h2, h2b, ppb, wg, wp, l)
        tails = a_tail.reshape(batch, -1, SUBLANES, D_FF_PAD)[:, -1, SUBLANES - (CONV_W - 1):, :D_FF]
        outs["conv_p"].append(tails)

        ys, *new_s = _project(xsb, w_in_p, l, tabs_s, new_s)
        oa_s = _sb_sample(page_table, ys, sb_pages, l, pps)
        bias = _idx_sample(page_table, ys, idx_pages_t, l, pps_idx, top_s, n_tok)
        ob_s = _dsa_sample(page_table, ys, bias, dsa_pages, l, pps)
        hs, hsb = _merge(oa_s, ob_s, ys, xs, wa, wb, wo, g1, b1, l, alpha)
        h2s, h2sb, a_s = _ffn(hs, hsb, w1a, w1u, cw, cb, w2, g2, b2, l, alpha,
                              TOK_PAD, state=(conv_s1[l], conv_s2[l]))
        xs, xsb = _ple(h2s, h2sb, psb, wg, wp, l)
        outs["conv_s"].append(a_s.reshape(nseq, TOK_PAD, D_FF_PAD)[:, n_tok - (CONV_W - 1):n_tok, :D_FF])

    sb_p, dsa_p, kiw_p = new_p
    sb_s, dsa_s, kiw_s = [b.reshape(depth, nseq, TOK_PAD, -1)[:, :, :n_tok] for b in new_s]
    kv_shape = (2, N_KV, HEAD_DIM)
    return (xp.reshape(batch, seq, D_MODEL), xs.reshape(nseq, TOK_PAD, D_MODEL)[:, :n_tok],
            sb_p.reshape(depth, batch, seq, *kv_shape), dsa_p.reshape(depth, batch, seq, *kv_shape),
            kiw_p[..., :IDX_DIM].reshape(depth, batch, seq, IDX_DIM), jnp.stack(outs["conv_p"]),
            sb_s.reshape(depth, nseq, n_tok, *kv_shape), dsa_s.reshape(depth, nseq, n_tok, *kv_shape),
            kiw_s[..., :IDX_DIM], jnp.stack(outs["conv_s"]))
```

```python
import functools

import jax
import jax.numpy as jnp
import numpy as np
from jax import lax
from jax.experimental import pallas as pl
from jax.experimental.pallas import tpu as pltpu

F32 = jnp.float32
BF16 = jnp.bfloat16
I32 = jnp.int32

D_MODEL = 2048
HEAD_DIM = 128
N_HEADS = 8
N_KV = 2
GROUP = N_HEADS // N_KV
IDX_HEADS = 16
IDX_DIM = 64
TOPK_MAX = 256
ROPE_THETA = 500000.0
D_FF = 5504
CONV_W = 3
PLE_DIM = 256
LN_EPS = 1e-5

LANES = 128
SUBLANES = 8
VMEM_LIMIT = 56 * 1024 * 1024

TN = 512
COL_QA = 0
COL_KVA = 1024
COL_QB = 1536
COL_KVB = 2560
COL_QI = 3072
COL_KIW = 4096
COL_GA = 4608
COL_GB = 6656
IN_COLS = 8704
N_IN_TILES = IN_COLS // TN
D_FF_PAD = 5632
TF = 512

SB_Q_SCALE = (HEAD_DIM ** -0.5) * float(np.log2(np.e))
SB_STACK_ROWS = 64
NEG_BIG = -1e30
KEY_NEG_INF = np.int32(np.array(0xFF800000, dtype=np.uint32).view(np.int32) ^ 0x7FFFFFFF)
INT_MIN = np.int32(-2 ** 31)


def _params(sem, vmem_limit=VMEM_LIMIT):
    return pltpu.CompilerParams(dimension_semantics=sem, vmem_limit_bytes=vmem_limit)


def _dot_t(a, b):
    return lax.dot_general(a, b, (((1,), (1,)), ((), ())), preferred_element_type=F32)


def _dot(a, b):
    return jnp.dot(a, b, preferred_element_type=F32)


def _rope(y, c, s, half):
    w = y.shape[1]
    reps = w // LANES
    if reps > 1:
        c = jnp.concatenate([c] * reps, axis=1)
        s = jnp.concatenate([s] * reps, axis=1)
    lane = lax.broadcasted_iota(I32, y.shape, 1)
    first = (lane & (2 * half - 1)) < half
    partner = jnp.where(first, pltpu.roll(y, w - half, 1), pltpu.roll(y, half, 1))
    return y * c + partner * s


def _proj_kernel(x_ref, w_ref, c128_ref, s128_ref, c64_ref, s64_ref, sb_in, dsa_in, kiw_in,
                 y_ref, sb_ref, dsa_ref, kiw_ref):
    del sb_in, dsa_in, kiw_in
    n = pl.program_id(1)

    def product():
        return _dot(x_ref[...], w_ref[...])

    @pl.when(jnp.logical_and(n < COL_QB // TN, n != COL_KVA // TN))
    def _():
        y_ref[...] = product()

    def store_kv_rows(ref, tile):
        for slot in range(TN // HEAD_DIM):
            ref[pl.ds(slot, tile.shape[0], stride=TN // HEAD_DIM), :] = tile[:, slot * HEAD_DIM:(slot + 1) * HEAD_DIM]

    @pl.when(n == COL_KVA // TN)
    def _():
        y = product()
        y_ref[...] = y
        store_kv_rows(sb_ref, y)

    @pl.when(jnp.logical_and(n >= COL_QB // TN, n < COL_KVB // TN))
    def _():
        y_ref[...] = _rope(product(), c128_ref[...], s128_ref[...], 16)

    @pl.when(n == COL_KVB // TN)
    def _():
        y = product()
        k_rot = _rope(y[:, :256], c128_ref[...], s128_ref[...], 16)
        y_ref[:, :256] = k_rot
        y_ref[:, 256:] = y[:, 256:]
        store_kv_rows(dsa_ref, jnp.concatenate([k_rot, y[:, 256:]], axis=1))

    @pl.when(jnp.logical_and(n >= COL_QI // TN, n < COL_KIW // TN))
    def _():
        y_ref[...] = _rope(product(), c64_ref[...], s64_ref[...], 8)

    @pl.when(n == COL_KIW // TN)
    def _():
        y = product()
        lane = lax.broadcasted_iota(I32, c64_ref.shape, 1)
        c = jnp.where(lane < IDX_DIM, c64_ref[...], 1.0)
        s = jnp.where(lane < IDX_DIM, s64_ref[...], 0.0)
        kiw = _rope(y[:, :LANES], c, s, 8)
        y_ref[:, :LANES] = kiw
        y_ref[:, LANES:] = y[:, LANES:]
        kiw_ref[...] = kiw

    @pl.when(n >= COL_GA // TN)
    def _():
        y_ref[...] = jax.nn.sigmoid(product())


def _project(xb, w, layer, tabs, new_rows):
    rows = xb.shape[0]
    tr = min(rows, 1024)
    tab_spec = pl.BlockSpec((tr, LANES), lambda r, n: (r, 0))
    any_spec = pl.BlockSpec(memory_space=pl.ANY)
    kv_spec = pl.BlockSpec((None, tr * (TN // HEAD_DIM), HEAD_DIM), lambda r, n: (layer, r, 0))
    return pl.pallas_call(
        _proj_kernel,
        out_shape=(jax.ShapeDtypeStruct((rows, IN_COLS), F32),)
                  + tuple(jax.ShapeDtypeStruct(b.shape, b.dtype) for b in new_rows),
        grid=(rows // tr, N_IN_TILES),
        in_specs=[pl.BlockSpec((tr, D_MODEL), lambda r, n: (r, 0)),
                  pl.BlockSpec((None, D_MODEL, TN), lambda r, n: (layer, 0, n)),
                  tab_spec, tab_spec, tab_spec, tab_spec, any_spec, any_spec, any_spec],
        out_specs=(pl.BlockSpec((tr, TN), lambda r, n: (r, n)), kv_spec, kv_spec,
                   pl.BlockSpec((None, tr, LANES), lambda r, n: (layer, r, 0))),
        input_output_aliases={6: 1, 7: 2, 8: 3},
        compiler_params=_params(("parallel", "arbitrary")),
        name="in_proj",
    )(xb, w, *tabs, *new_rows)


def _suffix_matrix():
    j = lax.broadcasted_iota(I32, (2 * LANES, 2 * LANES), 0) & (LANES - 1)
    s = lax.broadcasted_iota(I32, (2 * LANES, 2 * LANES), 1)
    return jnp.where(jnp.logical_or(j > s, s >= LANES), -1.0, 0.0).astype(BF16)


def _sb_scores(q, k, mask):
    n_sub = k.shape[0] // LANES
    z = _dot_t(q, k)
    neg_abs = pltpu.bitcast(pltpu.bitcast(z, I32) | INT_MIN, F32)
    sp = jnp.maximum(z, 0.0) + jnp.log2(1.0 + jnp.exp2(neg_abs))
    spm = sp if mask is None else jnp.where(mask, sp, 0.0)
    hi = spm.astype(BF16)
    lo = (spm - hi.astype(F32)).astype(BF16)
    subs = [jnp.concatenate([hi[:, i * LANES:(i + 1) * LANES], lo[:, i * LANES:(i + 1) * LANES]], axis=1)
            for i in range(n_sub)]
    return z - sp, subs


def _sb_weights(d, subs, v, u, carry, mask):
    n_sub = len(subs)
    m_rows = d.shape[0]
    if m_rows <= SB_STACK_ROWS:
        r_all = _dot(subs[0] if n_sub == 1 else jnp.concatenate(subs, axis=0), u)
        rs = [r_all[i * m_rows:(i + 1) * m_rows] for i in range(n_sub)]
    else:
        rs = [_dot(s, u) for s in subs]
    afters = [None] * n_sub
    for i in reversed(range(n_sub)):
        afters[i] = rs[i][:, :LANES] + carry
        carry = carry + rs[i][:, LANES:]
    after = afters[0] if n_sub == 1 else jnp.concatenate(afters, axis=1)
    w = jnp.exp2(d + after)
    if mask is not None:
        w = jnp.where(mask, w, 0.0)
    return _dot(w.astype(BF16), v), carry


def _sb_block(q, k, v, u, carry, mask):
    d, subs = _sb_scores(q, k, mask)
    return _sb_weights(d, subs, v, u, carry, mask)


def _softmax_block(q, k, v, bias, m_old, l_old, acc_old):
    logit = _dot_t(q, k) + bias
    m_new = jnp.maximum(m_old, jnp.max(logit, axis=1, keepdims=True))
    p = jnp.exp(logit - m_new[:, 0:1])
    alpha = jnp.exp(m_old - m_new)
    l_new = alpha * l_old + jnp.sum(p, axis=1, keepdims=True)
    acc_new = alpha[:, 0:1] * acc_old + _dot(p.astype(BF16), v)
    return m_new, l_new, acc_new


def _sortable(x):
    b = pltpu.bitcast(x, I32)
    return jnp.where(b < 0, b ^ jnp.int32(0x7FFFFFFF), b)


def _kth_largest(count_ge, shape, k, n_total):
    bits_per_check = 4

    def cond(state):
        i, _, cnt = state
        return jnp.logical_and(i < 32, jnp.max(cnt) > k)

    def body(state):
        i, t, cnt = state
        for _ in range(bits_per_check):
            cand = t + jnp.left_shift(jnp.int32(1), jnp.int32(31) - i)
            c = count_ge(cand)
            ok = c >= k
            i, t, cnt = i + 1, jnp.where(ok, cand, t), jnp.where(ok, c, cnt)
        return i, t, cnt

    state = (jnp.int32(0), jnp.full(shape, INT_MIN, I32), jnp.full(shape, n_total, I32))
    _, t, cnt = lax.while_loop(cond, body, state)
    return t, cnt


def _tie_cutoff(count_tie_below, shape, need, n_bits):
    def body(i, j):
        cand = j + jnp.left_shift(jnp.int32(1), jnp.int32(n_bits - 1) - i)
        return jnp.where(count_tie_below(cand) < need, cand, j)

    return lax.fori_loop(0, n_bits, body, jnp.zeros(shape, I32))


def _sb_prompt_kernel(q_ref, k_ref, v_ref, o_ref, kb_sc, vb_sc, carry_sc, acc_sc, *, tq):
    iq = pl.program_id(1)

    @pl.when(iq == 0)
    def _():
        kb_sc[...] = k_ref[...].astype(BF16)
        vb_sc[...] = v_ref[...].astype(BF16)

    scale = SB_Q_SCALE
    u = _suffix_matrix()
    n_chain = N_HEADS // 4
    c_rows = 4 * tq
    qs = []
    for c in range(n_chain):
        q = jnp.concatenate([q_ref[:, h * HEAD_DIM:(h + 1) * HEAD_DIM] for h in range(4 * c, 4 * c + 4)], axis=0)
        qs.append((q * scale).astype(BF16))

    def kv_block(o2, c):
        g = (4 * c) // GROUP
        return (kb_sc[pl.ds(o2, tq), g * HEAD_DIM:(g + 1) * HEAD_DIM],
                vb_sc[pl.ds(o2, tq), g * HEAD_DIM:(g + 1) * HEAD_DIM])

    row = lax.broadcasted_iota(I32, (c_rows, tq), 0) & (tq - 1)
    col = lax.broadcasted_iota(I32, (c_rows, tq), 1)
    off = pl.multiple_of(iq * tq, tq)
    for c in range(n_chain):
        k, v = kv_block(off, c)
        contrib, carry = _sb_block(qs[c], k, v, u, jnp.zeros((c_rows, LANES), F32), col < row)
        acc_sc[c] = contrib
        carry_sc[c] = carry

    def body(i, _):
        o2 = pl.multiple_of((iq - 1 - i) * tq, tq)
        for c in range(n_chain):
            k, v = kv_block(o2, c)
            contrib, carry = _sb_block(qs[c], k, v, u, carry_sc[c], None)
            acc_sc[c] += contrib
            carry_sc[c] = carry
        return 0

    lax.fori_loop(0, iq, body, 0)
    for h in range(N_HEADS):
        r0 = (h % 4) * tq
        o_ref[:, h * HEAD_DIM:(h + 1) * HEAD_DIM] = acc_sc[h // 4, r0:r0 + tq, :].astype(o_ref.dtype)


def _sb_prompt(y, batch, seq):
    tq = min(2 * LANES, seq)
    nq = seq // tq
    qw = N_HEADS * HEAD_DIM
    kvw = N_KV * HEAD_DIM
    return pl.pallas_call(
        functools.partial(_sb_prompt_kernel, tq=tq),
        out_shape=jax.ShapeDtypeStruct((batch * seq, qw), BF16),
        grid=(batch, nq),
        in_specs=[pl.BlockSpec((tq, qw), lambda b, i: (b * nq + i, COL_QA // qw)),
                  pl.BlockSpec((seq, kvw), lambda b, i: (b, COL_KVA // kvw)),
                  pl.BlockSpec((seq, kvw), lambda b, i: (b, COL_KVA // kvw + 1))],
        out_specs=pl.BlockSpec((tq, qw), lambda b, i: (b * nq + i, 0)),
        scratch_shapes=[pltpu.VMEM((seq, kvw), BF16), pltpu.VMEM((seq, kvw), BF16),
                        pltpu.VMEM((N_HEADS // 4, 4 * tq, LANES), F32),
                        pltpu.VMEM((N_HEADS // 4, 4 * tq, LANES), F32)],
        compiler_params=_params(("parallel", "arbitrary")),
        name="sb_prompt",
    )(y, y, y)


TKI = 256
LOGIT_SAFE = 60.0
NORM_SLACK = 1.02


def _fold_rows(x, op):
    return op(x.reshape(x.shape[0] // SUBLANES, SUBLANES, x.shape[1]), axis=0)


def _dsa_prompt_kernel(qb0_ref, qb1_ref, k_ref, v_ref, qi_ref, kiw_all_ref, kiw_q_ref, o_ref,
                       kb_sc, vb_sc, ki2_sc, key_sc, bias_sc, cut_sc, kn_sc, m_sc, l_sc, acc_sc,
                       *, tq, n_top, idx_bits):
    iq = pl.program_id(1)
    seq = k_ref.shape[0]

    @pl.when(iq == 0)
    def _():
        kb_sc[...] = k_ref[...].astype(BF16)
        vb_sc[...] = v_ref[...].astype(BF16)
        k_sq = k_ref[...] * k_ref[...]
        k_sq_max = jnp.float32(0.0)
        for g in range(N_KV):
            norms = jnp.sum(k_sq[:, g * HEAD_DIM:(g + 1) * HEAD_DIM], axis=1, keepdims=True)
            k_sq_max = jnp.maximum(k_sq_max, jnp.max(norms))
        kn_sc[...] = jnp.zeros(kn_sc.shape, F32) + k_sq_max
        lane = lax.broadcasted_iota(I32, (seq, LANES), 1)
        kia = jnp.where(lane < IDX_DIM, kiw_all_ref[...], 0.0)
        ki2_sc[0] = kia.astype(BF16)
        ki2_sc[1] = pltpu.roll(kia, IDX_DIM, 1).astype(BF16)

    w_t = (kiw_q_ref[...] * ((IDX_DIM ** -0.5) * (IDX_HEADS ** -0.5))).T
    w_rows = [w_t[IDX_DIM + h:IDX_DIM + h + 1, :] for h in range(IDX_HEADS)]

    q_pairs = [qi_ref[:, p * LANES:(p + 1) * LANES].astype(BF16) for p in range(IDX_HEADS // 2)]
    q_pos = iq * tq + lax.broadcasted_iota(I32, (1, tq), 1)
    n_blk = (iq * tq + tq + TKI - 1) // TKI

    def idx_body(j, _):
        off = pl.multiple_of(j * TKI, TKI)
        ki_e = ki2_sc[0, pl.ds(off, TKI), :]
        ki_o = ki2_sc[1, pl.ds(off, TKI), :]
        acc = jnp.zeros((TKI, tq), F32)
        for p in range(IDX_HEADS // 2):
            acc = acc + jnp.maximum(_dot_t(ki_e, q_pairs[p]), 0.0) * w_rows[2 * p]
            acc = acc + jnp.maximum(_dot_t(ki_o, q_pairs[p]), 0.0) * w_rows[2 * p + 1]
        k_pos = off + lax.broadcasted_iota(I32, (TKI, 1), 0)
        key_sc[pl.ds(off, TKI), :] = jnp.where(k_pos <= q_pos, _sortable(acc), KEY_NEG_INF)
        return 0

    lax.fori_loop(0, n_blk, idx_body, 0)

    def query_counts(hit_fn):
        def body(j, c):
            off = pl.multiple_of(j * TKI, TKI)
            return c + _fold_rows(hit_fn(key_sc[pl.ds(off, TKI), :], off), jnp.sum)

        c = lax.fori_loop(0, n_blk, body, jnp.zeros((SUBLANES, tq), I32))
        return jnp.sum(c, axis=0, keepdims=True)

    def count_ge(t):
        return query_counts(lambda kb, off: jnp.where(kb >= t, 1, 0))

    thr, n_ge = _kth_largest(count_ge, (1, tq), n_top, n_blk * TKI)

    cut_sc[...] = jnp.full(cut_sc.shape, seq, I32)

    @pl.when(jnp.max(n_ge) > n_top)
    def _():
        need = n_top - count_ge(thr + 1)

        def count_tie_below(jc):
            def hit(kb, off):
                idx = off + lax.broadcasted_iota(I32, kb.shape, 0)
                return jnp.where(kb == thr, jnp.where(idx < jc, 1, 0), 0)
            return query_counts(hit)

        cut = _tie_cutoff(count_tie_below, (1, tq), need, idx_bits)
        cut_sc[...] = jnp.broadcast_to(cut, cut_sc.shape)

    cut = cut_sc[0:1, :]

    def bias_body(j, _):
        off = pl.multiple_of(j * TKI, TKI)
        kb = key_sc[pl.ds(off, TKI), :]
        idx = off + lax.broadcasted_iota(I32, (TKI, tq), 0)
        tie = jnp.where(kb == thr, jnp.where(idx <= cut, 0.0, -jnp.inf), -jnp.inf)
        sel = jnp.where(kb > thr, 0.0, tie)
        sel = jnp.where(kb > KEY_NEG_INF, sel, -jnp.inf)
        bias_sc[:, pl.ds(off, TKI)] = sel.T
        return 0

    lax.fori_loop(0, n_blk, bias_body, 0)

    scale = (HEAD_DIM ** -0.5) * np.log2(np.e)
    n_chain = N_HEADS // 2
    n_sub = TKI // LANES
    qs = []
    ones = jnp.ones((HEAD_DIM, LANES), BF16)
    q_sq = jnp.zeros((SUBLANES, LANES), F32)
    for c in range(n_chain):
        q_ref = (qb0_ref, qb1_ref)[c // 2]
        hs = (2 * (c % 2), 2 * (c % 2) + 1)
        q = jnp.concatenate([q_ref[:, h * HEAD_DIM:(h + 1) * HEAD_DIM] for h in hs], axis=0) * scale
        q_sq = jnp.maximum(q_sq, _fold_rows(_dot((q * q).astype(BF16), ones), jnp.max))
        qs.append(q.astype(BF16))
    q_sq_max = jnp.max(q_sq) * NORM_SLACK

    def bias_rows(off):
        b = bias_sc[:, pl.ds(off, TKI)]
        return jnp.concatenate([b, b], axis=0)

    small_logits = q_sq_max * jnp.max(kn_sc[...]) <= LOGIT_SAFE * LOGIT_SAFE

    @pl.when(small_logits)
    def _():
        m_sc[...] = jnp.zeros(m_sc.shape, F32)

    @pl.when(jnp.logical_not(small_logits))
    def _():
        m_sc[...] = jnp.full(m_sc.shape, -jnp.inf, F32)

        def max_body(j, _):
            off = pl.multiple_of(j * TKI, TKI)
            bias = bias_rows(off)
            for c in range(n_chain):
                g = c // 2
                lg = _dot_t(qs[c], kb_sc[pl.ds(off, TKI), g * HEAD_DIM:(g + 1) * HEAD_DIM]) + bias
                m = m_sc[c]
                for i in range(n_sub):
                    m = jnp.maximum(m, lg[:, i * LANES:(i + 1) * LANES])
                m_sc[c] = m
            return 0

        lax.fori_loop(0, n_blk, max_body, 0)
        for c in range(n_chain):
            m_sc[c] = jnp.broadcast_to(jnp.max(m_sc[c], axis=1, keepdims=True), (2 * tq, LANES))

    l_sc[...] = jnp.zeros(l_sc.shape, F32)
    acc_sc[...] = jnp.zeros(acc_sc.shape, F32)

    def sum_body(j, _):
        off = pl.multiple_of(j * TKI, TKI)
        bias = bias_rows(off)
        for c in range(n_chain):
            g = c // 2
            lg = _dot_t(qs[c], kb_sc[pl.ds(off, TKI), g * HEAD_DIM:(g + 1) * HEAD_DIM]) + bias
            p = jnp.exp2(lg - jnp.concatenate([m_sc[c]] * n_sub, axis=1))
            l = l_sc[c]
            for i in range(n_sub):
                l = l + p[:, i * LANES:(i + 1) * LANES]
            l_sc[c] = l
            acc_sc[c] += _dot(p.astype(BF16), vb_sc[pl.ds(off, TKI), g * HEAD_DIM:(g + 1) * HEAD_DIM])
        return 0

    lax.fori_loop(0, n_blk, sum_body, 0)
    for c in range(n_chain):
        out = acc_sc[c] / jnp.sum(l_sc[c], axis=1, keepdims=True)
        for hh in range(2):
            c0 = (2 * c + hh) * HEAD_DIM
            o_ref[:, c0:c0 + HEAD_DIM] = out[hh * tq:(hh + 1) * tq, :].astype(o_ref.dtype)


def _dsa_prompt(y, batch, seq):
    tq = min(2 * LANES, seq)
    assert tq == TKI or seq == tq
    nq = seq // tq
    gw = GROUP * HEAD_DIM
    kvw = N_KV * HEAD_DIM
    n_top = max(1, min(TOPK_MAX, seq // 4))
    idx_bits = int(seq).bit_length()
    q_rows = 2 * tq
    return pl.pallas_call(
        functools.partial(_dsa_prompt_kernel, tq=tq, n_top=n_top, idx_bits=idx_bits),
        out_shape=jax.ShapeDtypeStruct((batch * seq, N_HEADS * HEAD_DIM), BF16),
        grid=(batch, nq),
        in_specs=[pl.BlockSpec((tq, gw), lambda b, i: (b * nq + i, COL_QB // gw)),
                  pl.BlockSpec((tq, gw), lambda b, i: (b * nq + i, COL_QB // gw + 1)),
                  pl.BlockSpec((seq, kvw), lambda b, i: (b, COL_KVB // kvw)),
                  pl.BlockSpec((seq, kvw), lambda b, i: (b, COL_KVB // kvw + 1)),
                  pl.BlockSpec((tq, IDX_HEADS * IDX_DIM), lambda b, i: (b * nq + i, COL_QI // (IDX_HEADS * IDX_DIM))),
                  pl.BlockSpec((seq, LANES), lambda b, i: (b, COL_KIW // LANES)),
                  pl.BlockSpec((tq, LANES), lambda b, i: (b * nq + i, COL_KIW // LANES))],
        out_specs=pl.BlockSpec((tq, N_HEADS * HEAD_DIM), lambda b, i: (b * nq + i, 0)),
        scratch_shapes=[pltpu.VMEM((seq, kvw), BF16),
                        pltpu.VMEM((seq, kvw), BF16),
                        pltpu.VMEM((2, seq, LANES), BF16),
                        pltpu.VMEM((seq, tq), I32),
                        pltpu.VMEM((tq, seq), F32),
                        pltpu.VMEM((SUBLANES, tq), I32),
                        pltpu.VMEM((SUBLANES, LANES), F32),
                        pltpu.VMEM((N_HEADS // 2, q_rows, LANES), F32),
                        pltpu.VMEM((N_HEADS // 2, q_rows, LANES), F32),
                        pltpu.VMEM((N_HEADS // 2, q_rows, HEAD_DIM), F32)],
        compiler_params=_params(("parallel", "arbitrary")),
        name="dsa_prompt",
    )(y, y, y, y, y, y, y)


TOK_PAD = SUBLANES
S_ROWS = GROUP * TOK_PAD


KV_SLOTS = 2 * N_KV


def _page_rows(ref, slot, page):
    return ref[pl.ds(slot, page, stride=KV_SLOTS), :].astype(BF16)


J_ROWS = N_KV * S_ROWS
J_LANES = N_KV * HEAD_DIM


def _gather_kv(page_refs, page):
    k = [jnp.concatenate([_page_rows(r, g, page) for g in range(N_KV)], axis=1) for r in page_refs]
    v = [jnp.concatenate([_page_rows(r, N_KV + g, page) for g in range(N_KV)], axis=1) for r in page_refs]
    if len(page_refs) == 1:
        return k[0], v[0]
    return jnp.concatenate(k, axis=0), jnp.concatenate(v, axis=0)


def _joint_queries(q_blks, scale):
    rows = []
    for g, q_blk in enumerate(q_blks):
        q = jnp.concatenate([q_blk[:, h * HEAD_DIM:(h + 1) * HEAD_DIM] for h in range(GROUP)], axis=0) * scale
        zero = jnp.zeros_like(q)
        rows.append(jnp.concatenate([q if gg == g else zero for gg in range(N_KV)], axis=1))
    return jnp.concatenate(rows, axis=0).astype(BF16)


def _new_token_kv(new_ref, page):
    pad = jnp.zeros((page - TOK_PAD, J_LANES), F32)
    k = jnp.concatenate([new_ref[:, :J_LANES], pad], axis=0)
    v = jnp.concatenate([new_ref[:, J_LANES:], pad], axis=0)
    return k.astype(BF16), v.astype(BF16)


def _unstack_heads(o_ref, acc):
    for g in range(N_KV):
        for h in range(GROUP):
            r0 = g * S_ROWS + h * TOK_PAD
            c0 = (g * GROUP + h) * HEAD_DIM
            o_ref[:, c0:c0 + HEAD_DIM] = acc[r0:r0 + TOK_PAD, g * HEAD_DIM:(g + 1) * HEAD_DIM]


def _sb_sample_kernel(pt_ref, q_ref, new_ref, *rest, pps, page):
    page_refs = rest[:pps]
    o_ref = rest[pps]
    carry_sc, acc_sc = rest[pps + 1:]
    j = pl.program_id(1)
    u = _suffix_matrix()
    gw = GROUP * HEAD_DIM
    q = _joint_queries([q_ref[:, g * gw:(g + 1) * gw] for g in range(N_KV)], SB_Q_SCALE)

    def visit(k, v, mask):
        c, cr = _sb_block(q, k, v, u, carry_sc[...], mask)
        acc_sc[...] += c
        carry_sc[...] = cr

    @pl.when(j == 0)
    def _():
        carry_sc[...] = jnp.zeros(carry_sc.shape, F32)
        acc_sc[...] = jnp.zeros(acc_sc.shape, F32)
        tok = lax.broadcasted_iota(I32, (J_ROWS, page), 0) & (TOK_PAD - 1)
        col = lax.broadcasted_iota(I32, (J_ROWS, page), 1)
        visit(*_new_token_kv(new_ref, page), col < tok)

    visit(*_gather_kv(page_refs, page), None)

    @pl.when(j == pl.num_programs(1) - 1)
    def _():
        _unstack_heads(o_ref, acc_sc[...])


def _sb_sample(page_table, ys, cache, layer, pps):
    nseq, n_pages = page_table.shape
    rows = cache.shape[2]
    page = rows // KV_SLOTS
    n_steps = n_pages // pps
    qw = N_HEADS * HEAD_DIM
    kvw = KV_SLOTS * HEAD_DIM

    def page_spec(i):
        return pl.BlockSpec((None, None, rows, HEAD_DIM),
                            lambda b, j, pt, i=i: (layer, pt[b, (n_steps - 1 - j) * pps + i], 0, 0))

    grid_spec = pltpu.PrefetchScalarGridSpec(
        num_scalar_prefetch=1,
        grid=(nseq, n_steps),
        in_specs=[pl.BlockSpec((TOK_PAD, qw), lambda b, j, pt: (b, COL_QA // qw)),
                  pl.BlockSpec((TOK_PAD, kvw), lambda b, j, pt: (b, COL_KVA // kvw))]
                 + [page_spec(i) for i in range(pps)],
        out_specs=pl.BlockSpec((TOK_PAD, qw), lambda b, j, pt: (b, 0)),
        scratch_shapes=[pltpu.VMEM((J_ROWS, LANES), F32), pltpu.VMEM((J_ROWS, J_LANES), F32)],
    )
    return pl.pallas_call(
        functools.partial(_sb_sample_kernel, pps=pps, page=page),
        out_shape=jax.ShapeDtypeStruct((nseq * TOK_PAD, qw), F32),
        grid_spec=grid_spec,
        compiler_params=_params(("parallel", "arbitrary")),
        name="sb_sample",
    )(page_table, ys, ys, *([cache] * pps))


def _idx_sample_kernel(pt_ref, qi_ref, kiw_ref, *rest, pps, page, n_pages, n_top, idx_bits, n_tok):
    page_refs = rest[:pps]
    bias_ref = rest[pps]
    score_sc = rest[pps + 1]
    j = pl.program_id(1)

    row = lax.broadcasted_iota(I32, (TOK_PAD, 1), 0)

    def real_rows(x):
        return jnp.where(row < n_tok, x, pltpu.roll(x, n_tok, 0))

    qi_rows = real_rows(qi_ref[...])
    kiw = real_rows(kiw_ref[...])
    qi = jnp.concatenate([qi_rows[:, h * IDX_DIM:(h + 1) * IDX_DIM] for h in range(IDX_HEADS)],
                         axis=0).astype(BF16)
    w_scale = (IDX_DIM ** -0.5) * (IDX_HEADS ** -0.5)
    wm = jnp.concatenate([jnp.broadcast_to(kiw[:, IDX_DIM + h:IDX_DIM + h + 1] * w_scale, (TOK_PAD, LANES))
                          for h in range(IDX_HEADS)], axis=0)

    def head_sum(s):
        n = s.shape[1]
        w = wm if n == LANES else jnp.concatenate([wm] * (n // LANES), axis=1)
        return jnp.sum((jnp.maximum(s, 0.0) * w).reshape(IDX_HEADS, TOK_PAD, n), axis=0)

    @pl.when(j == 0)
    def _():
        ki_new = jnp.concatenate([kiw_ref[:, :IDX_DIM], jnp.zeros((page - TOK_PAD, IDX_DIM), F32)], axis=0)
        tok = lax.broadcasted_iota(I32, (TOK_PAD, page), 0) & (n_tok - 1)
        col = lax.broadcasted_iota(I32, (TOK_PAD, page), 1)
        s_new = head_sum(_dot_t(qi, ki_new.astype(BF16)))
        score_sc[:, n_pages * page:] = jnp.where(col <= tok, s_new, -jnp.inf)

    off = pl.multiple_of(j * (pps * page), pps * page)
    ki_t = jnp.concatenate([r[...] for r in page_refs], axis=1)
    score_sc[:, pl.ds(off, pps * page)] = head_sum(_dot(qi, ki_t.astype(BF16)))

    @pl.when(j == pl.num_programs(1) - 1)
    def _():
        key = _sortable(score_sc[...])
        idx = lax.broadcasted_iota(I32, key.shape, 1)

        def count_ge(t):
            return jnp.sum(jnp.where(key >= t, 1, 0), axis=1, keepdims=True)

        thr, n_ge = _kth_largest(count_ge, (TOK_PAD, 1), n_top, key.shape[1])

        def search_cut():
            need = n_top - count_ge(thr + 1)

            def count_tie_below(jc):
                return jnp.sum(jnp.where(key == thr, jnp.where(idx < jc, 1, 0), 0), axis=1, keepdims=True)

            return _tie_cutoff(count_tie_below, (TOK_PAD, 1), need, idx_bits)

        cut = lax.cond(jnp.max(n_ge) > n_top, search_cut,
                       lambda: jnp.full((TOK_PAD, 1), key.shape[1], I32))
        tie = jnp.where(key == thr, jnp.where(idx <= cut, 0.0, -jnp.inf), -jnp.inf)
        sel = jnp.where(key > thr, 0.0, tie)
        bias_ref[0] = jnp.where(key > KEY_NEG_INF, sel, -jnp.inf)


def _idx_sample(page_table, ys, cache_t, layer, pps, n_top, n_tok):
    assert n_tok & (n_tok - 1) == 0 and TOK_PAD % n_tok == 0
    nseq, n_pages = page_table.shape
    page = cache_t.shape[3]
    n_cols = (n_pages + 1) * page
    cache = cache_t
    qiw = IDX_HEADS * IDX_DIM

    def page_spec(i):
        return pl.BlockSpec((None, None, IDX_DIM, page),
                            lambda b, j, pt, i=i: (layer, pt[b, j * pps + i], 0, 0))

    grid_spec = pltpu.PrefetchScalarGridSpec(
        num_scalar_prefetch=1,
        grid=(nseq, n_pages // pps),
        in_specs=[pl.BlockSpec((TOK_PAD, qiw), lambda b, j, pt: (b, COL_QI // qiw)),
                  pl.BlockSpec((TOK_PAD, LANES), lambda b, j, pt: (b, COL_KIW // LANES))]
                 + [page_spec(i) for i in range(pps)],
        out_specs=pl.BlockSpec((1, TOK_PAD, n_cols), lambda b, j, pt: (b, 0, 0)),
        scratch_shapes=[pltpu.VMEM((TOK_PAD, n_cols), F32)],
    )
    return pl.pallas_call(
        functools.partial(_idx_sample_kernel, pps=pps, page=page, n_pages=n_pages, n_top=n_top,
                          idx_bits=int(n_cols).bit_length(), n_tok=n_tok),
        out_shape=jax.ShapeDtypeStruct((nseq, TOK_PAD, n_cols), F32),
        grid_spec=grid_spec,
        compiler_params=_params(("parallel", "arbitrary")),
        name="idx_sample",
    )(page_table, ys, ys, *([cache] * pps))


def _dsa_sample_kernel(pt_ref, q0_ref, q1_ref, new_ref, bias_ref, *rest, pps, page, n_pages):
    page_refs = rest[:pps]
    o_ref = rest[pps]
    m_sc, l_sc, acc_sc = rest[pps + 1:]
    j = pl.program_id(1)
    q = _joint_queries([q0_ref[...], q1_ref[...]], HEAD_DIM ** -0.5)

    def visit(k, v, b8):
        bias = jnp.concatenate([b8] * (J_ROWS // TOK_PAD), axis=0)
        m, l, a = _softmax_block(q, k, v, bias, m_sc[...], l_sc[...], acc_sc[...])
        m_sc[...] = m
        l_sc[...] = l
        acc_sc[...] = a

    @pl.when(j == 0)
    def _():
        m_sc[...] = jnp.full(m_sc.shape, NEG_BIG, F32)
        l_sc[...] = jnp.zeros(l_sc.shape, F32)
        acc_sc[...] = jnp.zeros(acc_sc.shape, F32)
        visit(*_new_token_kv(new_ref, page), bias_ref[0, :, n_pages * page:])

    off = pl.multiple_of(j * (pps * page), pps * page)
    visit(*_gather_kv(page_refs, page), bias_ref[0, :, pl.ds(off, pps * page)])

    @pl.when(j == pl.num_programs(1) - 1)
    def _():
        _unstack_heads(o_ref, acc_sc[...] / l_sc[:, 0:1])


def _dsa_sample(page_table, ys, bias, cache, layer, pps):
    nseq, n_pages = page_table.shape
    rows = cache.shape[2]
    page = rows // KV_SLOTS
    n_cols = bias.shape[2]
    gw = GROUP * HEAD_DIM
    kvw = KV_SLOTS * HEAD_DIM

    def page_spec(i):
        return pl.BlockSpec((None, None, rows, HEAD_DIM),
                            lambda b, j, pt, i=i: (layer, pt[b, j * pps + i], 0, 0))

    grid_spec = pltpu.PrefetchScalarGridSpec(
        num_scalar_prefetch=1,
        grid=(nseq, n_pages // pps),
        in_specs=[pl.BlockSpec((TOK_PAD, gw), lambda b, j, pt: (b, COL_QB // gw)),
                  pl.BlockSpec((TOK_PAD, gw), lambda b, j, pt: (b, COL_QB // gw + 1)),
                  pl.BlockSpec((TOK_PAD, kvw), lambda b, j, pt: (b, COL_KVB // kvw)),
                  pl.BlockSpec((1, TOK_PAD, n_cols), lambda b, j, pt: (b, 0, 0))]
                 + [page_spec(i) for i in range(pps)],
        out_specs=pl.BlockSpec((TOK_PAD, N_HEADS * HEAD_DIM), lambda b, j, pt: (b, 0)),
        scratch_shapes=[pltpu.VMEM((J_ROWS, LANES), F32), pltpu.VMEM((J_ROWS, LANES), F32),
                        pltpu.VMEM((J_ROWS, J_LANES), F32)],
    )
    return pl.pallas_call(
        functools.partial(_dsa_sample_kernel, pps=pps, page=page, n_pages=n_pages),
        out_shape=jax.ShapeDtypeStruct((nseq * TOK_PAD, N_HEADS * HEAD_DIM), F32),
        grid_spec=grid_spec,
        compiler_params=_params(("parallel", "arbitrary")),
        name="dsa_sample",
    )(page_table, ys, ys, ys, bias, *([cache] * pps))


def _layer_norm(x, g, b):
    mu = jnp.mean(x, axis=-1, keepdims=True)
    xc = x - mu
    var = jnp.mean(xc * xc, axis=-1, keepdims=True)
    return xc * lax.rsqrt(var + LN_EPS) * g + b


def _merge_kernel(oa_ref, ob_ref, wa_ref, wb_ref, ga_ref, gb_ref, wo_ref, x_ref, g_ref, b_ref,
                  h_ref, hb_ref, acc_sc, *, alpha):
    kt = pl.program_id(1)

    @pl.when(kt == 0)
    def _():
        acc_sc[...] = jnp.zeros(acc_sc.shape, F32)

    oa = oa_ref[...].astype(BF16)
    ob = ob_ref[...].astype(BF16)
    mix = ga_ref[...] * _dot(oa, wa_ref[...]) + gb_ref[...] * _dot(ob, wb_ref[...])
    acc_sc[...] += _dot(mix.astype(BF16), wo_ref[...])

    @pl.when(kt == pl.num_programs(1) - 1)
    def _():
        h = _layer_norm(alpha * x_ref[...] + acc_sc[...], g_ref[...], b_ref[...])
        h_ref[...] = h
        hb_ref[...] = h.astype(BF16)


def _merge(oa, ob, y, x, wa, wb, wo, g, b, layer, alpha):
    rows = x.shape[0]
    tr = min(rows, 512)
    tk = 512
    kw = N_HEADS * HEAD_DIM
    return pl.pallas_call(
        functools.partial(_merge_kernel, alpha=alpha),
        out_shape=(jax.ShapeDtypeStruct((rows, D_MODEL), F32), jax.ShapeDtypeStruct((rows, D_MODEL), BF16)),
        grid=(rows // tr, D_MODEL // tk),
        in_specs=[pl.BlockSpec((tr, kw), lambda r, k: (r, 0)),
                  pl.BlockSpec((tr, kw), lambda r, k: (r, 0)),
                  pl.BlockSpec((None, kw, tk), lambda r, k: (layer, 0, k)),
                  pl.BlockSpec((None, kw, tk), lambda r, k: (layer, 0, k)),
                  pl.BlockSpec((tr, tk), lambda r, k: (r, COL_GA // tk + k)),
                  pl.BlockSpec((tr, tk), lambda r, k: (r, COL_GB // tk + k)),
                  pl.BlockSpec((None, tk, D_MODEL), lambda r, k: (layer, k, 0)),
                  pl.BlockSpec((tr, D_MODEL), lambda r, k: (r, 0)),
                  pl.BlockSpec((None, 1, D_MODEL), lambda r, k: (layer, 0, 0)),
                  pl.BlockSpec((None, 1, D_MODEL), lambda r, k: (layer, 0, 0))],
        out_specs=(pl.BlockSpec((tr, D_MODEL), lambda r, k: (r, 0)),
                   pl.BlockSpec((tr, D_MODEL), lambda r, k: (r, 0))),
        scratch_shapes=[pltpu.VMEM((tr, D_MODEL), F32)],
        compiler_params=_params(("parallel", "arbitrary")),
        name="merge_ln",
    )(oa, ob, wa, wb, y, y, wo, x, g, b)


def _gelu_tanh(x):
    return 0.5 * x * (1.0 + jnp.tanh(np.sqrt(2.0 / np.pi) * (x + 0.044715 * (x * x * x))))


HALO = 16
FFN_ROWS = 1024
FFN_VMEM_LIMIT = 61 * 1024 * 1024


def _ffn_kernel(*refs, alpha, tr, seq_len, blocks_per_seq, prompt_mode):
    if prompt_mode:
        (hb_ref, halo_ref, w1a_ref, w1u_ref, cw_ref, cb_ref, w2_ref, h_ref, g_ref, b_ref,
         h2_ref, h2b_ref, a_ref) = refs
    else:
        (hb_ref, s1_ref, s2_ref, w1a_ref, w1u_ref, cw_ref, cb_ref, w2_ref, h_ref, g_ref, b_ref,
         h2_ref, h2b_ref, a_ref) = refs
    r = pl.program_id(0)
    ft = pl.program_id(1)
    acc_sc = h2_ref

    @pl.when(ft == 0)
    def _():
        acc_sc[...] = jnp.zeros(acc_sc.shape, F32)

    hb = hb_ref[...]
    a = _dot(hb, w1a_ref[...])
    up = _dot(hb, w1u_ref[...])
    row = lax.broadcasted_iota(I32, a.shape, 0)
    p1 = pltpu.roll(a, 1, 0)
    p2 = pltpu.roll(a, 2, 0)
    if prompt_mode:
        a_halo = _dot(halo_ref[...], w1a_ref[...])
        keep = jnp.where(r % blocks_per_seq == 0, 0.0, 1.0)
        h6 = a_halo[HALO - 2:HALO - 1, :] * keep
        h7 = a_halo[HALO - 1:HALO, :] * keep
        p1 = jnp.where(row == 0, h7, p1)
        p2 = jnp.where(row == 0, h6, jnp.where(row == 1, h7, p2))
        a_ref[...] = a[tr - SUBLANES:, :]
    else:
        t = row & (seq_len - 1)
        p1 = jnp.where(t == 0, s1_ref[...], p1)
        p2 = jnp.where(t < 2, s2_ref[...], p2)
        a_ref[...] = a
    c = cb_ref[...] + cw_ref[0:1, :] * p2 + cw_ref[1:2, :] * p1 + cw_ref[2:3, :] * a
    hmid = (_gelu_tanh(c) * up).astype(BF16)
    acc_sc[...] += _dot(hmid, w2_ref[...])

    @pl.when(ft == pl.num_programs(1) - 1)
    def _():
        h2 = _layer_norm(alpha * h_ref[...] + acc_sc[...], g_ref[...], b_ref[...])
        h2_ref[...] = h2
        h2b_ref[...] = h2.astype(BF16)


def _ffn(h, hb, w1a, w1u, cw, cb, w2, g, b, layer, alpha, seq_len, state=None):
    rows = h.shape[0]
    prompt_mode = state is None
    tr = min(seq_len, FFN_ROWS) if prompt_mode else rows
    n_r = rows // tr
    n_f = D_FF_PAD // TF
    common_w = [pl.BlockSpec((None, D_MODEL, TF), lambda r, f: (layer, 0, f)),
                pl.BlockSpec((None, D_MODEL, TF), lambda r, f: (layer, 0, f)),
                pl.BlockSpec((None, SUBLANES, TF), lambda r, f: (layer, 0, f)),
                pl.BlockSpec((None, 1, TF), lambda r, f: (layer, 0, f)),
                pl.BlockSpec((None, TF, D_MODEL), lambda r, f: (layer, f, 0)),
                pl.BlockSpec((tr, D_MODEL), lambda r, f: (r, 0), pipeline_mode=pl.Buffered(1)),
                pl.BlockSpec((None, 1, D_MODEL), lambda r, f: (layer, 0, 0)),
                pl.BlockSpec((None, 1, D_MODEL), lambda r, f: (layer, 0, 0))]
    if prompt_mode:
        assert seq_len % tr == 0 and tr % HALO == 0
        per = tr // HALO
        extra_specs = [pl.BlockSpec((HALO, D_MODEL), lambda r, f: (jnp.maximum(r * per - 1, 0), 0))]
        extra = [hb]
        a_rows, a_blk = n_r * SUBLANES, SUBLANES
    else:
        assert seq_len & (seq_len - 1) == 0 and seq_len >= CONV_W - 1
        extra_specs = [pl.BlockSpec((tr, TF), lambda r, f: (0, f)),
                       pl.BlockSpec((tr, TF), lambda r, f: (0, f))]
        extra = list(state)
        a_rows, a_blk = rows, tr
    return pl.pallas_call(
        functools.partial(_ffn_kernel, alpha=alpha, tr=tr, seq_len=seq_len,
                          blocks_per_seq=max(seq_len // tr, 1), prompt_mode=prompt_mode),
        out_shape=(jax.ShapeDtypeStruct((rows, D_MODEL), F32), jax.ShapeDtypeStruct((rows, D_MODEL), BF16),
                   jax.ShapeDtypeStruct((a_rows, D_FF_PAD), F32)),
        grid=(n_r, n_f),
        in_specs=[pl.BlockSpec((tr, D_MODEL), lambda r, f: (r, 0), pipeline_mode=pl.Buffered(1))]
                 + extra_specs + common_w,
        out_specs=(pl.BlockSpec((tr, D_MODEL), lambda r, f: (r, 0)),
                   pl.BlockSpec((tr, D_MODEL), lambda r, f: (r, 0), pipeline_mode=pl.Buffered(1)),
                   pl.BlockSpec((a_blk, TF), lambda r, f: (r, f))),
        compiler_params=_params(("parallel", "arbitrary"), FFN_VMEM_LIMIT),
        name="conv_ffn_ln",
    )(hb, *extra, w1a, w1u, cw, cb, w2, h, g, b)


def _ple_kernel(hb_ref, wg_ref, p_ref, wp_ref, h_ref, o_ref, ob_ref):
    gate = jax.nn.sigmoid(_dot(hb_ref[...], wg_ref[...]))
    out = h_ref[...] + gate * _dot(p_ref[...], wp_ref[...])
    o_ref[...] = out
    ob_ref[...] = out.astype(BF16)


def _ple(h2, h2b, pb, wg, wp, layer):
    rows = h2.shape[0]
    tr = min(rows, 1024)
    tn = 512
    return pl.pallas_call(
        _ple_kernel,
        out_shape=(jax.ShapeDtypeStruct((rows, D_MODEL), F32), jax.ShapeDtypeStruct((rows, D_MODEL), BF16)),
        grid=(rows // tr, D_MODEL // tn),
        in_specs=[pl.BlockSpec((tr, D_MODEL), lambda r, n: (r, 0)),
                  pl.BlockSpec((None, D_MODEL, tn), lambda r, n: (layer, 0, n)),
                  pl.BlockSpec((None, tr, PLE_DIM), lambda r, n: (layer, r, 0)),
                  pl.BlockSpec((None, PLE_DIM, tn), lambda r, n: (layer, 0, n)),
                  pl.BlockSpec((tr, tn), lambda r, n: (r, n))],
        out_specs=(pl.BlockSpec((tr, tn), lambda r, n: (r, n)),
                   pl.BlockSpec((tr, tn), lambda r, n: (r, n))),
        compiler_params=_params(("parallel", "arbitrary")),
        name="ple_gate",
    )(h2b, wg, pb, wp, h2)


IN_WIDTH = COL_KIW + IDX_DIM + IDX_HEADS + 2 * D_MODEL
PACK_ROWS = 128


SMALL_COLS = IDX_DIM + IDX_HEADS
KIW_TILE = COL_KIW // TN


def _pack_w_in_kernel(prev_ref, cur_ref, o_ref):
    n = pl.program_id(1)

    @pl.when(n < KIW_TILE)
    def _():
        o_ref[...] = cur_ref[...].T.astype(BF16)

    @pl.when(n == KIW_TILE)
    def _():
        t = jnp.concatenate([cur_ref[:SMALL_COLS, :], jnp.zeros((TN - SMALL_COLS, D_MODEL), F32)], axis=0)
        o_ref[...] = t.T.astype(BF16)

    @pl.when(n > KIW_TILE)
    def _():
        t = jnp.concatenate([prev_ref[SMALL_COLS:, :], cur_ref[:SMALL_COLS, :]], axis=0)
        o_ref[...] = t.T.astype(BF16)


def _pack_w_in(w_in):
    depth, rows, width = w_in.shape
    assert width == IN_WIDTH and rows == D_MODEL and SMALL_COLS % 16 == 0
    w_t = jnp.swapaxes(w_in, 1, 2)
    return pl.pallas_call(
        _pack_w_in_kernel,
        out_shape=jax.ShapeDtypeStruct((depth, rows, IN_COLS), BF16),
        grid=(depth, N_IN_TILES),
        in_specs=[pl.BlockSpec((None, TN, D_MODEL), lambda l, n: (l, jnp.maximum(n - 1, KIW_TILE), 0)),
                  pl.BlockSpec((None, TN, D_MODEL), lambda l, n: (l, n, 0))],
        out_specs=pl.BlockSpec((None, D_MODEL, TN), lambda l, n: (l, 0, n)),
        compiler_params=_params(("parallel", "arbitrary")),
        name="pack_w_in",
    )(w_t, w_t)


def _pack_ffn_in_kernel(w_ref, a_ref, u_ref):
    x = w_ref[...]
    zeros = jnp.zeros((x.shape[0], D_FF_PAD - D_FF), BF16)
    a_ref[:, :D_FF] = x[:, :D_FF].astype(BF16)
    a_ref[:, D_FF:] = zeros
    u_ref[:, :D_FF] = x[:, D_FF:].astype(BF16)
    u_ref[:, D_FF:] = zeros


def _pack_ffn_in(w_ffn_in):
    depth, rows, width = w_ffn_in.shape
    out = jax.ShapeDtypeStruct((depth, rows, D_FF_PAD), BF16)
    spec = pl.BlockSpec((None, PACK_ROWS, D_FF_PAD), lambda l, r: (l, r, 0))
    return pl.pallas_call(
        _pack_ffn_in_kernel,
        out_shape=(out, out),
        grid=(depth, rows // PACK_ROWS),
        in_specs=[pl.BlockSpec((None, PACK_ROWS, width), lambda l, r: (l, r, 0))],
        out_specs=(spec, spec),
        compiler_params=_params(("parallel", "parallel")),
        name="pack_ffn_in",
    )(w_ffn_in)


def _pack_ffn_out_kernel(w_ref, o_ref):
    row = pl.program_id(1) * TF + lax.broadcasted_iota(I32, w_ref.shape, 0)
    o_ref[...] = jnp.where(row < D_FF, w_ref[...], 0.0).astype(BF16)


def _pack_ffn_out(w_ffn_out):
    depth, _, cols = w_ffn_out.shape
    return pl.pallas_call(
        _pack_ffn_out_kernel,
        out_shape=jax.ShapeDtypeStruct((depth, D_FF_PAD, cols), BF16),
        grid=(depth, D_FF_PAD // TF),
        in_specs=[pl.BlockSpec((None, TF, cols), lambda l, r: (l, r, 0))],
        out_specs=pl.BlockSpec((None, TF, cols), lambda l, r: (l, r, 0)),
        compiler_params=_params(("parallel", "parallel")),
        name="pack_ffn_out",
    )(w_ffn_out)


def _rope_tables(pos):
    pos = pos.astype(F32)[:, None]

    def table(head_dim):
        rot = head_dim // 4
        half = rot // 2
        inv = ROPE_THETA ** (-(2.0 * jnp.arange(half, dtype=F32)) / rot)
        ang = pos * inv[None, :]
        cos, sin = jnp.cos(ang), jnp.sin(ang)
        ones = jnp.ones((pos.shape[0], head_dim - rot), F32)
        c = jnp.concatenate([cos, cos, ones], axis=1)
        s = jnp.concatenate([-sin, sin, 0.0 * ones], axis=1)
        reps = LANES // head_dim
        return jnp.tile(c, (1, reps)), jnp.tile(s, (1, reps))

    c128, s128 = table(HEAD_DIM)
    c64, s64 = table(IDX_DIM)
    return c128, s128, c64, s64


def kernel(x_prompt, x_sample, cache_sb_kv, cache_dsa_kv, cache_idx_k, state_ffn_conv, page_table,
           p_prompt, p_sample, w_in, w_branch_sb, w_branch_dsa, w_out, ln1_g, ln1_b, w_ffn_in,
           ffn_conv_w, ffn_conv_b, w_ffn_out, ln2_g, ln2_b, w_ple_gate, w_ple_proj):
    batch, seq = x_prompt.shape[:2]
    nseq, n_tok = x_sample.shape[:2]
    depth = w_in.shape[0]
    n_pool, page = cache_sb_kv.shape[1:3]
    n_pages = page_table.shape[1]
    past_len = n_pages * page
    alpha = (2 * depth) ** 0.25
    kv_w = 2 * N_KV * HEAD_DIM
    top_s = max(1, min(TOPK_MAX, (past_len + n_tok) // 4))
    pps = min(32, n_pages)
    pps_idx = min(64, n_pages)

    w_in_p = _pack_w_in(w_in)
    wa = w_branch_sb.astype(BF16)
    wb = w_branch_dsa.astype(BF16)
    wo = w_out.astype(BF16)
    ff_pad = D_FF_PAD - D_FF
    w1a, w1u = _pack_ffn_in(w_ffn_in)
    w2 = _pack_ffn_out(w_ffn_out)
    cw = jnp.pad(ffn_conv_w, ((0, 0), (0, SUBLANES - CONV_W), (0, ff_pad)))
    cb = jnp.pad(ffn_conv_b, ((0, 0), (0, ff_pad)))[:, None, :]
    wg = w_ple_gate.astype(BF16)
    wp = w_ple_proj.astype(BF16)
    g1, b1 = ln1_g[:, None, :], ln1_b[:, None, :]
    g2, b2 = ln2_g[:, None, :], ln2_b[:, None, :]

    tabs_p = _rope_tables(jnp.tile(jnp.arange(seq, dtype=jnp.int32), batch))
    tabs_s = _rope_tables(jnp.tile(past_len + jnp.arange(TOK_PAD, dtype=jnp.int32), nseq))

    sb_pages = cache_sb_kv.reshape(depth, n_pool, page * KV_SLOTS, HEAD_DIM)
    dsa_pages = cache_dsa_kv.reshape(depth, n_pool, page * KV_SLOTS, HEAD_DIM)
    idx_pages_t = jnp.swapaxes(cache_idx_k, 2, 3)

    assert n_tok <= TOK_PAD
    tok_pad = ((0, 0), (0, TOK_PAD - n_tok), (0, 0))
    xp = x_prompt.reshape(batch * seq, D_MODEL)
    xs = jnp.pad(x_sample, tok_pad).reshape(nseq * TOK_PAD, D_MODEL)
    xpb, xsb = xp.astype(BF16), xs.astype(BF16)
    ppb = p_prompt.reshape(depth, batch * seq, PLE_DIM).astype(BF16)
    psb = jnp.pad(p_sample, ((0, 0),) + tok_pad).reshape(depth, nseq * TOK_PAD, PLE_DIM).astype(BF16)
    st = jnp.pad(state_ffn_conv, ((0, 0), (0, 0), (0, 0), (0, ff_pad)))
    conv_s1 = jnp.repeat(st[:, :, 1], TOK_PAD, axis=1)
    conv_s2 = jnp.pad(st, ((0, 0), (0, 0), (0, TOK_PAD - (CONV_W - 1)), (0, 0))).reshape(
        depth, nseq * TOK_PAD, D_FF_PAD)

    def new_row_buffers(rows):
        kv_shape = (depth, rows * KV_SLOTS, HEAD_DIM)
        return jnp.zeros(kv_shape, F32), jnp.zeros(kv_shape, F32), jnp.zeros((depth, rows, LANES), F32)

    new_p = new_row_buffers(batch * seq)
    new_s = new_row_buffers(nseq * TOK_PAD)
    outs = {k: [] for k in ("conv_p", "conv_s")}
    for l in range(depth):
        y, *new_p = _project(xpb, w_in_p, l, tabs_p, new_p)
        oa = _sb_prompt(y, batch, seq)
        ob = _dsa_prompt(y, batch, seq)
        h, hb = _merge(oa, ob, y, xp, wa, wb, wo, g1, b1, l, alpha)
        h2, h2b, a_tail = _ffn(h, hb, w1a, w1u, cw, cb, w2, g2, b2, l, alpha, seq)
        xp, xpb = _ple(h2, h2b, ppb, wg, wp, l)
        tails = a_tail.reshape(batch, -1, SUBLANES, D_FF_PAD)[:, -1, SUBLANES - (CONV_W - 1):, :D_FF]
        outs["conv_p"].append(tails)

        ys, *new_s = _project(xsb, w_in_p, l, tabs_s, new_s)
        oa_s = _sb_sample(page_table, ys, sb_pages, l, pps)
        bias = _idx_sample(page_table, ys, idx_pages_t, l, pps_idx, top_s, n_tok)
        ob_s = _dsa_sample(page_table, ys, bias, dsa_pages, l, pps)
        hs, hsb = _merge(oa_s, ob_s, ys, xs, wa, wb, wo, g1, b1, l, alpha)
        h2s, h2sb, a_s = _ffn(hs, hsb, w1a, w1u, cw, cb, w2, g2, b2, l, alpha,
                              TOK_PAD, state=(conv_s1[l], conv_s2[l]))
        xs, xsb = _ple(h2s, h2sb, psb, wg, wp, l)
        outs["conv_s"].append(a_s.reshape(nseq, TOK_PAD, D_FF_PAD)[:, n_tok - (CONV_W - 1):n_tok, :D_FF])

    sb_p, dsa_p, kiw_p = new_p
    sb_s, dsa_s, kiw_s = [b.reshape(depth, nseq, TOK_PAD, -1)[:, :, :n_tok] for b in new_s]
    kv_shape = (2, N_KV, HEAD_DIM)
    return (xp.reshape(batch, seq, D_MODEL), xs.reshape(nseq, TOK_PAD, D_MODEL)[:, :n_tok],
            sb_p.reshape(depth, batch, seq, *kv_shape), dsa_p.reshape(depth, batch, seq, *kv_shape),
            kiw_p[..., :IDX_DIM].reshape(depth, batch, seq, IDX_DIM), jnp.stack(outs["conv_p"]),
            sb_s.reshape(depth, nseq, n_tok, *kv_shape), dsa_s.reshape(depth, nseq, n_tok, *kv_shape),
            kiw_s[..., :IDX_DIM], jnp.stack(outs["conv_s"]))
```
